```python
import jax, jax.numpy as jnp
from jax import lax
import numpy as np

D_MODEL = 1024
BATCH = 8
SEQ = 4096
DEPTH = 1

ATT_HEAD_DIM = 128
ATT_GROUPS = ((128, 1), (512, 4), (2048, 16))
ATT_N_GROUPS = 3
ATT_KV_HEADS = 4
ATT_Q_HEADS = ATT_N_GROUPS * ATT_KV_HEADS
ATT_Q_WIDTH = ATT_Q_HEADS * ATT_HEAD_DIM
ATT_KV_WIDTH = ATT_KV_HEADS * ATT_HEAD_DIM
ATT_BLOCK = 128
ROPE_THETA = 10000.0
M_HEADS = 4
M_HEAD_DIM = 256
M_WIDTH = M_HEADS * M_HEAD_DIM
M_CONV = 4
M_CHUNK = 128
IN_WIDTH = ATT_Q_WIDTH + 3 * ATT_KV_WIDTH + 3 * M_WIDTH + 2 * D_MODEL
LN_EPS = 1e-5

kernel_name = 'hybrid_dilated_attn_mlstm_gated_merge'


def _layer_norm(x):
    xf = x.astype(jnp.float32)
    mu = jnp.mean(xf, axis=-1, keepdims=True)
    var = jnp.mean(jnp.square(xf - mu), axis=-1, keepdims=True)
    return (xf - mu) * lax.rsqrt(var + LN_EPS)


def _rope(x, pos):
    half = x.shape[-1] // 2
    inv = jnp.power(ROPE_THETA, -jnp.arange(half, dtype=jnp.float32) / half)
    ang = pos.astype(jnp.float32)[..., None] * inv
    cos = jnp.cos(ang)[:, :, None, :]
    sin = jnp.sin(ang)[:, :, None, :]
    x1, x2 = x[..., :half], x[..., half:]
    return jnp.concatenate([x1 * cos - x2 * sin, x2 * cos + x1 * sin], axis=-1)


def _dilated_window_attention(q, k, v, dilation, n_steps):
    B, S, H, E = q.shape
    span = dilation * ATT_BLOCK
    s_pad = -(-S // span) * span
    pad = ((0, 0), (0, s_pad - S), (0, 0), (0, 0))
    q, k, v = jnp.pad(q, pad), jnp.pad(k, pad), jnp.pad(v, pad)
    nb = s_pad // span
    shape = (B, nb, ATT_BLOCK, dilation, H, E)
    qb, kb, vb = q.reshape(shape), k.reshape(shape), v.reshape(shape)

    def with_prev(a):
        prev = jnp.pad(a[:, :-1], ((0, 0), (1, 0), (0, 0), (0, 0), (0, 0), (0, 0)))
        return jnp.concatenate([prev, a], axis=2)

    kk, vv = with_prev(kb), with_prev(vb)
    s = jnp.einsum('bnidhe,bnjdhe->bndhij', qb, kk)
    i = jnp.arange(ATT_BLOCK)[:, None]
    j = jnp.arange(2 * ATT_BLOCK)[None, :]
    dist = ATT_BLOCK + i - j
    first = (jnp.arange(nb) == 0)[:, None, None]
    valid = (dist >= 0) & (dist <= n_steps) & ~(first & (j < ATT_BLOCK))
    s = jnp.where(valid[None, :, None, None], s, -jnp.inf)
    m = jnp.max(s, axis=-1, keepdims=True)
    p = jnp.exp(s - m)
    den = jnp.sum(p, axis=-1)
    o = jnp.einsum('bndhij,bnjdhe->bnidhe', p, vv) / jnp.transpose(den, (0, 1, 4, 2, 3))[..., None]
    lse = jnp.transpose(m[..., 0] + jnp.log(den), (0, 1, 4, 2, 3))
    o = o.reshape(B, s_pad, H, E)[:, :S]
    lse = lse.reshape(B, s_pad, H)[:, :S]
    return o, lse


def _causal_conv(x, w, b):
    C = x.shape[-1]
    y = lax.conv_general_dilated(
        x.astype(jnp.float32), w.astype(jnp.float32)[:, None, :], window_strides=(1,),
        padding=[(M_CONV - 1, 0)], dimension_numbers=('NWC', 'WIO', 'NWC'),
        feature_group_count=C)
    return y + b.astype(jnp.float32)


def _mlstm_chunkwise(q, k, v, log_i, log_f):
    B, S, H, E = q.shape
    nc = S // M_CHUNK

    def chunks(a):
        return a.reshape(B, nc, M_CHUNK, H, E).transpose(0, 3, 1, 2, 4)

    def gchunks(a):
        return a.reshape(B, nc, M_CHUNK, H).transpose(0, 3, 1, 2)

    q, k, v = chunks(q), chunks(k), chunks(v)
    li, lf = gchunks(log_i), gchunks(log_f)
    b = jnp.cumsum(lf, axis=-1)
    b_last = b[..., -1]
    w_src = b_last[..., None] - b + li
    a = jnp.max(w_src, axis=-1)
    e_src = jnp.exp(w_src - a[..., None])
    c_loc = jnp.einsum('bhcs,bhcsd,bhcse->bhcde', e_src, v, k)
    n_loc = jnp.einsum('bhcs,bhcse->bhce', e_src, k)

    def step(carry, inp):
        c_st, n_st, m_st = carry
        c_l, n_l, a_c, bl, q_c = inp
        num_inter = jnp.einsum('bhde,bhte->bhtd', c_st, q_c)
        den_inter = jnp.einsum('bhe,bhte->bht', n_st, q_c)
        m_new = jnp.maximum(bl + m_st, a_c)
        decay = jnp.exp(bl + m_st - m_new)
        gain = jnp.exp(a_c - m_new)
        c_st = decay[..., None, None] * c_st + gain[..., None, None] * c_l
        n_st = decay[..., None] * n_st + gain[..., None] * n_l
        return (c_st, n_st, m_new), (num_inter, den_inter, m_st)

    init = (jnp.zeros((B, H, E, E), jnp.float32), jnp.zeros((B, H, E), jnp.float32),
            jnp.zeros((B, H), jnp.float32))
    xs = (jnp.moveaxis(c_loc, 2, 0), jnp.moveaxis(n_loc, 2, 0), jnp.moveaxis(a, 2, 0),
          jnp.moveaxis(b_last, 2, 0), jnp.moveaxis(q, 2, 0))
    _, (num_inter, den_inter, m_prev) = lax.scan(step, init, xs)
    num_inter = jnp.moveaxis(num_inter, 0, 2)
    den_inter = jnp.moveaxis(den_inter, 0, 2)
    m_prev = jnp.moveaxis(m_prev, 0, 2)

    causal = jnp.tril(jnp.ones((M_CHUNK, M_CHUNK), dtype=bool))
    dmat = jnp.where(causal, b[..., :, None] - b[..., None, :] + li[..., None, :], -jnp.inf)
    g = b + m_prev[..., None]
    m_t = jnp.maximum(g, jnp.max(dmat, axis=-1))
    smat = jnp.einsum('bhcte,bhcse->bhcts', q, k) * jnp.exp(dmat - m_t[..., None])
    w_inter = jnp.exp(g - m_t)
    num = w_inter[..., None] * num_inter + jnp.einsum('bhcts,bhcsd->bhctd', smat, v)
    den = w_inter * den_inter + jnp.sum(smat, axis=-1)
    h = num / jnp.maximum(jnp.abs(den), jnp.exp(-m_t))[..., None]
    return h.transpose(0, 2, 3, 1, 4).reshape(B, S, H, E)


def _hybrid_layer(x, c, positions, w_ada, b_ada, w_in, conv_w, conv_b, w_qm, w_km, w_vm,
                  w_if, b_if, mh_norm_w, skip_m, w_pa, w_pm, w_out, ln_g, ln_b):
    B, S, _ = x.shape
    dt = x.dtype
    f32 = jnp.float32
    alpha = (2.0 * DEPTH) ** 0.25
    ada = (jax.nn.silu(c) @ w_ada + b_ada).astype(f32)
    shift, scale, gate = jnp.split(ada, 3, axis=-1)
    h = (_layer_norm(x) * (1.0 + scale[:, None]) + shift[:, None]).astype(dt)

    proj = h @ w_in
    sizes = [ATT_Q_WIDTH, ATT_KV_WIDTH, ATT_KV_WIDTH, ATT_KV_WIDTH, M_WIDTH, M_WIDTH, M_WIDTH, D_MODEL]
    offs = [int(o) for o in np.cumsum(sizes)]
    q_a, k_a, v_a, z_a, x_m, z_m, o_m, g_a, g_m = jnp.split(proj, offs, axis=-1)

    q = _rope(q_a.reshape(B, S, ATT_Q_HEADS, ATT_HEAD_DIM).astype(f32), positions) * (ATT_HEAD_DIM ** -0.5)
    q = q.reshape(B, S, ATT_N_GROUPS, ATT_KV_HEADS, ATT_HEAD_DIM)
    k = _rope(k_a.reshape(B, S, ATT_KV_HEADS, ATT_HEAD_DIM).astype(f32), positions)
    v = v_a.reshape(B, S, ATT_KV_HEADS, ATT_HEAD_DIM).astype(f32)
    outs, lses = [], []
    for g_idx, (window, dil) in enumerate(ATT_GROUPS):
        o_g, l_g = _dilated_window_attention(q[:, :, g_idx], k, v, dil, window // dil)
        outs.append(o_g)
        lses.append(l_g)
    wgt = jax.nn.softmax(jnp.stack(lses, axis=0), axis=0)
    o_att = jnp.sum(wgt[..., None] * jnp.stack(outs, axis=0), axis=0).reshape(B, S, ATT_KV_WIDTH)
    y_att = (o_att * jax.nn.silu(z_a.astype(f32))).astype(dt) @ w_pa

    xc = jax.nn.silu(_causal_conv(x_m, conv_w, conv_b))
    xch = xc.reshape(B, S, M_HEADS, M_HEAD_DIM)
    qm = jnp.einsum('bshd,hde->bshe', xch, w_qm.astype(f32))
    km = jnp.einsum('bshd,hde->bshe', xch, w_km.astype(f32)) * (M_HEAD_DIM ** -0.5)
    vm = jnp.einsum('bshd,hde->bshe', x_m.reshape(B, S, M_HEADS, M_HEAD_DIM).astype(f32), w_vm.astype(f32))
    qkv = jnp.concatenate([qm.reshape(B, S, M_WIDTH), km.reshape(B, S, M_WIDTH), vm.reshape(B, S, M_WIDTH)], axis=-1)
    gates = qkv @ w_if.astype(f32) + b_if.astype(f32)
    log_i = gates[..., :M_HEADS]
    log_f = jax.nn.log_sigmoid(gates[..., M_HEADS:])
    hm = _mlstm_chunkwise(qm, km, vm, log_i, log_f)
    hm = jax.nn.sigmoid(o_m.astype(f32)).reshape(B, S, M_HEADS, M_HEAD_DIM) * hm
    hm = _layer_norm(hm) * mh_norm_w.astype(f32).reshape(M_HEADS, M_HEAD_DIM)
    hm = hm.reshape(B, S, M_WIDTH) + skip_m.astype(f32) * xc
    y_m = (hm * jax.nn.silu(z_m.astype(f32))).astype(dt) @ w_pm

    merged = jax.nn.sigmoid(g_a) * y_att + jax.nn.sigmoid(g_m) * y_m
    out = (merged @ w_out).astype(f32)
    res = alpha * x.astype(f32) + gate[:, None] * out
    return (_layer_norm(res) * ln_g.astype(f32) + ln_b.astype(f32)).astype(dt)


def setup_inputs(seed: int = 0) -> dict:
    key = jax.random.key(seed)
    ks = jax.random.split(key, 24)
    f32 = jnp.float32
    beta = (8.0 * DEPTH) ** -0.25

    def nrm(k, shape, scale):
        return jax.random.normal(k, shape, f32) * scale

    x = jax.random.normal(ks[0], (BATCH, SEQ, D_MODEL), f32)
    c = jax.random.normal(ks[1], (BATCH, D_MODEL), f32)
    offset = jax.random.randint(ks[2], (BATCH, 1), 0, 1024, dtype=jnp.int32)
    positions = (jnp.arange(SEQ, dtype=jnp.int32)[None, :] + offset).astype(jnp.int32)
    w_ada = nrm(ks[3], (DEPTH, D_MODEL, 3 * D_MODEL), D_MODEL ** -0.5)
    b_ada = nrm(ks[4], (DEPTH, 3 * D_MODEL), 0.02)
    w_in = nrm(ks[5], (DEPTH, D_MODEL, IN_WIDTH), D_MODEL ** -0.5)
    conv_w = nrm(ks[6], (DEPTH, M_CONV, M_WIDTH), M_CONV ** -0.5)
    conv_b = nrm(ks[7], (DEPTH, M_WIDTH), 0.02)
    w_qm = nrm(ks[8], (DEPTH, M_HEADS, M_HEAD_DIM, M_HEAD_DIM), M_HEAD_DIM ** -0.5)
    w_km = nrm(ks[9], (DEPTH, M_HEADS, M_HEAD_DIM, M_HEAD_DIM), M_HEAD_DIM ** -0.5)
    w_vm = nrm(ks[10], (DEPTH, M_HEADS, M_HEAD_DIM, M_HEAD_DIM), M_HEAD_DIM ** -0.5)
    w_if = nrm(ks[11], (DEPTH, 3 * M_WIDTH, 2 * M_HEADS), (3 * M_WIDTH) ** -0.5)
    b_i = nrm(ks[12], (DEPTH, M_HEADS), 0.1)
    b_f = 3.0 + nrm(ks[13], (DEPTH, M_HEADS), 0.5)
    b_if = jnp.concatenate([b_i, b_f], axis=-1)
    mh_norm_w = 1.0 + nrm(ks[14], (DEPTH, M_WIDTH), 0.02)
    skip_m = 1.0 + nrm(ks[15], (DEPTH, M_WIDTH), 0.02)
    w_pa = nrm(ks[16], (DEPTH, ATT_KV_WIDTH, D_MODEL), beta * ATT_KV_WIDTH ** -0.5)
    w_pm = nrm(ks[17], (DEPTH, M_WIDTH, D_MODEL), beta * M_WIDTH ** -0.5)
    w_out = nrm(ks[18], (DEPTH, D_MODEL, D_MODEL), beta * D_MODEL ** -0.5)
    ln_g = 1.0 + nrm(ks[19], (DEPTH, D_MODEL), 0.02)
    ln_b = nrm(ks[20], (DEPTH, D_MODEL), 0.02)
    return {'x': x, 'c': c, 'positions': positions, 'w_ada': w_ada, 'b_ada': b_ada,
            'w_in': w_in, 'conv_w': conv_w, 'conv_b': conv_b, 'w_qm': w_qm, 'w_km': w_km,
            'w_vm': w_vm, 'w_if': w_if, 'b_if': b_if, 'mh_norm_w': mh_norm_w, 'skip_m': skip_m,
            'w_pa': w_pa, 'w_pm': w_pm, 'w_out': w_out, 'ln_g': ln_g, 'ln_b': ln_b}


def reference(x, c, positions, w_ada, b_ada, w_in, conv_w, conv_b, w_qm, w_km, w_vm,
              w_if, b_if, mh_norm_w, skip_m, w_pa, w_pm, w_out, ln_g, ln_b):
    for l in range(DEPTH):
        x = _hybrid_layer(x, c, positions, w_ada[l], b_ada[l], w_in[l], conv_w[l], conv_b[l],
                          w_qm[l], w_km[l], w_vm[l], w_if[l], b_if[l], mh_norm_w[l], skip_m[l],
                          w_pa[l], w_pm[l], w_out[l], ln_g[l], ln_b[l])
    return x
```

```python
import functools

import jax
import jax.numpy as jnp
from jax import lax
from jax.experimental import pallas as pl
from jax.experimental.pallas import tpu as pltpu

F32 = jnp.float32
BF16 = jnp.bfloat16

ATT_HEAD_DIM = 128
ATT_GROUPS = ((128, 1), (512, 4), (2048, 16))
ATT_KV_HEADS = 4
ATT_KV_WIDTH = ATT_KV_HEADS * ATT_HEAD_DIM
ATT_BLOCK = 128
ROPE_THETA = 10000.0
M_HEADS = 4
M_HEAD_DIM = 256
M_WIDTH = M_HEADS * M_HEAD_DIM
M_CONV = 4
M_CHUNK = 128
LN_EPS = 1e-5
NEG = -1e30

V7X_VMEM_LIMIT_BYTES = 56 * 1024 * 1024
LANES = 128

INPROJ_TM = 1024
INPROJ_TN = 512
ATT_SUB = 4
MPRE_TM = 512
MLSTM_TM = 512
POST_TM = 512
CONV_HALO = 16


def _silu(v):
    return v * jax.nn.sigmoid(v)


def _params(*sem):
    return pltpu.CompilerParams(dimension_semantics=sem, vmem_limit_bytes=V7X_VMEM_LIMIT_BYTES)


def _ada_kernel(c_ref, w_ref, b_ref, o_ref):
    sc = _silu(c_ref[...])
    o_ref[...] = jnp.dot(sc, w_ref[...], precision=lax.Precision.HIGHEST,
                         preferred_element_type=F32) + b_ref[...]


def _ada(c, w_ada, b_ada):
    bsz, d = c.shape
    n = w_ada.shape[1]
    return pl.pallas_call(
        _ada_kernel,
        grid=(n // d,),
        in_specs=[pl.BlockSpec((bsz, d), lambda j: (0, 0)),
                  pl.BlockSpec((d, d), lambda j: (0, j)),
                  pl.BlockSpec((1, d), lambda j: (0, j))],
        out_specs=pl.BlockSpec((bsz, d), lambda j: (0, j)),
        out_shape=jax.ShapeDtypeStruct((bsz, n), F32),
        compiler_params=_params("arbitrary"),
        name="ada",
    )(c, w_ada, b_ada.reshape(1, n))


def _inproj_kernel(x_ref, ada_ref, pos_ref, tab_ref, w_ref,
                   q0_ref, q1_ref, q2_ref, kv_ref, rest_ref,
                   h_scr, cos_scr, sin_scr, *, d_model):
    j = pl.program_id(1)

    @pl.when(j == 0)
    def _():
        x = x_ref[...]
        mu = jnp.mean(x, axis=-1, keepdims=True)
        xc = x - mu
        var = jnp.mean(xc * xc, axis=-1, keepdims=True)
        xn = xc * lax.rsqrt(var + LN_EPS)
        shift = ada_ref[:, 0:d_model]
        scale = ada_ref[:, d_model:2 * d_model]
        h_scr[...] = (xn * (1.0 + scale) + shift).astype(BF16)
        ang = pos_ref[...].astype(F32) * tab_ref[0:1, :]
        cos_scr[...] = jnp.cos(ang)
        sin_scr[...] = jnp.sin(ang) * tab_ref[1:2, :]

    acc = jnp.dot(h_scr[...], w_ref[...], preferred_element_type=F32)

    def rope_store(dst_ref, scale):
        cos = cos_scr[...]
        sin = sin_scr[...]
        for hh in range(ATT_KV_HEADS):
            sl = slice(hh * ATT_HEAD_DIM, (hh + 1) * ATT_HEAD_DIM)
            xh = acc[:, sl]
            rot = pltpu.roll(xh, ATT_HEAD_DIM // 2, axis=1)
            val = xh * cos + rot * sin
            if scale is not None:
                val = val * scale
            dst_ref[:, sl] = val.astype(dst_ref.dtype)

    q_scale = ATT_HEAD_DIM ** -0.5

    @pl.when(j == 0)
    def _():
        rope_store(q0_ref, q_scale)

    @pl.when(j == 1)
    def _():
        rope_store(q1_ref, q_scale)

    @pl.when(j == 2)
    def _():
        rope_store(q2_ref, q_scale)

    @pl.when(j == 3)
    def _():
        rope_store(kv_ref, None)

    @pl.when(j == 4)
    def _():
        kv_ref[...] = acc.astype(kv_ref.dtype)

    @pl.when(j >= 5)
    def _():
        rest_ref[...] = acc.astype(rest_ref.dtype)


def _inproj(x2d, ada3, pos2d, rope_tab, w_in_bf16, seq):
    rows, d = x2d.shape
    n_in = w_in_bf16.shape[1]
    tm, tn = INPROJ_TM, INPROJ_TN
    n_j = n_in // tn
    n_rest = n_j - 5
    tiles_per_seq = seq // tm

    def w_map(i, j):
        return (0, jnp.where(j < 5, j, jnp.where(j == n_j - 1, 5, j + 1)))

    out_shapes = (
        jax.ShapeDtypeStruct((rows, ATT_KV_WIDTH), BF16),
        jax.ShapeDtypeStruct((rows, ATT_KV_WIDTH), BF16),
        jax.ShapeDtypeStruct((rows, ATT_KV_WIDTH), BF16),
        jax.ShapeDtypeStruct((rows, 2 * ATT_KV_WIDTH), BF16),
        jax.ShapeDtypeStruct((rows, n_rest * tn), BF16),
    )
    q_spec = pl.BlockSpec((tm, tn), lambda i, j: (i, 0))
    return pl.pallas_call(
        functools.partial(_inproj_kernel, d_model=d),
        grid=(rows // tm, n_j),
        in_specs=[
            pl.BlockSpec((tm, d), lambda i, j: (i, 0)),
            pl.BlockSpec((None, 1, 3 * d), lambda i, j: (i // tiles_per_seq, 0, 0)),
            pl.BlockSpec((tm, 1), lambda i, j: (i, 0)),
            pl.BlockSpec((8, LANES), lambda i, j: (0, 0)),
            pl.BlockSpec((d, tn), w_map),
        ],
        out_specs=(
            q_spec, q_spec, q_spec,
            pl.BlockSpec((tm, tn), lambda i, j: (i, jnp.clip(j - 3, 0, 1))),
            pl.BlockSpec((tm, tn), lambda i, j: (i, jnp.clip(j - 5, 0, n_rest - 1))),
        ),
        out_shape=out_shapes,
        scratch_shapes=[pltpu.VMEM((tm, d), BF16),
                        pltpu.VMEM((tm, LANES), F32),
                        pltpu.VMEM((tm, LANES), F32)],
        compiler_params=_params("arbitrary", "arbitrary"),
        name="inproj",
    )(x2d, ada3, pos2d, rope_tab, w_in_bf16)


def _attn_sub(q, k_own, v_own, k_prev, v_prev, prev_thr):
    nt = (((1,), (1,)), ((), ()))
    s_own = lax.dot_general(q, k_own, nt, preferred_element_type=F32)
    s_prev = lax.dot_general(q, k_prev, nt, preferred_element_type=F32)
    row = lax.broadcasted_iota(jnp.int32, (ATT_BLOCK, ATT_BLOCK), 0)
    col = lax.broadcasted_iota(jnp.int32, (ATT_BLOCK, ATT_BLOCK), 1)
    s_own = jnp.where(col <= row, s_own, NEG)
    s_prev = jnp.where((col - row) >= prev_thr, s_prev, NEG)
    m = jnp.maximum(jnp.max(s_own, axis=-1, keepdims=True), jnp.max(s_prev, axis=-1, keepdims=True))
    p_own = jnp.exp(s_own - m)
    p_prev = jnp.exp(s_prev - m)
    den = jnp.sum(p_own, axis=-1, keepdims=True) + jnp.sum(p_prev, axis=-1, keepdims=True)
    o = (jnp.dot(p_own.astype(BF16), v_own, preferred_element_type=F32)
         + jnp.dot(p_prev.astype(BF16), v_prev, preferred_element_type=F32))
    return o / den, m + jnp.log(den)


def _attn_kernel(q_ref, kv_ref, kvp_ref, o_ref, st_ref, *, folded):
    first = pl.program_id(1) == 0
    thr_first = jnp.where(first, 2 * ATT_BLOCK, 0).astype(jnp.int32)
    lane = lax.broadcasted_iota(jnp.int32, (ATT_BLOCK, LANES), 1)
    hd, kw = ATT_HEAD_DIM, ATT_KV_WIDTH
    for u in range(ATT_SUB):
        stats = jnp.zeros((ATT_BLOCK, LANES), F32)
        for h in range(ATT_KV_HEADS):
            if folded:
                q = q_ref[:, u * kw + h * hd:u * kw + (h + 1) * hd]
                base = u * 2 * kw + h * hd
                k_own = kv_ref[:, base:base + hd]
                v_own = kv_ref[:, base + kw:base + kw + hd]
                k_prev = kvp_ref[:, base:base + hd]
                v_prev = kvp_ref[:, base + kw:base + kw + hd]
                thr = thr_first
            else:
                q = q_ref[u, :, h * hd:(h + 1) * hd]
                k_own = kv_ref[u, :, h * hd:(h + 1) * hd]
                v_own = kv_ref[u, :, kw + h * hd:kw + (h + 1) * hd]
                if u == 0:
                    k_prev = kvp_ref[0, :, h * hd:(h + 1) * hd]
                    v_prev = kvp_ref[0, :, kw + h * hd:kw + (h + 1) * hd]
                    thr = thr_first
                else:
                    k_prev = kv_ref[u - 1, :, h * hd:(h + 1) * hd]
                    v_prev = kv_ref[u - 1, :, kw + h * hd:kw + (h + 1) * hd]
                    thr = jnp.int32(0)
            o, lse = _attn_sub(q, k_own, v_own, k_prev, v_prev, thr)
            stats = jnp.where(lane == h, lse, stats)
            if folded:
                o_ref[:, u * kw + h * hd:u * kw + (h + 1) * hd] = o.astype(o_ref.dtype)
            else:
                o_ref[u, :, h * hd:(h + 1) * hd] = o.astype(o_ref.dtype)
        if folded:
            st_ref[:, u * LANES:(u + 1) * LANES] = stats
        else:
            st_ref[u] = stats


def _attention_group(q2d, kv2d, bsz, seq, dil):
    kw = ATT_KV_WIDTH
    nb = seq // (ATT_BLOCK * dil)
    if dil == 1:
        q4 = q2d.reshape(bsz, nb, ATT_BLOCK, kw)
        kv4 = kv2d.reshape(bsz, nb, ATT_BLOCK, 2 * kw)
        grid = (bsz, nb // ATT_SUB)
        in_specs = [
            pl.BlockSpec((None, ATT_SUB, ATT_BLOCK, kw), lambda b, n: (b, n, 0, 0)),
            pl.BlockSpec((None, ATT_SUB, ATT_BLOCK, 2 * kw), lambda b, n: (b, n, 0, 0)),
            pl.BlockSpec((None, 1, ATT_BLOCK, 2 * kw),
                         lambda b, n: (b, jnp.maximum(n * ATT_SUB - 1, 0), 0, 0)),
        ]
        out_specs = (
            pl.BlockSpec((None, ATT_SUB, ATT_BLOCK, kw), lambda b, n: (b, n, 0, 0)),
            pl.BlockSpec((None, ATT_SUB, ATT_BLOCK, LANES), lambda b, n: (b, n, 0, 0)),
        )
        out_shape = (jax.ShapeDtypeStruct((bsz, nb, ATT_BLOCK, kw), BF16),
                     jax.ShapeDtypeStruct((bsz, nb, ATT_BLOCK, LANES), F32))
        sem = ("arbitrary", "arbitrary")
        kernel = functools.partial(_attn_kernel, folded=False)
    else:
        q4 = q2d.reshape(bsz, nb, ATT_BLOCK, dil * kw)
        kv4 = kv2d.reshape(bsz, nb, ATT_BLOCK, dil * 2 * kw)
        grid = (bsz, nb, dil // ATT_SUB)
        in_specs = [
            pl.BlockSpec((None, None, ATT_BLOCK, ATT_SUB * kw), lambda b, n, r: (b, n, 0, r)),
            pl.BlockSpec((None, None, ATT_BLOCK, ATT_SUB * 2 * kw), lambda b, n, r: (b, n, 0, r)),
            pl.BlockSpec((None, None, ATT_BLOCK, ATT_SUB * 2 * kw),
                         lambda b, n, r: (b, jnp.maximum(n - 1, 0), 0, r)),
        ]
        out_specs = (
            pl.BlockSpec((None, None, ATT_BLOCK, ATT_SUB * kw), lambda b, n, r: (b, n, 0, r)),
            pl.BlockSpec((None, None, ATT_BLOCK, ATT_SUB * LANES), lambda b, n, r: (b, n, 0, r)),
        )
        out_shape = (jax.ShapeDtypeStruct((bsz, nb, ATT_BLOCK, dil * kw), BF16),
                     jax.ShapeDtypeStruct((bsz, nb, ATT_BLOCK, dil * LANES), F32))
        sem = ("arbitrary", "arbitrary", "arbitrary")
        kernel = functools.partial(_attn_kernel, folded=True)
    o, st = pl.pallas_call(
        kernel, grid=grid, in_specs=in_specs, out_specs=out_specs, out_shape=out_shape,
        compiler_params=_params(*sem), name=f"attn_d{dil}",
    )(q4, kv4, kv4)
    return o.reshape(bsz * seq, kw), st.reshape(bsz * seq, LANES)


def _mpre_kernel(xm_ref, halo_ref, cw_ref, cb_ref, wq_ref, wk_ref, wv_ref, wif_ref, bif_ref,
                 qm_ref, km_ref, vm_ref, xc_ref, g_ref, ext_scr):
    i = pl.program_id(1)
    tm = xm_ref.shape[0]
    xm = xm_ref[...]
    halo = halo_ref[...].astype(F32)
    ext_scr[0:CONV_HALO, :] = jnp.where(i > 0, halo, 0.0)
    ext_scr[CONV_HALO:CONV_HALO + tm, :] = xm.astype(F32)
    acc = jnp.zeros((tm, M_WIDTH), F32) + cb_ref[...]
    for k in range(M_CONV):
        start = CONV_HALO - (M_CONV - 1) + k
        acc = acc + cw_ref[k:k + 1, :] * ext_scr[pl.ds(start, tm), :]
    xc = _silu(acc)
    xc_b = xc.astype(BF16)
    xc_ref[...] = xc_b
    k_scale = M_HEAD_DIM ** -0.5
    gates = jnp.zeros((tm, LANES), F32) + bif_ref[...]
    for h in range(M_HEADS):
        sl = slice(h * M_HEAD_DIM, (h + 1) * M_HEAD_DIM)
        qh = jnp.dot(xc_b[:, sl], wq_ref[h], preferred_element_type=F32).astype(BF16)
        kh = (jnp.dot(xc_b[:, sl], wk_ref[h], preferred_element_type=F32) * k_scale).astype(BF16)
        vh = jnp.dot(xm[:, sl], wv_ref[h], preferred_element_type=F32).astype(BF16)
        qm_ref[:, sl] = qh
        km_ref[:, sl] = kh
        vm_ref[:, sl] = vh
        gates = gates + jnp.dot(qh, wif_ref[0, sl, :], preferred_element_type=F32)
        gates = gates + jnp.dot(kh, wif_ref[1, sl, :], preferred_element_type=F32)
        gates = gates + jnp.dot(vh, wif_ref[2, sl, :], preferred_element_type=F32)
    lane = lax.broadcasted_iota(jnp.int32, gates.shape, 1)
    log_f = jnp.minimum(gates, 0.0) - jnp.log1p(jnp.exp(-jnp.abs(gates)))
    g_ref[...] = jnp.where(lane >= M_HEADS, log_f, gates)


def _mpre(rest3, conv_w, conv_b, wq, wk, wv, wif3, bif):
    bsz, seq, _ = rest3.shape
    tm = MPRE_TM
    halo_blocks = tm // CONV_HALO
    full = lambda shape: pl.BlockSpec(shape, lambda b, i: (0,) * len(shape))
    big = jax.ShapeDtypeStruct((bsz, seq, M_WIDTH), BF16)
    return pl.pallas_call(
        _mpre_kernel,
        grid=(bsz, seq // tm),
        in_specs=[
            pl.BlockSpec((None, tm, M_WIDTH), lambda b, i: (b, i, 0)),
            pl.BlockSpec((None, CONV_HALO, M_WIDTH),
                         lambda b, i: (b, jnp.maximum(i * halo_blocks - 1, 0), 0)),
            full((M_CONV, M_WIDTH)), full((1, M_WIDTH)),
            full((M_HEADS, M_HEAD_DIM, M_HEAD_DIM)), full((M_HEADS, M_HEAD_DIM, M_HEAD_DIM)),
            full((M_HEADS, M_HEAD_DIM, M_HEAD_DIM)),
            full((3, M_WIDTH, LANES)), full((1, LANES)),
        ],
        out_specs=(
            pl.BlockSpec((None, tm, M_WIDTH), lambda b, i: (b, i, 0)),
            pl.BlockSpec((None, tm, M_WIDTH), lambda b, i: (b, i, 0)),
            pl.BlockSpec((None, tm, M_WIDTH), lambda b, i: (b, i, 0)),
            pl.BlockSpec((None, tm, M_WIDTH), lambda b, i: (b, i, 0)),
            pl.BlockSpec((None, tm, LANES), lambda b, i: (b, i, 0)),
        ),
        out_shape=(big, big, big, big, jax.ShapeDtypeStruct((bsz, seq, LANES), F32)),
        scratch_shapes=[pltpu.VMEM((tm + CONV_HALO, M_WIDTH), F32)],
        compiler_params=_params("arbitrary", "arbitrary"),
        name="mpre",
    )(rest3, rest3, conv_w, conv_b, wq, wk, wv, wif3, bif)


def _mlstm_kernel(q_ref, k_ref, v_ref, g_ref, om_ref, zm_ref, xc_ref, nw_ref, sk_ref,
                  y_ref, c_scr, n_scr, m_scr):
    @pl.when(pl.program_id(1) == 0)
    def _():
        c_scr[...] = jnp.zeros_like(c_scr)
        n_scr[...] = jnp.zeros_like(n_scr)
        m_scr[...] = jnp.zeros_like(m_scr)

    L = M_CHUNK
    row = lax.broadcasted_iota(jnp.int32, (L, L), 0)
    col = lax.broadcasted_iota(jnp.int32, (L, L), 1)
    causal = col <= row
    tri = causal.astype(F32)
    nt = (((1,), (1,)), ((), ()))
    tn = (((0,), (0,)), ((), ()))

    def chunk(c, carry):
        r0 = pl.multiple_of(c * L, L)
        rows = pl.ds(r0, L)
        g = g_ref[rows, :]
        bm = jnp.dot(tri, g, precision=lax.Precision.HIGHEST,
                     preferred_element_type=F32)
        g_t = g.T
        bm_t = bm.T
        for h in range(M_HEADS):
            sl = slice(h * M_HEAD_DIM, (h + 1) * M_HEAD_DIM)
            q = q_ref[rows, sl]
            k = k_ref[rows, sl]
            v = v_ref[rows, sl]
            li_col = g[:, h:h + 1]
            b_col = bm[:, M_HEADS + h:M_HEADS + h + 1]
            li_row = g_t[h:h + 1, :]
            b_row = bm_t[M_HEADS + h:M_HEADS + h + 1, :]
            b_last = b_col[L - 1:L, :]
            m_prev = m_scr[h:h + 1, 0:1]
            c_prev = c_scr[h]
            n_prev = n_scr[h:h + 1, :]

            dmat = jnp.where(causal, b_col - b_row + li_row, NEG)
            g_col = b_col + m_prev
            m_t = jnp.maximum(g_col, jnp.max(dmat, axis=-1, keepdims=True))
            smat = lax.dot_general(q, k, nt, preferred_element_type=F32) * jnp.exp(dmat - m_t)
            w_inter = jnp.exp(g_col - m_t)
            num = (w_inter * jnp.dot(q, c_prev.astype(BF16), preferred_element_type=F32)
                   + jnp.dot(smat.astype(BF16), v, preferred_element_type=F32))
            den = (w_inter * jnp.sum(q.astype(F32) * n_prev, axis=-1, keepdims=True)
                   + jnp.sum(smat, axis=-1, keepdims=True))
            hh = num / jnp.maximum(jnp.abs(den), jnp.exp(-m_t))

            w_src = b_last - b_col + li_col
            a = jnp.max(w_src, axis=0, keepdims=True)
            ks = k.astype(F32) * jnp.exp(w_src - a)
            c_loc = lax.dot_general(ks.astype(BF16), v, tn, preferred_element_type=F32)
            n_loc = jnp.sum(ks, axis=0, keepdims=True)
            m_new = jnp.maximum(b_last + m_prev, a)
            decay = jnp.exp(b_last + m_prev - m_new)
            gain = jnp.exp(a - m_new)
            c_scr[h] = decay * c_prev + gain * c_loc
            n_scr[h:h + 1, :] = decay * n_prev + gain * n_loc
            m_scr[h:h + 1, :] = jnp.broadcast_to(m_new, (1, LANES))

            hm = jax.nn.sigmoid(om_ref[rows, sl].astype(F32)) * hh
            mu = jnp.mean(hm, axis=-1, keepdims=True)
            hc = hm - mu
            var = jnp.mean(hc * hc, axis=-1, keepdims=True)
            hn = hc * lax.rsqrt(var + LN_EPS) * nw_ref[:, sl]
            y = (hn + sk_ref[:, sl] * xc_ref[rows, sl].astype(F32)) * _silu(zm_ref[rows, sl].astype(F32))
            y_ref[rows, sl] = y.astype(y_ref.dtype)
        return carry

    lax.fori_loop(0, q_ref.shape[0] // L, chunk, 0)


def _mlstm(qm, km, vm, gates, rest3, xc, norm_w, skip):
    bsz, seq, _ = qm.shape
    tm = MLSTM_TM
    row_spec = pl.BlockSpec((None, tm, M_WIDTH), lambda b, i: (b, i, 0))
    return pl.pallas_call(
        _mlstm_kernel,
        grid=(bsz, seq // tm),
        in_specs=[
            row_spec, row_spec, row_spec,
            pl.BlockSpec((None, tm, LANES), lambda b, i: (b, i, 0)),
            pl.BlockSpec((None, tm, M_WIDTH), lambda b, i: (b, i, 2)),
            pl.BlockSpec((None, tm, M_WIDTH), lambda b, i: (b, i, 1)),
            row_spec,
            pl.BlockSpec((1, M_WIDTH), lambda b, i: (0, 0)),
            pl.BlockSpec((1, M_WIDTH), lambda b, i: (0, 0)),
        ],
        out_specs=row_spec,
        out_shape=jax.ShapeDtypeStruct((bsz, seq, M_WIDTH), BF16),
        scratch_shapes=[pltpu.VMEM((M_HEADS, M_HEAD_DIM, M_HEAD_DIM), F32),
                        pltpu.VMEM((8, M_HEAD_DIM), F32),
                        pltpu.VMEM((8, LANES), F32)],
        compiler_params=_params("arbitrary", "arbitrary"),
        name="mlstm",
    )(qm, km, vm, gates, rest3, rest3, xc, norm_w, skip)


def _post_kernel(o0_ref, o1_ref, o2_ref, s0_ref, s1_ref, s2_ref, za_ref, ym_ref, ga_ref, gm_ref,
                 x_ref, ada_ref, wpa_ref, wpm_ref, wout_ref, lng_ref, lnb_ref, out_ref,
                 *, alpha, d_model):
    hd = ATT_HEAD_DIM
    parts = []
    for h in range(ATT_KV_HEADS):
        sl = slice(h * hd, (h + 1) * hd)
        l0 = s0_ref[:, h:h + 1]
        l1 = s1_ref[:, h:h + 1]
        l2 = s2_ref[:, h:h + 1]
        mx = jnp.maximum(jnp.maximum(l0, l1), l2)
        e0 = jnp.exp(l0 - mx)
        e1 = jnp.exp(l1 - mx)
        e2 = jnp.exp(l2 - mx)
        tot = e0 + e1 + e2
        o = ((e0 / tot) * o0_ref[:, sl].astype(F32) + (e1 / tot) * o1_ref[:, sl].astype(F32)
             + (e2 / tot) * o2_ref[:, sl].astype(F32))
        parts.append((o * _silu(za_ref[:, sl].astype(F32))).astype(BF16))
    att_in = jnp.concatenate(parts, axis=1)
    y_att = jnp.dot(att_in, wpa_ref[...], preferred_element_type=F32)
    y_m = jnp.dot(ym_ref[...], wpm_ref[...], preferred_element_type=F32)
    merged = (jax.nn.sigmoid(ga_ref[...].astype(F32)) * y_att
              + jax.nn.sigmoid(gm_ref[...].astype(F32)) * y_m)
    out = jnp.dot(merged.astype(BF16), wout_ref[...], preferred_element_type=F32)
    gate = ada_ref[:, 2 * d_model:3 * d_model]
    res = alpha * x_ref[...] + gate * out
    mu = jnp.mean(res, axis=-1, keepdims=True)
    rc = res - mu
    var = jnp.mean(rc * rc, axis=-1, keepdims=True)
    out_ref[...] = (rc * lax.rsqrt(var + LN_EPS) * lng_ref[...] + lnb_ref[...]).astype(out_ref.dtype)


def _post(o_list, st_list, rest2, ymin2, x2d, ada3, wpa, wpm, wout, ln_g, ln_b, seq, alpha):
    rows, d = x2d.shape
    tm = POST_TM
    tiles_per_seq = seq // tm
    kw = ATT_KV_WIDTH
    o_spec = pl.BlockSpec((tm, kw), lambda i: (i, 0))
    s_spec = pl.BlockSpec((tm, LANES), lambda i: (i, 0))
    za_block = (5 * M_WIDTH) // kw
    const = lambda shape: pl.BlockSpec(shape, lambda i: (0,) * len(shape))
    return pl.pallas_call(
        functools.partial(_post_kernel, alpha=alpha, d_model=d),
        grid=(rows // tm,),
        in_specs=[
            o_spec, o_spec, o_spec, s_spec, s_spec, s_spec,
            pl.BlockSpec((tm, kw), lambda i: (i, za_block)),
            pl.BlockSpec((tm, M_WIDTH), lambda i: (i, 0)),
            pl.BlockSpec((tm, d), lambda i: (i, 3)),
            pl.BlockSpec((tm, d), lambda i: (i, 4)),
            pl.BlockSpec((tm, d), lambda i: (i, 0)),
            pl.BlockSpec((None, 1, 3 * d), lambda i: (i // tiles_per_seq, 0, 0)),
            const((kw, d)), const((M_WIDTH, d)), const((d, d)),
            const((1, d)), const((1, d)),
        ],
        out_specs=pl.BlockSpec((tm, d), lambda i: (i, 0)),
        out_shape=jax.ShapeDtypeStruct((rows, d), x2d.dtype),
        compiler_params=_params("arbitrary"),
        name="post",
    )(*o_list, *st_list, rest2, ymin2, rest2, rest2, x2d, ada3, wpa, wpm, wout, ln_g, ln_b)


def _layer(x, c, positions, w_ada, b_ada, w_in, conv_w, conv_b, w_qm, w_km, w_vm,
           w_if, b_if, mh_norm_w, skip_m, w_pa, w_pm, w_out, ln_g, ln_b, alpha):
    bsz, seq, d = x.shape
    rows = bsz * seq
    x2d = x.reshape(rows, d)

    ada3 = _ada(c, w_ada, b_ada).reshape(bsz, 1, 3 * d)

    half = ATT_HEAD_DIM // 2
    inv = jnp.power(ROPE_THETA, -jnp.arange(half, dtype=F32) / half)
    sign = jnp.concatenate([-jnp.ones((half,), F32), jnp.ones((half,), F32)])
    rope_tab = jnp.zeros((8, LANES), F32).at[0].set(jnp.concatenate([inv, inv])).at[1].set(sign)

    q0, q1, q2, kv, rest = _inproj(x2d, ada3, positions.reshape(rows, 1), rope_tab,
                                   w_in.astype(BF16), seq)

    o_list, st_list = [], []
    for qg, (_, dil) in zip((q0, q1, q2), ATT_GROUPS):
        o_g, st_g = _attention_group(qg, kv, bsz, seq, dil)
        o_list.append(o_g)
        st_list.append(st_g)

    rest3 = rest.reshape(bsz, seq, rest.shape[1])
    n_gate = 2 * M_HEADS
    wif3 = jnp.zeros((3, M_WIDTH, LANES), BF16).at[:, :, :n_gate].set(
        w_if.astype(BF16).reshape(3, M_WIDTH, n_gate))
    bif = jnp.zeros((1, LANES), F32).at[0, :n_gate].set(b_if.astype(F32))
    qm, km, vm, xc, gates = _mpre(rest3, conv_w.astype(F32), conv_b.astype(F32).reshape(1, M_WIDTH),
                                  w_qm.astype(BF16), w_km.astype(BF16), w_vm.astype(BF16), wif3, bif)
    ymin = _mlstm(qm, km, vm, gates, rest3, xc,
                  mh_norm_w.astype(F32).reshape(1, M_WIDTH), skip_m.astype(F32).reshape(1, M_WIDTH))

    out = _post(o_list, st_list, rest, ymin.reshape(rows, M_WIDTH), x2d, ada3,
                w_pa.astype(BF16), w_pm.astype(BF16), w_out.astype(BF16),
                ln_g.astype(F32).reshape(1, d), ln_b.astype(F32).reshape(1, d), seq, alpha)
    return out.reshape(bsz, seq, d)


def kernel(x, c, positions, w_ada, b_ada, w_in, conv_w, conv_b, w_qm, w_km, w_vm, w_if, b_if,
           mh_norm_w, skip_m, w_pa, w_pm, w_out, ln_g, ln_b):
    depth = w_ada.shape[0]
    alpha = (2.0 * depth) ** 0.25
    for l in range(depth):
        x = _layer(x, c, positions, w_ada[l], b_ada[l], w_in[l], conv_w[l], conv_b[l],
                   w_qm[l], w_km[l], w_vm[l], w_if[l], b_if[l], mh_norm_w[l], skip_m[l],
                   w_pa[l], w_pm[l], w_out[l], ln_g[l], ln_b[l], alpha)
    return x
```

```python
import functools

import jax
import jax.numpy as jnp
from jax import lax
from jax.experimental import pallas as pl
from jax.experimental.pallas import tpu as pltpu

F32 = jnp.float32
BF16 = jnp.bfloat16

ATT_HEAD_DIM = 128
ATT_GROUPS = ((128, 1), (512, 4), (2048, 16))
ATT_KV_HEADS = 4
ATT_KV_WIDTH = ATT_KV_HEADS * ATT_HEAD_DIM
ATT_BLOCK = 128
ROPE_THETA = 10000.0
M_HEADS = 4
M_HEAD_DIM = 256
M_WIDTH = M_HEADS * M_HEAD_DIM
M_CONV = 4
M_CHUNK = 128
LN_EPS = 1e-5
NEG = -1e30

V7X_VMEM_LIMIT_BYTES = 56 * 1024 * 1024
LANES = 128

INPROJ_TM = 1024
INPROJ_TN = 512
ATT_SUB = 4
ATT_SPAN_SUBS = 16
MPRE_TM = 512
MLSTM_TM = 512
POST_TM = 512
CONV_HALO = 16


def _silu(v):
    return v * jax.nn.sigmoid(v)


def _params(*sem):
    return pltpu.CompilerParams(dimension_semantics=sem, vmem_limit_bytes=V7X_VMEM_LIMIT_BYTES)


def _ada_kernel(c_ref, w_ref, b_ref, o_ref):
    sc = _silu(c_ref[...])
    o_ref[...] = jnp.dot(sc, w_ref[...], precision=lax.Precision.HIGHEST,
                         preferred_element_type=F32) + b_ref[...]


def _ada(c, w_ada, b_ada):
    bsz, d = c.shape
    n = w_ada.shape[1]
    return pl.pallas_call(
        _ada_kernel,
        grid=(n // d,),
        in_specs=[pl.BlockSpec((bsz, d), lambda j: (0, 0)),
                  pl.BlockSpec((d, d), lambda j: (0, j)),
                  pl.BlockSpec((1, d), lambda j: (0, j))],
        out_specs=pl.BlockSpec((bsz, d), lambda j: (0, j)),
        out_shape=jax.ShapeDtypeStruct((bsz, n), F32),
        compiler_params=_params("arbitrary"),
        name="ada",
    )(c, w_ada, b_ada.reshape(1, n))


def _inproj_kernel(x_ref, ada_ref, pos_ref, tab_ref, w_ref,
                   q0_ref, q1_ref, q2_ref, kv_ref, kv4_ref, kv16_ref, rest_ref,
                   h_scr, cos_scr, sin_scr, stage_scr, *, d_model):
    j = pl.program_id(1)
    tm = x_ref.shape[0]
    hd = ATT_HEAD_DIM

    @pl.when(j == 0)
    def _():
        x = x_ref[...]
        mu = jnp.mean(x, axis=-1, keepdims=True)
        xc = x - mu
        var = jnp.mean(xc * xc, axis=-1, keepdims=True)
        xn = xc * lax.rsqrt(var + LN_EPS)
        shift = ada_ref[:, 0:d_model]
        scale = ada_ref[:, d_model:2 * d_model]
        h_scr[...] = (xn * (1.0 + scale) + shift).astype(BF16)
        ang = pos_ref[...].astype(F32) * tab_ref[0:1, :]
        cos_scr[...] = jnp.cos(ang)
        sin_scr[...] = jnp.sin(ang) * tab_ref[1:2, :]

    def project():
        return jnp.dot(h_scr[...], w_ref[...], preferred_element_type=F32)

    def heads(acc, rope, scale):
        for hh in range(ATT_KV_HEADS):
            xh = acc[:, hh * hd:(hh + 1) * hd]
            if rope:
                rot = pltpu.roll(xh, hd // 2, axis=1)
                xh = xh * cos_scr[...] + rot * sin_scr[...]
            if scale is not None:
                xh = xh * scale
            yield hh, xh

    def regroup(dst_ref, dil):
        blk = ATT_BLOCK * dil
        for hh in range(ATT_KV_HEADS):
            sl = slice(hh * hd, (hh + 1) * hd)
            if blk <= tm:
                for nn in range(tm // blk):
                    for r in range(dil):
                        rows = stage_scr[hh, pl.ds(nn * blk + r, ATT_BLOCK, stride=dil), :]
                        dst_ref[nn, r, :, sl] = rows.astype(dst_ref.dtype)
            else:
                for r in range(dil):
                    rows = stage_scr[hh, pl.ds(r, tm // dil, stride=dil), :]
                    dst_ref[r, :, sl] = rows.astype(dst_ref.dtype)

    q_scale = hd ** -0.5
    dil4, dil16 = ATT_GROUPS[1][1], ATT_GROUPS[2][1]

    @pl.when(j == 0)
    def _():
        for hh, val in heads(project(), True, q_scale):
            q0_ref[:, hh * hd:(hh + 1) * hd] = val.astype(q0_ref.dtype)

    @pl.when(j == 1)
    def _():
        for hh, val in heads(project(), True, q_scale):
            stage_scr[hh] = val
        regroup(q1_ref, dil4)

    @pl.when(j == 2)
    def _():
        for hh, val in heads(project(), True, q_scale):
            stage_scr[hh] = val
        regroup(q2_ref, dil16)

    def kv_tile(rope):
        for hh, val in heads(project(), rope, None):
            stage_scr[hh] = val
            kv_ref[:, hh * hd:(hh + 1) * hd] = val.astype(kv_ref.dtype)
        regroup(kv4_ref, dil4)
        regroup(kv16_ref, dil16)

    @pl.when(j == 3)
    def _():
        kv_tile(True)

    @pl.when(j == 4)
    def _():
        kv_tile(False)

    @pl.when(j >= 5)
    def _():
        rest_ref[...] = project().astype(rest_ref.dtype)


def _inproj(x2d, ada3, pos2d, rope_tab, w_in_bf16, bsz, seq):
    rows, d = x2d.shape
    n_in = w_in_bf16.shape[1]
    tm, tn = INPROJ_TM, INPROJ_TN
    kw = ATT_KV_WIDTH
    n_j = n_in // tn
    n_rest = n_j - 5
    tps = seq // tm
    dil4, dil16 = ATT_GROUPS[1][1], ATT_GROUPS[2][1]
    blk4, blk16 = ATT_BLOCK * dil4, ATT_BLOCK * dil16
    assert tm % blk4 == 0 and blk16 % tm == 0 and tn == kw
    t16 = blk16 // tm

    def w_map(i, j):
        return (0, jnp.where(j < 5, j, jnp.where(j == n_j - 1, 5, j + 1)))

    def kv_col(j):
        return jnp.clip(j - 3, 0, 1)

    out_shapes = (
        jax.ShapeDtypeStruct((rows, kw), BF16),
        jax.ShapeDtypeStruct((bsz, seq // blk4, dil4, ATT_BLOCK, kw), BF16),
        jax.ShapeDtypeStruct((bsz, seq // blk16, dil16, ATT_BLOCK, kw), BF16),
        jax.ShapeDtypeStruct((rows, 2 * kw), BF16),
        jax.ShapeDtypeStruct((bsz, seq // blk4, dil4, ATT_BLOCK, 2 * kw), BF16),
        jax.ShapeDtypeStruct((bsz, seq // blk16, dil16, ATT_BLOCK, 2 * kw), BF16),
        jax.ShapeDtypeStruct((rows, n_rest * tn), BF16),
    )
    r4_block = (None, tm // blk4, dil4, ATT_BLOCK, tn)
    r16_block = (None, None, dil16, tm // dil16, tn)
    return pl.pallas_call(
        functools.partial(_inproj_kernel, d_model=d),
        grid=(rows // tm, n_j),
        in_specs=[
            pl.BlockSpec((tm, d), lambda i, j: (i, 0)),
            pl.BlockSpec((None, 1, 3 * d), lambda i, j: (i // tps, 0, 0)),
            pl.BlockSpec((tm, 1), lambda i, j: (i, 0)),
            pl.BlockSpec((8, LANES), lambda i, j: (0, 0)),
            pl.BlockSpec((d, tn), w_map),
        ],
        out_specs=(
            pl.BlockSpec((tm, tn), lambda i, j: (i, 0)),
            pl.BlockSpec(r4_block, lambda i, j: (i // tps, i % tps, 0, 0, 0)),
            pl.BlockSpec(r16_block, lambda i, j: (i // tps, (i % tps) // t16, 0, (i % tps) % t16, 0)),
            pl.BlockSpec((tm, tn), lambda i, j: (i, kv_col(j))),
            pl.BlockSpec(r4_block, lambda i, j: (i // tps, i % tps, 0, 0, kv_col(j))),
            pl.BlockSpec(r16_block,
                         lambda i, j: (i // tps, (i % tps) // t16, 0, (i % tps) % t16, kv_col(j))),
            pl.BlockSpec((tm, tn), lambda i, j: (i, jnp.clip(j - 5, 0, n_rest - 1))),
        ),
        out_shape=out_shapes,
        scratch_shapes=[pltpu.VMEM((tm, d), BF16),
                        pltpu.VMEM((tm, LANES), F32),
                        pltpu.VMEM((tm, LANES), F32),
                        pltpu.VMEM((ATT_KV_HEADS, tm, ATT_HEAD_DIM), F32)],
        compiler_params=_params("arbitrary", "arbitrary"),
        name="inproj",
    )(x2d, ada3, pos2d, rope_tab, w_in_bf16)


def _attn_kernel(q_ref, kv_ref, kvp_ref, o_ref, st_ref, onat_scr, *, dil):
    span = pl.program_id(1)
    g = pl.program_id(2)
    hd, kw = ATT_HEAD_DIM, ATT_KV_WIDTH
    groups_per_span = ATT_SPAN_SUBS // ATT_SUB
    if dil == 1:
        first = jnp.logical_and(span == 0, g == 0)
    else:
        first = (span * groups_per_span + g) < dil // ATT_SUB
    thr_first = jnp.where(first, 2 * ATT_BLOCK, 0).astype(jnp.int32)

    row = lax.broadcasted_iota(jnp.int32, (ATT_BLOCK, 2 * ATT_BLOCK), 0)
    col = lax.broadcasted_iota(jnp.int32, (ATT_BLOCK, 2 * ATT_BLOCK), 1)
    in_prev = col < ATT_BLOCK
    rel = col - row
    ones = jnp.ones((2 * ATT_BLOCK, hd), BF16)
    lane = lax.broadcasted_iota(jnp.int32, (ATT_BLOCK, LANES), 1)
    nt = (((1,), (1,)), ((), ()))

    for u in range(ATT_SUB):
        if dil == 1:
            dst = pl.ds(pl.multiple_of((g * ATT_SUB + u) * ATT_BLOCK, ATT_BLOCK), ATT_BLOCK)
        elif dil == ATT_SUB:
            dst = pl.ds(g * (ATT_BLOCK * dil) + u, ATT_BLOCK, stride=dil)
        else:
            dst = pl.ds(g * ATT_SUB + u, ATT_BLOCK, stride=dil)

        scores, vexts = [], []
        for h in range(ATT_KV_HEADS):
            ksl = slice(h * hd, (h + 1) * hd)
            vsl = slice(kw + h * hd, kw + (h + 1) * hd)
            q = q_ref[u, :, ksl]
            if dil == 1 and u > 0:
                k_prev, v_prev, thr = kv_ref[u - 1, :, ksl], kv_ref[u - 1, :, vsl], jnp.int32(0)
            elif dil == 1:
                k_prev, v_prev, thr = kvp_ref[0, :, ksl], kvp_ref[0, :, vsl], thr_first
            else:
                k_prev, v_prev, thr = kvp_ref[u, :, ksl], kvp_ref[u, :, vsl], thr_first
            k_cat = jnp.concatenate([k_prev, kv_ref[u, :, ksl]], axis=0)
            v_cat = jnp.concatenate([v_prev, kv_ref[u, :, vsl]], axis=0)
            vexts.append(jnp.concatenate([v_cat, ones], axis=1))
            s = lax.dot_general(q, k_cat, nt, preferred_element_type=F32)
            slack = jnp.where(in_prev, rel - thr, ATT_BLOCK - rel)
            scores.append(jnp.where(slack >= 0, s, NEG))

        probs, maxes = [], []
        for h in range(ATT_KV_HEADS):
            m = jnp.max(scores[h], axis=-1, keepdims=True)
            maxes.append(m)
            probs.append(jnp.exp(scores[h] - m).astype(BF16))

        stats = jnp.zeros((ATT_BLOCK, LANES), F32)
        for h in range(ATT_KV_HEADS):
            oe = jnp.dot(probs[h], vexts[h], preferred_element_type=F32)
            den = oe[:, hd:]
            onat_scr[h, dst, :] = oe[:, :hd] / den
            stats = jnp.where(lane == h, maxes[h] + jnp.log(den[:, 0:1]), stats)
        st_ref[dst, :] = stats

    @pl.when(g == groups_per_span - 1)
    def _():
        for h in range(ATT_KV_HEADS):
            o_ref[:, h * hd:(h + 1) * hd] = onat_scr[h].astype(o_ref.dtype)


def _attention_group(q_sub, kv_sub, kv_prev, bsz, seq, dil):
    kw = ATT_KV_WIDTH
    span_rows = ATT_SPAN_SUBS * ATT_BLOCK
    gps = ATT_SPAN_SUBS // ATT_SUB
    assert ATT_SPAN_SUBS % dil == 0 and (dil == 1 or dil % ATT_SUB == 0)
    own_map = lambda b, s, g: (b, s * gps + g, 0, 0)
    if dil == 1:
        prev_spec = pl.BlockSpec((None, 1, ATT_BLOCK, 2 * kw),
                                 lambda b, s, g: (b, jnp.maximum((s * gps + g) * ATT_SUB - 1, 0), 0, 0))
    else:
        prev_spec = pl.BlockSpec((None, ATT_SUB, ATT_BLOCK, 2 * kw),
                                 lambda b, s, g: (b, jnp.maximum(s * gps + g - dil // ATT_SUB, 0), 0, 0))
    return pl.pallas_call(
        functools.partial(_attn_kernel, dil=dil),
        grid=(bsz, seq // span_rows, gps),
        in_specs=[
            pl.BlockSpec((None, ATT_SUB, ATT_BLOCK, kw), own_map),
            pl.BlockSpec((None, ATT_SUB, ATT_BLOCK, 2 * kw), own_map),
            prev_spec,
        ],
        out_specs=(
            pl.BlockSpec((None, span_rows, kw), lambda b, s, g: (b, s, 0)),
            pl.BlockSpec((None, span_rows, LANES), lambda b, s, g: (b, s, 0)),
        ),
        out_shape=(jax.ShapeDtypeStruct((bsz, seq, kw), BF16),
                   jax.ShapeDtypeStruct((bsz, seq, LANES), F32)),
        scratch_shapes=[pltpu.VMEM((ATT_KV_HEADS, span_rows, ATT_HEAD_DIM), F32)],
        compiler_params=_params("arbitrary", "arbitrary", "arbitrary"),
        name=f"attn_d{dil}",
    )(q_sub, kv_sub, kv_prev)


def _mpre_kernel(xm_ref, halo_ref, cw_ref, cb_ref, wq_ref, wk_ref, wv_ref, wif_ref, bif_ref,
                 qm_ref, km_ref, vm_ref, xc_ref, g_ref, ext_scr):
    i = pl.program_id(1)
    tm = xm_ref.shape[0]
    xm = xm_ref[...]
    halo = halo_ref[...].astype(F32)
    ext_scr[0:CONV_HALO, :] = jnp.where(i > 0, halo, 0.0)
    ext_scr[CONV_HALO:CONV_HALO + tm, :] = xm.astype(F32)
    acc = jnp.zeros((tm, M_WIDTH), F32) + cb_ref[...]
    for k in range(M_CONV):
        start = CONV_HALO - (M_CONV - 1) + k
        acc = acc + cw_ref[k:k + 1, :] * ext_scr[pl.ds(start, tm), :]
    xc = _silu(acc)
    xc_b = xc.astype(BF16)
    xc_ref[...] = xc_b
    k_scale = M_HEAD_DIM ** -0.5
    gates = jnp.zeros((tm, LANES), F32) + bif_ref[...]
    for h in range(M_HEADS):
        sl = slice(h * M_HEAD_DIM, (h + 1) * M_HEAD_DIM)
        qh = jnp.dot(xc_b[:, sl], wq_ref[h], preferred_element_type=F32).astype(BF16)
        kh = (jnp.dot(xc_b[:, sl], wk_ref[h], preferred_element_type=F32) * k_scale).astype(BF16)
        vh = jnp.dot(xm[:, sl], wv_ref[h], preferred_element_type=F32).astype(BF16)
        qm_ref[:, sl] = qh
        km_ref[:, sl] = kh
        vm_ref[:, sl] = vh
        gates = gates + jnp.dot(qh, wif_ref[0, sl, :], preferred_element_type=F32)
        gates = gates + jnp.dot(kh, wif_ref[1, sl, :], preferred_element_type=F32)
        gates = gates + jnp.dot(vh, wif_ref[2, sl, :], preferred_element_type=F32)
    lane = lax.broadcasted_iota(jnp.int32, gates.shape, 1)
    log_f = jnp.minimum(gates, 0.0) - jnp.log1p(jnp.exp(-jnp.abs(gates)))
    g_ref[...] = jnp.where(lane >= M_HEADS, log_f, gates)


def _mpre(rest3, conv_w, conv_b, wq, wk, wv, wif3, bif):
    bsz, seq, _ = rest3.shape
    tm = MPRE_TM
    halo_blocks = tm // CONV_HALO
    full = lambda shape: pl.BlockSpec(shape, lambda b, i: (0,) * len(shape))
    big = jax.ShapeDtypeStruct((bsz, seq, M_WIDTH), BF16)
    return pl.pallas_call(
        _mpre_kernel,
        grid=(bsz, seq // tm),
        in_specs=[
            pl.BlockSpec((None, tm, M_WIDTH), lambda b, i: (b, i, 0)),
            pl.BlockSpec((None, CONV_HALO, M_WIDTH),
                         lambda b, i: (b, jnp.maximum(i * halo_blocks - 1, 0), 0)),
            full((M_CONV, M_WIDTH)), full((1, M_WIDTH)),
            full((M_HEADS, M_HEAD_DIM, M_HEAD_DIM)), full((M_HEADS, M_HEAD_DIM, M_HEAD_DIM)),
            full((M_HEADS, M_HEAD_DIM, M_HEAD_DIM)),
            full((3, M_WIDTH, LANES)), full((1, LANES)),
        ],
        out_specs=(
            pl.BlockSpec((None, tm, M_WIDTH), lambda b, i: (b, i, 0)),
            pl.BlockSpec((None, tm, M_WIDTH), lambda b, i: (b, i, 0)),
            pl.BlockSpec((None, tm, M_WIDTH), lambda b, i: (b, i, 0)),
            pl.BlockSpec((None, tm, M_WIDTH), lambda b, i: (b, i, 0)),
            pl.BlockSpec((None, tm, LANES), lambda b, i: (b, i, 0)),
        ),
        out_shape=(big, big, big, big, jax.ShapeDtypeStruct((bsz, seq, LANES), F32)),
        scratch_shapes=[pltpu.VMEM((tm + CONV_HALO, M_WIDTH), F32)],
        compiler_params=_params("arbitrary", "arbitrary"),
        name="mpre",
    )(rest3, rest3, conv_w, conv_b, wq, wk, wv, wif3, bif)


def _mlstm_kernel(q_ref, k_ref, v_ref, g_ref, om_ref, zm_ref, xc_ref, nw_ref, sk_ref,
                  y_ref, c_scr, n_scr, m_scr):
    @pl.when(pl.program_id(1) == 0)
    def _():
        c_scr[...] = jnp.zeros_like(c_scr)
        n_scr[...] = jnp.zeros_like(n_scr)
        m_scr[...] = jnp.zeros_like(m_scr)

    L = M_CHUNK
    row = lax.broadcasted_iota(jnp.int32, (L, L), 0)
    col = lax.broadcasted_iota(jnp.int32, (L, L), 1)
    causal = col <= row
    tri = causal.astype(F32)
    nt = (((1,), (1,)), ((), ()))
    tn = (((0,), (0,)), ((), ()))

    def chunk(c):
        rows = pl.ds(c * L, L)
        g = g_ref[rows, :]
        bm = jnp.dot(tri, g, precision=lax.Precision.HIGHEST,
                     preferred_element_type=F32)
        g_t = g.T
        bm_t = bm.T
        for h in range(M_HEADS):
            sl = slice(h * M_HEAD_DIM, (h + 1) * M_HEAD_DIM)
            q = q_ref[rows, sl]
            k = k_ref[rows, sl]
            v = v_ref[rows, sl]
            li_col = g[:, h:h + 1]
            b_col = bm[:, M_HEADS + h:M_HEADS + h + 1]
            li_row = g_t[h:h + 1, :]
            b_row = bm_t[M_HEADS + h:M_HEADS + h + 1, :]
            b_last = b_col[L - 1:L, :]
            m_prev = m_scr[h:h + 1, 0:1]
            c_prev = c_scr[h]
            n_prev = n_scr[h:h + 1, :]

            dmat = jnp.where(causal, b_col - b_row + li_row, NEG)
            g_col = b_col + m_prev
            m_t = jnp.maximum(g_col, jnp.max(dmat, axis=-1, keepdims=True))
            smat = lax.dot_general(q, k, nt, preferred_element_type=F32) * jnp.exp(dmat - m_t)
            w_inter = jnp.exp(g_col - m_t)
            num = (w_inter * jnp.dot(q, c_prev.astype(BF16), preferred_element_type=F32)
                   + jnp.dot(smat.astype(BF16), v, preferred_element_type=F32))
            den = (w_inter * jnp.sum(q.astype(F32) * n_prev, axis=-1, keepdims=True)
                   + jnp.sum(smat, axis=-1, keepdims=True))
            hh = num / jnp.maximum(jnp.abs(den), jnp.exp(-m_t))

            w_src = b_last - b_col + li_col
            a = jnp.max(w_src, axis=0, keepdims=True)
            ks = k.astype(F32) * jnp.exp(w_src - a)
            c_loc = lax.dot_general(ks.astype(BF16), v, tn, preferred_element_type=F32)
            n_loc = jnp.sum(ks, axis=0, keepdims=True)
            m_new = jnp.maximum(b_last + m_prev, a)
            decay = jnp.exp(b_last + m_prev - m_new)
            gain = jnp.exp(a - m_new)
            c_scr[h] = decay * c_prev + gain * c_loc
            n_scr[h:h + 1, :] = decay * n_prev + gain * n_loc
            m_scr[h:h + 1, :] = jnp.broadcast_to(m_new, (1, LANES))

            hm = jax.nn.sigmoid(om_ref[rows, sl].astype(F32)) * hh
            mu = jnp.mean(hm, axis=-1, keepdims=True)
            hc = hm - mu
            var = jnp.mean(hc * hc, axis=-1, keepdims=True)
            hn = hc * lax.rsqrt(var + LN_EPS) * nw_ref[:, sl]
            y = (hn + sk_ref[:, sl] * xc_ref[rows, sl].astype(F32)) * _silu(zm_ref[rows, sl].astype(F32))
            y_ref[rows, sl] = y.astype(y_ref.dtype)

    for c in range(q_ref.shape[0] // L):
        chunk(c)


def _mlstm(qm, km, vm, gates, rest3, xc, norm_w, skip):
    bsz, seq, _ = qm.shape
    tm = MLSTM_TM
    row_spec = pl.BlockSpec((None, tm, M_WIDTH), lambda b, i: (b, i, 0))
    return pl.pallas_call(
        _mlstm_kernel,
        grid=(bsz, seq // tm),
        in_specs=[
            row_spec, row_spec, row_spec,
            pl.BlockSpec((None, tm, LANES), lambda b, i: (b, i, 0)),
            pl.BlockSpec((None, tm, M_WIDTH), lambda b, i: (b, i, 2)),
            pl.BlockSpec((None, tm, M_WIDTH), lambda b, i: (b, i, 1)),
            row_spec,
            pl.BlockSpec((1, M_WIDTH), lambda b, i: (0, 0)),
            pl.BlockSpec((1, M_WIDTH), lambda b, i: (0, 0)),
        ],
        out_specs=row_spec,
        out_shape=jax.ShapeDtypeStruct((bsz, seq, M_WIDTH), BF16),
        scratch_shapes=[pltpu.VMEM((M_HEADS, M_HEAD_DIM, M_HEAD_DIM), F32),
                        pltpu.VMEM((8, M_HEAD_DIM), F32),
                        pltpu.VMEM((8, LANES), F32)],
        compiler_params=_params("arbitrary", "arbitrary"),
        name="mlstm",
    )(qm, km, vm, gates, rest3, rest3, xc, norm_w, skip)


def _post_kernel(o0_ref, o1_ref, o2_ref, s0_ref, s1_ref, s2_ref, za_ref, ym_ref, ga_ref, gm_ref,
                 x_ref, ada_ref, wpa_ref, wpm_ref, wout_ref, lng_ref, lnb_ref, out_ref,
                 *, alpha, d_model):
    hd = ATT_HEAD_DIM
    parts = []
    for h in range(ATT_KV_HEADS):
        sl = slice(h * hd, (h + 1) * hd)
        l0 = s0_ref[:, h:h + 1]
        l1 = s1_ref[:, h:h + 1]
        l2 = s2_ref[:, h:h + 1]
        mx = jnp.maximum(jnp.maximum(l0, l1), l2)
        e0 = jnp.exp(l0 - mx)
        e1 = jnp.exp(l1 - mx)
        e2 = jnp.exp(l2 - mx)
        tot = e0 + e1 + e2
        o = ((e0 / tot) * o0_ref[:, sl].astype(F32) + (e1 / tot) * o1_ref[:, sl].astype(F32)
             + (e2 / tot) * o2_ref[:, sl].astype(F32))
        parts.append((o * _silu(za_ref[:, sl].astype(F32))).astype(BF16))
    att_in = jnp.concatenate(parts, axis=1)
    y_att = jnp.dot(att_in, wpa_ref[...], preferred_element_type=F32)
    y_m = jnp.dot(ym_ref[...], wpm_ref[...], preferred_element_type=F32)
    merged = (jax.nn.sigmoid(ga_ref[...].astype(F32)) * y_att
              + jax.nn.sigmoid(gm_ref[...].astype(F32)) * y_m)
    out = jnp.dot(merged.astype(BF16), wout_ref[...], preferred_element_type=F32)
    gate = ada_ref[:, 2 * d_model:3 * d_model]
    res = alpha * x_ref[...] + gate * out
    mu = jnp.mean(res, axis=-1, keepdims=True)
    rc = res - mu
    var = jnp.mean(rc * rc, axis=-1, keepdims=True)
    out_ref[...] = (rc * lax.rsqrt(var + LN_EPS) * lng_ref[...] + lnb_ref[...]).astype(out_ref.dtype)


def _post(o_list, st_list, rest2, ymin2, x2d, ada3, wpa, wpm, wout, ln_g, ln_b, seq, alpha):
    rows, d = x2d.shape
    tm = POST_TM
    tiles_per_seq = seq // tm
    kw = ATT_KV_WIDTH
    o_spec = pl.BlockSpec((tm, kw), lambda i: (i, 0))
    s_spec = pl.BlockSpec((tm, LANES), lambda i: (i, 0))
    za_block = (5 * M_WIDTH) // kw
    const = lambda shape: pl.BlockSpec(shape, lambda i: (0,) * len(shape))
    return pl.pallas_call(
        functools.partial(_post_kernel, alpha=alpha, d_model=d),
        grid=(rows // tm,),
        in_specs=[
            o_spec, o_spec, o_spec, s_spec, s_spec, s_spec,
            pl.BlockSpec((tm, kw), lambda i: (i, za_block)),
            pl.BlockSpec((tm, M_WIDTH), lambda i: (i, 0)),
            pl.BlockSpec((tm, d), lambda i: (i, 3)),
            pl.BlockSpec((tm, d), lambda i: (i, 4)),
            pl.BlockSpec((tm, d), lambda i: (i, 0)),
            pl.BlockSpec((None, 1, 3 * d), lambda i: (i // tiles_per_seq, 0, 0)),
            const((kw, d)), const((M_WIDTH, d)), const((d, d)),
            const((1, d)), const((1, d)),
        ],
        out_specs=pl.BlockSpec((tm, d), lambda i: (i, 0)),
        out_shape=jax.ShapeDtypeStruct((rows, d), x2d.dtype),
        compiler_params=_params("arbitrary"),
        name="post",
    )(*o_list, *st_list, rest2, ymin2, rest2, rest2, x2d, ada3, wpa, wpm, wout, ln_g, ln_b)


def _layer(x, c, positions, w_ada, b_ada, w_in, conv_w, conv_b, w_qm, w_km, w_vm,
           w_if, b_if, mh_norm_w, skip_m, w_pa, w_pm, w_out, ln_g, ln_b, alpha):
    bsz, seq, d = x.shape
    rows = bsz * seq
    x2d = x.reshape(rows, d)

    ada3 = _ada(c, w_ada, b_ada).reshape(bsz, 1, 3 * d)

    half = ATT_HEAD_DIM // 2
    inv = jnp.power(ROPE_THETA, -jnp.arange(half, dtype=F32) / half)
    sign = jnp.concatenate([-jnp.ones((half,), F32), jnp.ones((half,), F32)])
    rope_tab = jnp.zeros((8, LANES), F32).at[0].set(jnp.concatenate([inv, inv])).at[1].set(sign)

    q0, q1r, q2r, kv, kv4r, kv16r, rest = _inproj(x2d, ada3, positions.reshape(rows, 1), rope_tab,
                                                  w_in.astype(BF16), bsz, seq)

    n_sub = seq // ATT_BLOCK
    o_list, st_list = [], []
    for q_g, kv_g, (_, dil) in zip((q0, q1r, q2r), (kv, kv4r, kv16r), ATT_GROUPS):
        kv_sub = kv_g.reshape(bsz, n_sub, ATT_BLOCK, 2 * ATT_KV_WIDTH)
        o_g, st_g = _attention_group(q_g.reshape(bsz, n_sub, ATT_BLOCK, ATT_KV_WIDTH), kv_sub, kv_sub,
                                     bsz, seq, dil)
        o_list.append(o_g.reshape(rows, ATT_KV_WIDTH))
        st_list.append(st_g.reshape(rows, LANES))

    rest3 = rest.reshape(bsz, seq, rest.shape[1])
    n_gate = 2 * M_HEADS
    wif3 = jnp.zeros((3, M_WIDTH, LANES), BF16).at[:, :, :n_gate].set(
        w_if.astype(BF16).reshape(3, M_WIDTH, n_gate))
    bif = jnp.zeros((1, LANES), F32).at[0, :n_gate].set(b_if.astype(F32))
    qm, km, vm, xc, gates = _mpre(rest3, conv_w.astype(F32), conv_b.astype(F32).reshape(1, M_WIDTH),
                                  w_qm.astype(BF16), w_km.astype(BF16), w_vm.astype(BF16), wif3, bif)
    ymin = _mlstm(qm, km, vm, gates, rest3, xc,
                  mh_norm_w.astype(F32).reshape(1, M_WIDTH), skip_m.astype(F32).reshape(1, M_WIDTH))

    out = _post(o_list, st_list, rest, ymin.reshape(rows, M_WIDTH), x2d, ada3,
                w_pa.astype(BF16), w_pm.astype(BF16), w_out.astype(BF16),
                ln_g.astype(F32).reshape(1, d), ln_b.astype(F32).reshape(1, d), seq, alpha)
    return out.reshape(bsz, seq, d)


def kernel(x, c, positions, w_ada, b_ada, w_in, conv_w, conv_b, w_qm, w_km, w_vm, w_if, b_if,
           mh_norm_w, skip_m, w_pa, w_pm, w_out, ln_g, ln_b):
    depth = w_ada.shape[0]
    alpha = (2.0 * depth) ** 0.25
    for l in range(depth):
        x = _layer(x, c, positions, w_ada[l], b_ada[l], w_in[l], conv_w[l], conv_b[l],
                   w_qm[l], w_km[l], w_vm[l], w_if[l], b_if[l], mh_norm_w[l], skip_m[l],
                   w_pa[l], w_pm[l], w_out[l], ln_g[l], ln_b[l], alpha)
    return x
```

```python
import functools

import jax
import jax.numpy as jnp
from jax import lax
from jax.experimental import pallas as pl
from jax.experimental.pallas import tpu as pltpu

F32 = jnp.float32
BF16 = jnp.bfloat16

ATT_HEAD_DIM = 128
ATT_GROUPS = ((128, 1), (512, 4), (2048, 16))
ATT_KV_HEADS = 4
ATT_KV_WIDTH = ATT_KV_HEADS * ATT_HEAD_DIM
ATT_BLOCK = 128
ROPE_THETA = 10000.0
M_HEADS = 4
M_HEAD_DIM = 256
M_WIDTH = M_HEADS * M_HEAD_DIM
M_CONV = 4
M_CHUNK = 128
LN_EPS = 1e-5
NEG = -1e30

V7X_VMEM_LIMIT_BYTES = 56 * 1024 * 1024
LANES = 128

INPROJ_TM = 1024
INPROJ_TN = 1024
ATT_SUB = 4
ATT_SPAN_SUBS = 16
MPRE_TM = 512
MLSTM_TM = 512
POST_TM = 512
CONV_HALO = 16


def _sigmoid(v):
    return 0.5 * jnp.tanh(0.5 * v) + 0.5


def _silu(v):
    return v * _sigmoid(v)


def _params(*sem):
    return pltpu.CompilerParams(dimension_semantics=sem, vmem_limit_bytes=V7X_VMEM_LIMIT_BYTES)


def _ada_kernel(c_ref, w_ref, b_ref, o_ref):
    sc = _silu(c_ref[...])
    o_ref[...] = jnp.dot(sc, w_ref[...], precision=lax.Precision.HIGHEST,
                         preferred_element_type=F32) + b_ref[...]


def _ada(c, w_ada, b_ada):
    bsz, d = c.shape
    n = w_ada.shape[1]
    return pl.pallas_call(
        _ada_kernel,
        grid=(n // d,),
        in_specs=[pl.BlockSpec((bsz, d), lambda j: (0, 0)),
                  pl.BlockSpec((d, d), lambda j: (0, j)),
                  pl.BlockSpec((1, d), lambda j: (0, j))],
        out_specs=pl.BlockSpec((bsz, d), lambda j: (0, j)),
        out_shape=jax.ShapeDtypeStruct((bsz, n), F32),
        compiler_params=_params("arbitrary"),
        name="ada",
    )(c, w_ada, b_ada.reshape(1, n))


def _inproj_kernel(x_ref, ada_ref, pos_ref, tab_ref, w_ref,
                   q0_ref, q1_ref, q2_ref, kv_ref, kv4_ref, kv16_ref, za_ref, rest_ref,
                   h_scr, cos_scr, sin_scr, stage_scr, stage4_scr, *, d_model):
    j = pl.program_id(1)
    tm = x_ref.shape[0]
    hd, kw = ATT_HEAD_DIM, ATT_KV_WIDTH
    dil4, dil16 = ATT_GROUPS[1][1], ATT_GROUPS[2][1]
    outer = dil16 // dil4

    @pl.when(j == 0)
    def _():
        x = x_ref[...]
        mu = jnp.mean(x, axis=-1, keepdims=True)
        xc = x - mu
        var = jnp.mean(xc * xc, axis=-1, keepdims=True)
        xn = xc * lax.rsqrt(var + LN_EPS)
        shift = ada_ref[:, 0:d_model]
        scale = ada_ref[:, d_model:2 * d_model]
        h_scr[...] = (xn * (1.0 + scale) + shift).astype(BF16)
        ang = pos_ref[...].astype(F32) * tab_ref[0:1, :]
        cos_scr[...] = jnp.cos(ang)
        sin_scr[...] = jnp.sin(ang) * tab_ref[1:2, :]

    def project():
        return jnp.dot(h_scr[...], w_ref[...], preferred_element_type=F32)

    def heads(acc, rope, scale):
        for hh in range(ATT_KV_HEADS):
            xh = acc[:, hh * hd:(hh + 1) * hd]
            if rope:
                rot = pltpu.roll(xh, hd // 2, axis=1)
                xh = xh * cos_scr[...] + rot * sin_scr[...]
            if scale is not None:
                xh = xh * scale
            yield hh, xh

    def stage_heads(s, acc, rope, scale, nat_ref=None):
        for hh, val in heads(acc, rope, scale):
            stage_scr[s, hh] = val
            if nat_ref is not None:
                nat_ref[:, hh * hd:(hh + 1) * hd] = val.astype(nat_ref.dtype)
        for hh in range(ATT_KV_HEADS):
            for r in range(dil4):
                stage4_scr[s, hh * dil4 + r] = stage_scr[s, hh, pl.ds(r, tm // dil4, stride=dil4), :]

    def emit_d4(s, dst_ref):
        for hh in range(ATT_KV_HEADS):
            for r in range(dil4):
                for nn in range(tm // (ATT_BLOCK * dil4)):
                    rows = stage4_scr[s, hh * dil4 + r, nn * ATT_BLOCK:(nn + 1) * ATT_BLOCK, :]
                    dst_ref[nn, r, :, hh * hd:(hh + 1) * hd] = rows.astype(dst_ref.dtype)

    def emit_d16(s, dst_ref):
        for hh in range(ATT_KV_HEADS):
            for r in range(dil4):
                for a in range(outer):
                    rows = stage4_scr[s, hh * dil4 + r, pl.ds(a, tm // dil16, stride=outer), :]
                    dst_ref[a * dil4 + r, :, hh * hd:(hh + 1) * hd] = rows.astype(dst_ref.dtype)

    q_scale = hd ** -0.5

    @pl.when(j == 0)
    def _():
        acc = project()
        for hh, val in heads(acc[:, :kw], True, q_scale):
            q0_ref[:, hh * hd:(hh + 1) * hd] = val.astype(q0_ref.dtype)
        stage_heads(0, acc[:, kw:], True, q_scale)
        emit_d4(0, q1_ref)

    @pl.when(j == 1)
    def _():
        acc = project()
        stage_heads(0, acc[:, :kw], True, q_scale)
        emit_d16(0, q2_ref)
        stage_heads(1, acc[:, kw:], True, None, kv_ref)
        emit_d4(1, kv4_ref)
        emit_d16(1, kv16_ref)

    @pl.when(j == 2)
    def _():
        acc = project()
        stage_heads(0, acc[:, :kw], False, None, kv_ref)
        emit_d4(0, kv4_ref)
        emit_d16(0, kv16_ref)
        za_ref[...] = acc[:, kw:].astype(za_ref.dtype)

    @pl.when(j >= 3)
    def _():
        rest_ref[...] = project().astype(rest_ref.dtype)


def _inproj(x2d, ada3, pos2d, rope_tab, w_in_bf16, bsz, seq):
    rows, d = x2d.shape
    n_in = w_in_bf16.shape[1]
    tm, tn = INPROJ_TM, INPROJ_TN
    kw = ATT_KV_WIDTH
    n_j = n_in // tn
    n_rest = n_j - 3
    tps = seq // tm
    dil4, dil16 = ATT_GROUPS[1][1], ATT_GROUPS[2][1]
    blk4, blk16 = ATT_BLOCK * dil4, ATT_BLOCK * dil16
    assert tm % blk4 == 0 and blk16 % tm == 0 and tn == 2 * kw and dil16 % dil4 == 0
    t16 = blk16 // tm

    def kv_col(j):
        return jnp.clip(j - 1, 0, 1)

    out_shapes = (
        jax.ShapeDtypeStruct((rows, kw), BF16),
        jax.ShapeDtypeStruct((bsz, seq // blk4, dil4, ATT_BLOCK, kw), BF16),
        jax.ShapeDtypeStruct((bsz, seq // blk16, dil16, ATT_BLOCK, kw), BF16),
        jax.ShapeDtypeStruct((rows, 2 * kw), BF16),
        jax.ShapeDtypeStruct((bsz, seq // blk4, dil4, ATT_BLOCK, 2 * kw), BF16),
        jax.ShapeDtypeStruct((bsz, seq // blk16, dil16, ATT_BLOCK, 2 * kw), BF16),
        jax.ShapeDtypeStruct((rows, kw), BF16),
        jax.ShapeDtypeStruct((rows, n_rest * tn), BF16),
    )
    r4_block = (None, tm // blk4, dil4, ATT_BLOCK, kw)
    r16_block = (None, None, dil16, tm // dil16, kw)
    r4_map = lambda i, j: (i // tps, i % tps, 0, 0, 0)
    r16_map = lambda i, j: (i // tps, (i % tps) // t16, 0, (i % tps) % t16, 0)
    return pl.pallas_call(
        functools.partial(_inproj_kernel, d_model=d),
        grid=(rows // tm, n_j),
        in_specs=[
            pl.BlockSpec((tm, d), lambda i, j: (i, 0)),
            pl.BlockSpec((None, 1, 3 * d), lambda i, j: (i // tps, 0, 0)),
            pl.BlockSpec((tm, 1), lambda i, j: (i, 0)),
            pl.BlockSpec((8, LANES), lambda i, j: (0, 0)),
            pl.BlockSpec((d, tn), lambda i, j: (0, j)),
        ],
        out_specs=(
            pl.BlockSpec((tm, kw), lambda i, j: (i, 0)),
            pl.BlockSpec(r4_block, r4_map),
            pl.BlockSpec(r16_block, r16_map),
            pl.BlockSpec((tm, kw), lambda i, j: (i, kv_col(j))),
            pl.BlockSpec(r4_block, lambda i, j: r4_map(i, j)[:4] + (kv_col(j),)),
            pl.BlockSpec(r16_block, lambda i, j: r16_map(i, j)[:4] + (kv_col(j),)),
            pl.BlockSpec((tm, kw), lambda i, j: (i, 0)),
            pl.BlockSpec((tm, tn), lambda i, j: (i, jnp.clip(j - 3, 0, n_rest - 1))),
        ),
        out_shape=out_shapes,
        scratch_shapes=[pltpu.VMEM((tm, d), BF16),
                        pltpu.VMEM((tm, LANES), F32),
                        pltpu.VMEM((tm, LANES), F32),
                        pltpu.VMEM((2, ATT_KV_HEADS, tm, ATT_HEAD_DIM), F32),
                        pltpu.VMEM((2, ATT_KV_HEADS * dil4, tm // dil4, ATT_HEAD_DIM), F32)],
        compiler_params=_params("arbitrary", "arbitrary"),
        name="inproj",
    )(x2d, ada3, pos2d, rope_tab, w_in_bf16)


def _attn_kernel(q_ref, kv_ref, kvp_ref, o_ref, st_ref, onat_scr, *, dil):
    span = pl.program_id(1)
    g = pl.program_id(2)
    hd, kw = ATT_HEAD_DIM, ATT_KV_WIDTH
    groups_per_span = ATT_SPAN_SUBS // ATT_SUB
    if dil == 1:
        first = jnp.logical_and(span == 0, g == 0)
    else:
        first = (span * groups_per_span + g) < dil // ATT_SUB
    thr_first = jnp.where(first, 2 * ATT_BLOCK, 0).astype(jnp.int32)

    row = lax.broadcasted_iota(jnp.int32, (ATT_BLOCK, 2 * ATT_BLOCK), 0)
    col = lax.broadcasted_iota(jnp.int32, (ATT_BLOCK, 2 * ATT_BLOCK), 1)
    in_prev = col < ATT_BLOCK
    rel = col - row
    ones = jnp.ones((2 * ATT_BLOCK, hd), BF16)
    lane = lax.broadcasted_iota(jnp.int32, (ATT_BLOCK, LANES), 1)
    nt = (((1,), (1,)), ((), ()))

    for u in range(ATT_SUB):
        if dil == 1:
            dst = pl.ds(pl.multiple_of((g * ATT_SUB + u) * ATT_BLOCK, ATT_BLOCK), ATT_BLOCK)
        elif dil == ATT_SUB:
            dst = pl.ds(g * (ATT_BLOCK * dil) + u, ATT_BLOCK, stride=dil)
        else:
            dst = pl.ds(g * ATT_SUB + u, ATT_BLOCK, stride=dil)

        scores, vexts = [], []
        for h in range(ATT_KV_HEADS):
            ksl = slice(h * hd, (h + 1) * hd)
            vsl = slice(kw + h * hd, kw + (h + 1) * hd)
            q = q_ref[u, :, ksl]
            if dil == 1 and u > 0:
                k_prev, v_prev, thr = kv_ref[u - 1, :, ksl], kv_ref[u - 1, :, vsl], jnp.int32(0)
            elif dil == 1:
                k_prev, v_prev, thr = kvp_ref[0, :, ksl], kvp_ref[0, :, vsl], thr_first
            else:
                k_prev, v_prev, thr = kvp_ref[u, :, ksl], kvp_ref[u, :, vsl], thr_first
            k_cat = jnp.concatenate([k_prev, kv_ref[u, :, ksl]], axis=0)
            v_cat = jnp.concatenate([v_prev, kv_ref[u, :, vsl]], axis=0)
            vexts.append(jnp.concatenate([v_cat, ones], axis=1))
            s = lax.dot_general(q, k_cat, nt, preferred_element_type=F32)
            slack = jnp.where(in_prev, rel - thr, ATT_BLOCK - rel)
            scores.append(jnp.where(slack >= 0, s, NEG))

        probs, maxes = [], []
        for h in range(ATT_KV_HEADS):
            m = jnp.max(scores[h], axis=-1, keepdims=True)
            maxes.append(m)
            probs.append(jnp.exp(scores[h] - m).astype(BF16))

        stats = jnp.zeros((ATT_BLOCK, LANES), F32)
        for h in range(ATT_KV_HEADS):
            oe = jnp.dot(probs[h], vexts[h], preferred_element_type=F32)
            den = oe[:, hd:]
            onat_scr[h, dst, :] = oe[:, :hd] / den
            stats = jnp.where(lane == h, maxes[h] + jnp.log(den[:, 0:1]), stats)
        st_ref[dst, :] = stats

    @pl.when(g == groups_per_span - 1)
    def _():
        for h in range(ATT_KV_HEADS):
            o_ref[:, h * hd:(h + 1) * hd] = onat_scr[h].astype(o_ref.dtype)


def _attention_group(q_sub, kv_sub, kv_prev, bsz, seq, dil):
    kw = ATT_KV_WIDTH
    span_rows = ATT_SPAN_SUBS * ATT_BLOCK
    gps = ATT_SPAN_SUBS // ATT_SUB
    assert ATT_SPAN_SUBS % dil == 0 and (dil == 1 or dil % ATT_SUB == 0)
    own_map = lambda b, s, g: (b, s * gps + g, 0, 0)
    if dil == 1:
        prev_spec = pl.BlockSpec((None, 1, ATT_BLOCK, 2 * kw),
                                 lambda b, s, g: (b, jnp.maximum((s * gps + g) * ATT_SUB - 1, 0), 0, 0))
    else:
        prev_spec = pl.BlockSpec((None, ATT_SUB, ATT_BLOCK, 2 * kw),
                                 lambda b, s, g: (b, jnp.maximum(s * gps + g - dil // ATT_SUB, 0), 0, 0))
    return pl.pallas_call(
        functools.partial(_attn_kernel, dil=dil),
        grid=(bsz, seq // span_rows, gps),
        in_specs=[
            pl.BlockSpec((None, ATT_SUB, ATT_BLOCK, kw), own_map),
            pl.BlockSpec((None, ATT_SUB, ATT_BLOCK, 2 * kw), own_map),
            prev_spec,
        ],
        out_specs=(
            pl.BlockSpec((None, span_rows, kw), lambda b, s, g: (b, s, 0)),
            pl.BlockSpec((None, span_rows, LANES), lambda b, s, g: (b, s, 0)),
        ),
        out_shape=(jax.ShapeDtypeStruct((bsz, seq, kw), BF16),
                   jax.ShapeDtypeStruct((bsz, seq, LANES), F32)),
        scratch_shapes=[pltpu.VMEM((ATT_KV_HEADS, span_rows, ATT_HEAD_DIM), F32)],
        compiler_params=_params("arbitrary", "arbitrary", "arbitrary"),
        name=f"attn_d{dil}",
    )(q_sub, kv_sub, kv_prev)


def _mpre_kernel(xm_ref, halo_ref, cw_ref, cb_ref, wq_ref, wk_ref, wv_ref, wif_ref, bif_ref,
                 qm_ref, km_ref, vm_ref, xc_ref, g_ref, ext_scr):
    i = pl.program_id(1)
    tm = xm_ref.shape[0]
    xm = xm_ref[...]
    halo = halo_ref[...].astype(F32)
    ext_scr[0:CONV_HALO, :] = jnp.where(i > 0, halo, 0.0)
    ext_scr[CONV_HALO:CONV_HALO + tm, :] = xm.astype(F32)
    acc = jnp.zeros((tm, M_WIDTH), F32) + cb_ref[...]
    for k in range(M_CONV):
        start = CONV_HALO - (M_CONV - 1) + k
        acc = acc + cw_ref[k:k + 1, :] * ext_scr[pl.ds(start, tm), :]
    xc = _silu(acc)
    xc_b = xc.astype(BF16)
    xc_ref[...] = xc_b
    k_scale = M_HEAD_DIM ** -0.5
    gates = jnp.zeros((tm, LANES), F32) + bif_ref[...]
    for h in range(M_HEADS):
        sl = slice(h * M_HEAD_DIM, (h + 1) * M_HEAD_DIM)
        qh = jnp.dot(xc_b[:, sl], wq_ref[h], preferred_element_type=F32).astype(BF16)
        kh = (jnp.dot(xc_b[:, sl], wk_ref[h], preferred_element_type=F32) * k_scale).astype(BF16)
        vh = jnp.dot(xm[:, sl], wv_ref[h], preferred_element_type=F32).astype(BF16)
        qm_ref[:, sl] = qh
        km_ref[:, sl] = kh
        vm_ref[:, sl] = vh
        gates = gates + jnp.dot(qh, wif_ref[0, sl, :], preferred_element_type=F32)
        gates = gates + jnp.dot(kh, wif_ref[1, sl, :], preferred_element_type=F32)
        gates = gates + jnp.dot(vh, wif_ref[2, sl, :], preferred_element_type=F32)
    lane = lax.broadcasted_iota(jnp.int32, gates.shape, 1)
    log_f = jnp.minimum(gates, 0.0) - jnp.log1p(jnp.exp(-jnp.abs(gates)))
    g_ref[...] = jnp.where(lane >= M_HEADS, log_f, gates)


def _mpre(rest3, conv_w, conv_b, wq, wk, wv, wif3, bif):
    bsz, seq, _ = rest3.shape
    tm = MPRE_TM
    halo_blocks = tm // CONV_HALO
    full = lambda shape: pl.BlockSpec(shape, lambda b, i: (0,) * len(shape))
    big = jax.ShapeDtypeStruct((bsz, seq, M_WIDTH), BF16)
    return pl.pallas_call(
        _mpre_kernel,
        grid=(bsz, seq // tm),
        in_specs=[
            pl.BlockSpec((None, tm, M_WIDTH), lambda b, i: (b, i, 0)),
            pl.BlockSpec((None, CONV_HALO, M_WIDTH),
                         lambda b, i: (b, jnp.maximum(i * halo_blocks - 1, 0), 0)),
            full((M_CONV, M_WIDTH)), full((1, M_WIDTH)),
            full((M_HEADS, M_HEAD_DIM, M_HEAD_DIM)), full((M_HEADS, M_HEAD_DIM, M_HEAD_DIM)),
            full((M_HEADS, M_HEAD_DIM, M_HEAD_DIM)),
            full((3, M_WIDTH, LANES)), full((1, LANES)),
        ],
        out_specs=(
            pl.BlockSpec((None, tm, M_WIDTH), lambda b, i: (b, i, 0)),
            pl.BlockSpec((None, tm, M_WIDTH), lambda b, i: (b, i, 0)),
            pl.BlockSpec((None, tm, M_WIDTH), lambda b, i: (b, i, 0)),
            pl.BlockSpec((None, tm, M_WIDTH), lambda b, i: (b, i, 0)),
            pl.BlockSpec((None, tm, LANES), lambda b, i: (b, i, 0)),
        ),
        out_shape=(big, big, big, big, jax.ShapeDtypeStruct((bsz, seq, LANES), F32)),
        scratch_shapes=[pltpu.VMEM((tm + CONV_HALO, M_WIDTH), F32)],
        compiler_params=_params("arbitrary", "arbitrary"),
        name="mpre",
    )(rest3, rest3, conv_w, conv_b, wq, wk, wv, wif3, bif)


def _mlstm_kernel(q_ref, k_ref, v_ref, g_ref, om_ref, zm_ref, xc_ref, nw_ref, sk_ref,
                  y_ref, c_scr, m_scr):
    @pl.when(pl.program_id(1) == 0)
    def _():
        c_scr[...] = jnp.zeros_like(c_scr)
        m_scr[...] = jnp.zeros_like(m_scr)

    L, E = M_CHUNK, M_HEAD_DIM
    row = lax.broadcasted_iota(jnp.int32, (L, L), 0)
    col = lax.broadcasted_iota(jnp.int32, (L, L), 1)
    causal = col <= row
    tri = causal.astype(F32)
    ones_blk = jnp.ones((L, LANES), BF16)
    nt = (((1,), (1,)), ((), ()))
    tn = (((0,), (0,)), ((), ()))

    def chunk(c):
        rows = pl.ds(c * L, L)
        g = g_ref[rows, :]
        bm = jnp.dot(tri, g, precision=lax.Precision.HIGHEST,
                     preferred_element_type=F32)
        g_t = g.T
        bm_t = bm.T
        for h in range(M_HEADS):
            sl = slice(h * E, (h + 1) * E)
            q = q_ref[rows, sl]
            k = k_ref[rows, sl]
            v_ext = jnp.concatenate([v_ref[rows, sl], ones_blk], axis=1)
            li_col = g[:, h:h + 1]
            b_col = bm[:, M_HEADS + h:M_HEADS + h + 1]
            li_row = g_t[h:h + 1, :]
            b_row = bm_t[M_HEADS + h:M_HEADS + h + 1, :]
            b_last = b_col[L - 1:L, :]

            dmat = jnp.where(causal, b_col - b_row + li_row, NEG)
            m_intra = jnp.max(dmat, axis=-1, keepdims=True)
            pmat = lax.dot_general(q, k, nt, preferred_element_type=F32) * jnp.exp(dmat - m_intra)
            y_intra = jnp.dot(pmat.astype(BF16), v_ext, preferred_element_type=F32)
            w_src = b_last - b_col + li_col
            a = jnp.max(w_src, axis=0, keepdims=True)
            m_prev = m_scr[h:h + 1, 0:1]
            m_new = jnp.maximum(b_last + m_prev, a)
            decay = jnp.exp(b_last + m_prev - m_new)
            e_k = jnp.broadcast_to(jnp.exp(w_src - m_new), (L, LANES)).astype(BF16)
            ks = jnp.concatenate([k[:, j * LANES:(j + 1) * LANES] * e_k for j in range(E // LANES)], axis=1)
            c_loc = lax.dot_general(ks, v_ext, tn, preferred_element_type=F32)

            c_prev = c_scr[h]
            g_col = b_col + m_prev
            m_t = jnp.maximum(g_col, m_intra)
            y_all = (jnp.exp(m_intra - m_t) * y_intra
                     + jnp.exp(g_col - m_t) * jnp.dot(q, c_prev.astype(BF16), preferred_element_type=F32))
            inv_den = 1.0 / jnp.maximum(jnp.abs(y_all[:, E:]), jnp.exp(-m_t))
            hh = jnp.concatenate([y_all[:, j * LANES:(j + 1) * LANES] * inv_den for j in range(E // LANES)],
                                 axis=1)

            c_scr[h] = decay * c_prev + c_loc
            m_scr[h:h + 1, :] = jnp.broadcast_to(m_new, (1, LANES))

            hm = _sigmoid(om_ref[rows, sl].astype(F32)) * hh
            mu = jnp.mean(hm, axis=-1, keepdims=True)
            hc = hm - mu
            var = jnp.mean(hc * hc, axis=-1, keepdims=True)
            hn = hc * lax.rsqrt(var + LN_EPS) * nw_ref[:, sl]
            y = (hn + sk_ref[:, sl] * xc_ref[rows, sl].astype(F32)) * _silu(zm_ref[rows, sl].astype(F32))
            y_ref[rows, sl] = y.astype(y_ref.dtype)

    for c in range(q_ref.shape[0] // L):
        chunk(c)


def _mlstm(qm, km, vm, gates, rest3, xc, norm_w, skip):
    bsz, seq, _ = qm.shape
    tm = MLSTM_TM
    row_spec = pl.BlockSpec((None, tm, M_WIDTH), lambda b, i: (b, i, 0))
    return pl.pallas_call(
        _mlstm_kernel,
        grid=(bsz, seq // tm),
        in_specs=[
            row_spec, row_spec, row_spec,
            pl.BlockSpec((None, tm, LANES), lambda b, i: (b, i, 0)),
            pl.BlockSpec((None, tm, M_WIDTH), lambda b, i: (b, i, 2)),
            pl.BlockSpec((None, tm, M_WIDTH), lambda b, i: (b, i, 1)),
            row_spec,
            pl.BlockSpec((1, M_WIDTH), lambda b, i: (0, 0)),
            pl.BlockSpec((1, M_WIDTH), lambda b, i: (0, 0)),
        ],
        out_specs=row_spec,
        out_shape=jax.ShapeDtypeStruct((bsz, seq, M_WIDTH), BF16),
        scratch_shapes=[pltpu.VMEM((M_HEADS, M_HEAD_DIM, M_HEAD_DIM + LANES), F32),
                        pltpu.VMEM((8, LANES), F32)],
        compiler_params=_params("arbitrary", "arbitrary"),
        name="mlstm",
    )(qm, km, vm, gates, rest3, rest3, xc, norm_w, skip)


def _post_kernel(o0_ref, o1_ref, o2_ref, s0_ref, s1_ref, s2_ref, za_ref, ym_ref, ga_ref, gm_ref,
                 x_ref, ada_ref, wpa_ref, wpm_ref, wout_ref, lng_ref, lnb_ref, out_ref,
                 *, alpha, d_model):
    hd = ATT_HEAD_DIM
    parts = []
    for h in range(ATT_KV_HEADS):
        sl = slice(h * hd, (h + 1) * hd)
        l0 = s0_ref[:, h:h + 1]
        l1 = s1_ref[:, h:h + 1]
        l2 = s2_ref[:, h:h + 1]
        mx = jnp.maximum(jnp.maximum(l0, l1), l2)
        e0 = jnp.exp(l0 - mx)
        e1 = jnp.exp(l1 - mx)
        e2 = jnp.exp(l2 - mx)
        tot = e0 + e1 + e2
        o = ((e0 / tot) * o0_ref[:, sl].astype(F32) + (e1 / tot) * o1_ref[:, sl].astype(F32)
             + (e2 / tot) * o2_ref[:, sl].astype(F32))
        parts.append((o * _silu(za_ref[:, sl].astype(F32))).astype(BF16))
    att_in = jnp.concatenate(parts, axis=1)
    y_att = jnp.dot(att_in, wpa_ref[...], preferred_element_type=F32)
    y_m = jnp.dot(ym_ref[...], wpm_ref[...], preferred_element_type=F32)
    merged = (_sigmoid(ga_ref[...].astype(F32)) * y_att
              + _sigmoid(gm_ref[...].astype(F32)) * y_m)
    out = jnp.dot(merged.astype(BF16), wout_ref[...], preferred_element_type=F32)
    gate = ada_ref[:, 2 * d_model:3 * d_model]
    res = alpha * x_ref[...] + gate * out
    mu = jnp.mean(res, axis=-1, keepdims=True)
    rc = res - mu
    var = jnp.mean(rc * rc, axis=-1, keepdims=True)
    out_ref[...] = (rc * lax.rsqrt(var + LN_EPS) * lng_ref[...] + lnb_ref[...]).astype(out_ref.dtype)


def _post(o_list, st_list, za2, rest2, ymin2, x2d, ada3, wpa, wpm, wout, ln_g, ln_b, seq, alpha):
    rows, d = x2d.shape
    tm = POST_TM
    tiles_per_seq = seq // tm
    kw = ATT_KV_WIDTH
    o_spec = pl.BlockSpec((tm, kw), lambda i: (i, 0))
    s_spec = pl.BlockSpec((tm, LANES), lambda i: (i, 0))
    const = lambda shape: pl.BlockSpec(shape, lambda i: (0,) * len(shape))
    return pl.pallas_call(
        functools.partial(_post_kernel, alpha=alpha, d_model=d),
        grid=(rows // tm,),
        in_specs=[
            o_spec, o_spec, o_spec, s_spec, s_spec, s_spec,
            pl.BlockSpec((tm, kw), lambda i: (i, 0)),
            pl.BlockSpec((tm, M_WIDTH), lambda i: (i, 0)),
            pl.BlockSpec((tm, d), lambda i: (i, 3)),
            pl.BlockSpec((tm, d), lambda i: (i, 4)),
            pl.BlockSpec((tm, d), lambda i: (i, 0)),
            pl.BlockSpec((None, 1, 3 * d), lambda i: (i // tiles_per_seq, 0, 0)),
            const((kw, d)), const((M_WIDTH, d)), const((d, d)),
            const((1, d)), const((1, d)),
        ],
        out_specs=pl.BlockSpec((tm, d), lambda i: (i, 0)),
        out_shape=jax.ShapeDtypeStruct((rows, d), x2d.dtype),
        compiler_params=_params("arbitrary"),
        name="post",
    )(*o_list, *st_list, za2, ymin2, rest2, rest2, x2d, ada3, wpa, wpm, wout, ln_g, ln_b)


def _layer(x, c, positions, w_ada, b_ada, w_in, conv_w, conv_b, w_qm, w_km, w_vm,
           w_if, b_if, mh_norm_w, skip_m, w_pa, w_pm, w_out, ln_g, ln_b, alpha):
    bsz, seq, d = x.shape
    rows = bsz * seq
    x2d = x.reshape(rows, d)

    ada3 = _ada(c, w_ada, b_ada).reshape(bsz, 1, 3 * d)

    half = ATT_HEAD_DIM // 2
    inv = jnp.power(ROPE_THETA, -jnp.arange(half, dtype=F32) / half)
    sign = jnp.concatenate([-jnp.ones((half,), F32), jnp.ones((half,), F32)])
    rope_tab = jnp.zeros((8, LANES), F32).at[0].set(jnp.concatenate([inv, inv])).at[1].set(sign)

    q0, q1r, q2r, kv, kv4r, kv16r, za, rest = _inproj(x2d, ada3, positions.reshape(rows, 1), rope_tab,
                                                      w_in.astype(BF16), bsz, seq)

    n_sub = seq // ATT_BLOCK
    o_list, st_list = [], []
    for q_g, kv_g, (_, dil) in zip((q0, q1r, q2r), (kv, kv4r, kv16r), ATT_GROUPS):
        kv_sub = kv_g.reshape(bsz, n_sub, ATT_BLOCK, 2 * ATT_KV_WIDTH)
        o_g, st_g = _attention_group(q_g.reshape(bsz, n_sub, ATT_BLOCK, ATT_KV_WIDTH), kv_sub, kv_sub,
                                     bsz, seq, dil)
        o_list.append(o_g.reshape(rows, ATT_KV_WIDTH))
        st_list.append(st_g.reshape(rows, LANES))

    rest3 = rest.reshape(bsz, seq, rest.shape[1])
    n_gate = 2 * M_HEADS
    wif3 = jnp.zeros((3, M_WIDTH, LANES), BF16).at[:, :, :n_gate].set(
        w_if.astype(BF16).reshape(3, M_WIDTH, n_gate))
    bif = jnp.zeros((1, LANES), F32).at[0, :n_gate].set(b_if.astype(F32))
    qm, km, vm, xc, gates = _mpre(rest3, conv_w.astype(F32), conv_b.astype(F32).reshape(1, M_WIDTH),
                                  w_qm.astype(BF16), w_km.astype(BF16), w_vm.astype(BF16), wif3, bif)
    ymin = _mlstm(qm, km, vm, gates, rest3, xc,
                  mh_norm_w.astype(F32).reshape(1, M_WIDTH), skip_m.astype(F32).reshape(1, M_WIDTH))

    out = _post(o_list, st_list, za, rest, ymin.reshape(rows, M_WIDTH), x2d, ada3,
                w_pa.astype(BF16), w_pm.astype(BF16), w_out.astype(BF16),
                ln_g.astype(F32).reshape(1, d), ln_b.astype(F32).reshape(1, d), seq, alpha)
    return out.reshape(bsz, seq, d)


def kernel(x, c, positions, w_ada, b_ada, w_in, conv_w, conv_b, w_qm, w_km, w_vm, w_if, b_if,
           mh_norm_w, skip_m, w_pa, w_pm, w_out, ln_g, ln_b):
    depth = w_ada.shape[0]
    alpha = (2.0 * depth) ** 0.25
    for l in range(depth):
        x = _layer(x, c, positions, w_ada[l], b_ada[l], w_in[l], conv_w[l], conv_b[l],
                   w_qm[l], w_km[l], w_vm[l], w_if[l], b_if[l], mh_norm_w[l], skip_m[l],
                   w_pa[l], w_pm[l], w_out[l], ln_g[l], ln_b[l], alpha)
    return x
```

```python
import functools

import jax
import jax.numpy as jnp
from jax import lax
from jax.experimental import pallas as pl
from jax.experimental.pallas import tpu as pltpu

F32 = jnp.float32
BF16 = jnp.bfloat16

ATT_HEAD_DIM = 128
ATT_GROUPS = ((128, 1), (512, 4), (2048, 16))
ATT_KV_HEADS = 4
ATT_KV_WIDTH = ATT_KV_HEADS * ATT_HEAD_DIM
ATT_BLOCK = 128
ROPE_THETA = 10000.0
M_HEADS = 4
M_HEAD_DIM = 256
M_WIDTH = M_HEADS * M_HEAD_DIM
M_CONV = 4
M_CHUNK = 128
LN_EPS = 1e-5
NEG = -1e30

V7X_VMEM_LIMIT_BYTES = 56 * 1024 * 1024
LANES = 128

INPROJ_TM = 1024
INPROJ_TN = 1024
ATT_SUB = 4
ATT_SPAN_SUBS = 16
MPRE_TM = 512
MLSTM_TM = 512
POST_TM = 512
POST_PARTS = 2
CONV_HALO = 16


def _sigmoid(v):
    return 0.5 * jnp.tanh(0.5 * v) + 0.5


def _silu(v):
    return v * _sigmoid(v)


def _params(*sem):
    return pltpu.CompilerParams(dimension_semantics=sem, vmem_limit_bytes=V7X_VMEM_LIMIT_BYTES)


def _ada_kernel(c_ref, w_ref, b_ref, o_ref):
    sc = _silu(c_ref[...])
    o_ref[...] = jnp.dot(sc, w_ref[...], precision=lax.Precision.HIGHEST,
                         preferred_element_type=F32) + b_ref[...]


def _ada(c, w_ada, b_ada):
    bsz, d = c.shape
    n = w_ada.shape[1]
    return pl.pallas_call(
        _ada_kernel,
        grid=(n // d,),
        in_specs=[pl.BlockSpec((bsz, d), lambda j: (0, 0)),
                  pl.BlockSpec((d, d), lambda j: (0, j)),
                  pl.BlockSpec((1, d), lambda j: (0, j))],
        out_specs=pl.BlockSpec((bsz, d), lambda j: (0, j)),
        out_shape=jax.ShapeDtypeStruct((bsz, n), F32),
        compiler_params=_params("arbitrary"),
        name="ada",
    )(c, w_ada, b_ada.reshape(1, n))


def _inproj_kernel(x_ref, ada_ref, pos_ref, tab_ref, w_ref,
                   q0_ref, q1_ref, q2_ref, kv_ref, kv4_ref, kv16_ref, za_ref, rest_ref,
                   h_scr, cos_scr, sin_scr, stage_scr, stage4_scr, *, d_model):
    j = pl.program_id(1)
    tm = x_ref.shape[0]
    hd, kw = ATT_HEAD_DIM, ATT_KV_WIDTH
    dil4, dil16 = ATT_GROUPS[1][1], ATT_GROUPS[2][1]
    outer = dil16 // dil4

    @pl.when(j == 0)
    def _():
        x = x_ref[...]
        mu = jnp.mean(x, axis=-1, keepdims=True)
        xc = x - mu
        var = jnp.mean(xc * xc, axis=-1, keepdims=True)
        xn = xc * lax.rsqrt(var + LN_EPS)
        shift = ada_ref[:, 0:d_model]
        scale = ada_ref[:, d_model:2 * d_model]
        h_scr[...] = (xn * (1.0 + scale) + shift).astype(BF16)
        ang = pos_ref[...].astype(F32) * tab_ref[0:1, :]
        cos_scr[...] = jnp.cos(ang)
        sin_scr[...] = jnp.sin(ang) * tab_ref[1:2, :]

    def project():
        return jnp.dot(h_scr[...], w_ref[...], preferred_element_type=F32)

    def heads(acc, rope, scale):
        for hh in range(ATT_KV_HEADS):
            xh = acc[:, hh * hd:(hh + 1) * hd]
            if rope:
                rot = pltpu.roll(xh, hd // 2, axis=1)
                xh = xh * cos_scr[...] + rot * sin_scr[...]
            if scale is not None:
                xh = xh * scale
            yield hh, xh

    def stage_heads(s, acc, rope, scale, nat_ref=None):
        for hh, val in heads(acc, rope, scale):
            stage_scr[s, hh] = val
            if nat_ref is not None:
                nat_ref[:, hh * hd:(hh + 1) * hd] = val.astype(nat_ref.dtype)
        for hh in range(ATT_KV_HEADS):
            for r in range(dil4):
                stage4_scr[s, hh * dil4 + r] = stage_scr[s, hh, pl.ds(r, tm // dil4, stride=dil4), :]

    def emit_d4(s, dst_ref):
        for hh in range(ATT_KV_HEADS):
            for r in range(dil4):
                for nn in range(tm // (ATT_BLOCK * dil4)):
                    rows = stage4_scr[s, hh * dil4 + r, nn * ATT_BLOCK:(nn + 1) * ATT_BLOCK, :]
                    dst_ref[nn, r, :, hh * hd:(hh + 1) * hd] = rows.astype(dst_ref.dtype)

    def emit_d16(s, dst_ref):
        for hh in range(ATT_KV_HEADS):
            for r in range(dil4):
                for a in range(outer):
                    rows = stage4_scr[s, hh * dil4 + r, pl.ds(a, tm // dil16, stride=outer), :]
                    dst_ref[a * dil4 + r, :, hh * hd:(hh + 1) * hd] = rows.astype(dst_ref.dtype)

    q_scale = hd ** -0.5

    @pl.when(j == 0)
    def _():
        acc = project()
        for hh, val in heads(acc[:, :kw], True, q_scale):
            q0_ref[:, hh * hd:(hh + 1) * hd] = val.astype(q0_ref.dtype)
        stage_heads(0, acc[:, kw:], True, q_scale)
        emit_d4(0, q1_ref)

    @pl.when(j == 1)
    def _():
        acc = project()
        stage_heads(0, acc[:, :kw], True, q_scale)
        emit_d16(0, q2_ref)
        stage_heads(1, acc[:, kw:], True, None, kv_ref)
        emit_d4(1, kv4_ref)
        emit_d16(1, kv16_ref)

    @pl.when(j == 2)
    def _():
        acc = project()
        stage_heads(0, acc[:, :kw], False, None, kv_ref)
        emit_d4(0, kv4_ref)
        emit_d16(0, kv16_ref)
        za_ref[...] = acc[:, kw:].astype(za_ref.dtype)

    @pl.when(j >= 3)
    def _():
        rest_ref[...] = project().astype(rest_ref.dtype)


def _inproj(x2d, ada3, pos2d, rope_tab, w_in_bf16, bsz, seq):
    rows, d = x2d.shape
    n_in = w_in_bf16.shape[1]
    tm, tn = INPROJ_TM, INPROJ_TN
    kw = ATT_KV_WIDTH
    n_j = n_in // tn
    n_rest = n_j - 3
    tps = seq // tm
    dil4, dil16 = ATT_GROUPS[1][1], ATT_GROUPS[2][1]
    blk4, blk16 = ATT_BLOCK * dil4, ATT_BLOCK * dil16
    assert tm % blk4 == 0 and blk16 % tm == 0 and tn == 2 * kw and dil16 % dil4 == 0
    t16 = blk16 // tm

    def kv_col(j):
        return jnp.clip(j - 1, 0, 1)

    out_shapes = (
        jax.ShapeDtypeStruct((rows, kw), BF16),
        jax.ShapeDtypeStruct((bsz, seq // blk4, dil4, ATT_BLOCK, kw), BF16),
        jax.ShapeDtypeStruct((bsz, seq // blk16, dil16, ATT_BLOCK, kw), BF16),
        jax.ShapeDtypeStruct((rows, 2 * kw), BF16),
        jax.ShapeDtypeStruct((bsz, seq // blk4, dil4, ATT_BLOCK, 2 * kw), BF16),
        jax.ShapeDtypeStruct((bsz, seq // blk16, dil16, ATT_BLOCK, 2 * kw), BF16),
        jax.ShapeDtypeStruct((rows, kw), BF16),
        jax.ShapeDtypeStruct((rows, n_rest * tn), BF16),
    )
    r4_block = (None, tm // blk4, dil4, ATT_BLOCK, kw)
    r16_block = (None, None, dil16, tm // dil16, kw)
    r4_map = lambda i, j: (i // tps, i % tps, 0, 0, 0)
    r16_map = lambda i, j: (i // tps, (i % tps) // t16, 0, (i % tps) % t16, 0)
    return pl.pallas_call(
        functools.partial(_inproj_kernel, d_model=d),
        grid=(rows // tm, n_j),
        in_specs=[
            pl.BlockSpec((tm, d), lambda i, j: (i, 0)),
            pl.BlockSpec((None, 1, 3 * d), lambda i, j: (i // tps, 0, 0)),
            pl.BlockSpec((tm, 1), lambda i, j: (i, 0)),
            pl.BlockSpec((8, LANES), lambda i, j: (0, 0)),
            pl.BlockSpec((d, tn), lambda i, j: (0, j)),
        ],
        out_specs=(
            pl.BlockSpec((tm, kw), lambda i, j: (i, 0)),
            pl.BlockSpec(r4_block, r4_map),
            pl.BlockSpec(r16_block, r16_map),
            pl.BlockSpec((tm, kw), lambda i, j: (i, kv_col(j))),
            pl.BlockSpec(r4_block, lambda i, j: r4_map(i, j)[:4] + (kv_col(j),)),
            pl.BlockSpec(r16_block, lambda i, j: r16_map(i, j)[:4] + (kv_col(j),)),
            pl.BlockSpec((tm, kw), lambda i, j: (i, 0)),
            pl.BlockSpec((tm, tn), lambda i, j: (i, jnp.clip(j - 3, 0, n_rest - 1))),
        ),
        out_shape=out_shapes,
        scratch_shapes=[pltpu.VMEM((tm, d), BF16),
                        pltpu.VMEM((tm, LANES), F32),
                        pltpu.VMEM((tm, LANES), F32),
                        pltpu.VMEM((2, ATT_KV_HEADS, tm, ATT_HEAD_DIM), F32),
                        pltpu.VMEM((2, ATT_KV_HEADS * dil4, tm // dil4, ATT_HEAD_DIM), F32)],
        compiler_params=_params("arbitrary", "arbitrary"),
        name="inproj",
    )(x2d, ada3, pos2d, rope_tab, w_in_bf16)


def _attn_kernel(q_ref, kv_ref, kvp_ref, o_ref, st_ref, onat_scr, *, dil):
    span = pl.program_id(1)
    g = pl.program_id(2)
    hd, kw = ATT_HEAD_DIM, ATT_KV_WIDTH
    groups_per_span = ATT_SPAN_SUBS // ATT_SUB
    if dil == 1:
        first = jnp.logical_and(span == 0, g == 0)
    else:
        first = (span * groups_per_span + g) < dil // ATT_SUB
    thr_first = jnp.where(first, 2 * ATT_BLOCK, 0).astype(jnp.int32)

    row = lax.broadcasted_iota(jnp.int32, (ATT_BLOCK, 2 * ATT_BLOCK), 0)
    col = lax.broadcasted_iota(jnp.int32, (ATT_BLOCK, 2 * ATT_BLOCK), 1)
    in_prev = col < ATT_BLOCK
    rel = col - row
    ones = jnp.ones((2 * ATT_BLOCK, hd), BF16)
    lane = lax.broadcasted_iota(jnp.int32, (ATT_BLOCK, LANES), 1)
    nt = (((1,), (1,)), ((), ()))

    def dst_rows(u):
        if dil == 1:
            return pl.ds(pl.multiple_of((g * ATT_SUB + u) * ATT_BLOCK, ATT_BLOCK), ATT_BLOCK)
        if dil == ATT_SUB:
            return pl.ds(g * (ATT_BLOCK * dil) + u, ATT_BLOCK, stride=dil)
        return pl.ds(g * ATT_SUB + u, ATT_BLOCK, stride=dil)

    units = [(u, h) for u in range(ATT_SUB) for h in range(ATT_KV_HEADS)]
    scores, vexts = {}, {}
    for u, h in units:
        ksl = slice(h * hd, (h + 1) * hd)
        vsl = slice(kw + h * hd, kw + (h + 1) * hd)
        q = q_ref[u, :, ksl]
        if dil == 1 and u > 0:
            k_prev, v_prev, thr = kv_ref[u - 1, :, ksl], kv_ref[u - 1, :, vsl], jnp.int32(0)
        elif dil == 1:
            k_prev, v_prev, thr = kvp_ref[0, :, ksl], kvp_ref[0, :, vsl], thr_first
        else:
            k_prev, v_prev, thr = kvp_ref[u, :, ksl], kvp_ref[u, :, vsl], thr_first
        k_cat = jnp.concatenate([k_prev, kv_ref[u, :, ksl]], axis=0)
        v_cat = jnp.concatenate([v_prev, kv_ref[u, :, vsl]], axis=0)
        vexts[u, h] = jnp.concatenate([v_cat, ones], axis=1)
        s = lax.dot_general(q, k_cat, nt, preferred_element_type=F32)
        slack = jnp.where(in_prev, rel - thr, ATT_BLOCK - rel)
        scores[u, h] = jnp.where(slack >= 0, s, NEG)
    maxes = {un: jnp.max(scores[un], axis=-1, keepdims=True) for un in units}
    probs = {un: jnp.exp(scores[un] - maxes[un]).astype(BF16) for un in units}
    outs = {un: jnp.dot(probs[un], vexts[un], preferred_element_type=F32) for un in units}
    for u in range(ATT_SUB):
        dst = dst_rows(u)
        stats = jnp.zeros((ATT_BLOCK, LANES), F32)
        for h in range(ATT_KV_HEADS):
            den = outs[u, h][:, hd:]
            onat_scr[h, dst, :] = outs[u, h][:, :hd] / den
            stats = jnp.where(lane == h, maxes[u, h] + jnp.log(den[:, 0:1]), stats)
        st_ref[dst, :] = stats

    @pl.when(g == groups_per_span - 1)
    def _():
        for h in range(ATT_KV_HEADS):
            o_ref[:, h * hd:(h + 1) * hd] = onat_scr[h].astype(o_ref.dtype)


def _attention_group(q_sub, kv_sub, kv_prev, bsz, seq, dil):
    kw = ATT_KV_WIDTH
    span_rows = ATT_SPAN_SUBS * ATT_BLOCK
    gps = ATT_SPAN_SUBS // ATT_SUB
    assert ATT_SPAN_SUBS % dil == 0 and (dil == 1 or dil % ATT_SUB == 0)
    own_map = lambda b, s, g: (b, s * gps + g, 0, 0)
    if dil == 1:
        prev_spec = pl.BlockSpec((None, 1, ATT_BLOCK, 2 * kw),
                                 lambda b, s, g: (b, jnp.maximum((s * gps + g) * ATT_SUB - 1, 0), 0, 0))
    else:
        prev_spec = pl.BlockSpec((None, ATT_SUB, ATT_BLOCK, 2 * kw),
                                 lambda b, s, g: (b, jnp.maximum(s * gps + g - dil // ATT_SUB, 0), 0, 0))
    return pl.pallas_call(
        functools.partial(_attn_kernel, dil=dil),
        grid=(bsz, seq // span_rows, gps),
        in_specs=[
            pl.BlockSpec((None, ATT_SUB, ATT_BLOCK, kw), own_map),
            pl.BlockSpec((None, ATT_SUB, ATT_BLOCK, 2 * kw), own_map),
            prev_spec,
        ],
        out_specs=(
            pl.BlockSpec((None, span_rows, kw), lambda b, s, g: (b, s, 0)),
            pl.BlockSpec((None, span_rows, LANES), lambda b, s, g: (b, s, 0)),
        ),
        out_shape=(jax.ShapeDtypeStruct((bsz, seq, kw), BF16),
                   jax.ShapeDtypeStruct((bsz, seq, LANES), F32)),
        scratch_shapes=[pltpu.VMEM((ATT_KV_HEADS, span_rows, ATT_HEAD_DIM), F32)],
        compiler_params=_params("arbitrary", "arbitrary", "arbitrary"),
        name=f"attn_d{dil}",
    )(q_sub, kv_sub, kv_prev)


def _mpre_kernel(xm_ref, halo_ref, cw_ref, cb_ref, wq_ref, wk_ref, wkt_ref, wv_ref, wif_ref, bif_ref,
                 qm_ref, kt_ref, vm_ref, xc_ref, g_ref, ext_scr):
    i = pl.program_id(1)
    tm = xm_ref.shape[0]
    xm = xm_ref[...]
    halo = halo_ref[...].astype(F32)
    ext_scr[0:CONV_HALO, :] = jnp.where(i > 0, halo, 0.0)
    ext_scr[CONV_HALO:CONV_HALO + tm, :] = xm.astype(F32)
    acc = jnp.zeros((tm, M_WIDTH), F32) + cb_ref[...]
    for k in range(M_CONV):
        start = CONV_HALO - (M_CONV - 1) + k
        acc = acc + cw_ref[k:k + 1, :] * ext_scr[pl.ds(start, tm), :]
    xc = _silu(acc)
    xc_b = xc.astype(BF16)
    xc_ref[...] = xc_b
    gates = jnp.zeros((tm, LANES), F32) + bif_ref[...]
    for h in range(M_HEADS):
        sl = slice(h * M_HEAD_DIM, (h + 1) * M_HEAD_DIM)
        qh = jnp.dot(xc_b[:, sl], wq_ref[h], preferred_element_type=F32).astype(BF16)
        kh = jnp.dot(xc_b[:, sl], wk_ref[h], preferred_element_type=F32).astype(BF16)
        vh = jnp.dot(xm[:, sl], wv_ref[h], preferred_element_type=F32).astype(BF16)
        qm_ref[:, sl] = qh
        vm_ref[:, sl] = vh
        kt_ref[h] = lax.dot_general(wkt_ref[h], xc_b[:, sl], (((1,), (1,)), ((), ())),
                                    preferred_element_type=F32).astype(BF16)
        gates = gates + jnp.dot(qh, wif_ref[0, sl, :], preferred_element_type=F32)
        gates = gates + jnp.dot(kh, wif_ref[1, sl, :], preferred_element_type=F32)
        gates = gates + jnp.dot(vh, wif_ref[2, sl, :], preferred_element_type=F32)
    lane = lax.broadcasted_iota(jnp.int32, gates.shape, 1)
    log_f = jnp.minimum(gates, 0.0) - jnp.log1p(jnp.exp(-jnp.abs(gates)))
    g_ref[...] = jnp.where(lane >= M_HEADS, log_f, gates)


def _mpre(rest3, conv_w, conv_b, wq, wk, wkt, wv, wif3, bif):
    bsz, seq, _ = rest3.shape
    tm = MPRE_TM
    halo_blocks = tm // CONV_HALO
    full = lambda shape: pl.BlockSpec(shape, lambda b, i: (0,) * len(shape))
    big = jax.ShapeDtypeStruct((bsz, seq, M_WIDTH), BF16)
    return pl.pallas_call(
        _mpre_kernel,
        grid=(bsz, seq // tm),
        in_specs=[
            pl.BlockSpec((None, tm, M_WIDTH), lambda b, i: (b, i, 0)),
            pl.BlockSpec((None, CONV_HALO, M_WIDTH),
                         lambda b, i: (b, jnp.maximum(i * halo_blocks - 1, 0), 0)),
            full((M_CONV, M_WIDTH)), full((1, M_WIDTH)),
            full((M_HEADS, M_HEAD_DIM, M_HEAD_DIM)), full((M_HEADS, M_HEAD_DIM, M_HEAD_DIM)),
            full((M_HEADS, M_HEAD_DIM, M_HEAD_DIM)), full((M_HEADS, M_HEAD_DIM, M_HEAD_DIM)),
            full((3, M_WIDTH, LANES)), full((1, LANES)),
        ],
        out_specs=(
            pl.BlockSpec((None, tm, M_WIDTH), lambda b, i: (b, i, 0)),
            pl.BlockSpec((None, M_HEADS, M_HEAD_DIM, tm), lambda b, i: (b, 0, 0, i)),
            pl.BlockSpec((None, tm, M_WIDTH), lambda b, i: (b, i, 0)),
            pl.BlockSpec((None, tm, M_WIDTH), lambda b, i: (b, i, 0)),
            pl.BlockSpec((None, tm, LANES), lambda b, i: (b, i, 0)),
        ),
        out_shape=(big, jax.ShapeDtypeStruct((bsz, M_HEADS, M_HEAD_DIM, seq), BF16), big, big,
                   jax.ShapeDtypeStruct((bsz, seq, LANES), F32)),
        scratch_shapes=[pltpu.VMEM((tm + CONV_HALO, M_WIDTH), F32)],
        compiler_params=_params("arbitrary", "arbitrary"),
        name="mpre",
    )(rest3, rest3, conv_w, conv_b, wq, wk, wkt, wv, wif3, bif)


def _mlstm_kernel(q_ref, kt_ref, v_ref, g_ref, om_ref, zm_ref, xc_ref, nw_ref, sk_ref,
                  y_ref, c_scr, m_scr):
    @pl.when(pl.program_id(1) == 0)
    def _():
        c_scr[...] = jnp.zeros_like(c_scr)
        m_scr[...] = jnp.zeros_like(m_scr)

    L, E = M_CHUNK, M_HEAD_DIM
    row = lax.broadcasted_iota(jnp.int32, (L, L), 0)
    col = lax.broadcasted_iota(jnp.int32, (L, L), 1)
    causal = col <= row
    tri = causal.astype(F32)
    ones_blk = jnp.ones((L, LANES), BF16)

    def chunk(c):
        rows = pl.ds(c * L, L)
        heads = range(M_HEADS)
        hsl = [slice(h * E, (h + 1) * E) for h in heads]
        g = g_ref[rows, :]
        bm = jnp.dot(tri, g, precision=lax.Precision.HIGHEST,
                     preferred_element_type=F32)
        g_t = g.T
        bm_t = bm.T

        b_col = [bm[:, M_HEADS + h:M_HEADS + h + 1] for h in heads]
        li_row = [g_t[h:h + 1, :] for h in heads]
        b_row = [bm_t[M_HEADS + h:M_HEADS + h + 1, :] for h in heads]
        b_last = [b_col[h][L - 1:L, :] for h in heads]
        dmat = [jnp.where(causal, b_col[h] - b_row[h] + li_row[h], NEG) for h in heads]
        m_intra = [jnp.max(dmat[h], axis=-1, keepdims=True) for h in heads]
        e_intra = [jnp.exp(dmat[h] - m_intra[h]) for h in heads]
        w_src = [b_last[h] - b_row[h] + li_row[h] for h in heads]
        a = [jnp.max(w_src[h], axis=1, keepdims=True) for h in heads]
        m_prev = [m_scr[h:h + 1, 0:1] for h in heads]
        m_new = [jnp.maximum(b_last[h] + m_prev[h], a[h]) for h in heads]
        decay = [jnp.exp(b_last[h] + m_prev[h] - m_new[h]) for h in heads]
        e_key = [jnp.exp(w_src[h] - m_new[h]).astype(BF16) for h in heads]
        g_col = [b_col[h] + m_prev[h] for h in heads]
        m_t = [jnp.maximum(g_col[h], m_intra[h]) for h in heads]
        w_intra = [jnp.exp(m_intra[h] - m_t[h]) for h in heads]
        w_state = [jnp.exp(g_col[h] - m_t[h]) for h in heads]
        floor = [jnp.exp(-m_t[h]) for h in heads]

        q = [q_ref[rows, hsl[h]] for h in heads]
        kt = [kt_ref[h, :, c * L:(c + 1) * L] for h in heads]
        v_ext = [jnp.concatenate([v_ref[rows, hsl[h]], ones_blk], axis=1) for h in heads]
        qk = [jnp.dot(q[h], kt[h], preferred_element_type=F32) for h in heads]
        c_prev = [c_scr[h] for h in heads]
        y_state = [jnp.dot(q[h], c_prev[h].astype(BF16), preferred_element_type=F32) for h in heads]
        pmat = [(qk[h] * e_intra[h]).astype(BF16) for h in heads]
        y_intra = [jnp.dot(pmat[h], v_ext[h], preferred_element_type=F32) for h in heads]
        c_loc = [jnp.dot(kt[h] * e_key[h], v_ext[h], preferred_element_type=F32) for h in heads]

        y_all = [w_intra[h] * y_intra[h] + w_state[h] * y_state[h] for h in heads]
        for h in heads:
            c_scr[h] = decay[h] * c_prev[h] + c_loc[h]
            m_scr[h:h + 1, :] = jnp.broadcast_to(m_new[h], (1, LANES))
        inv_den = [1.0 / jnp.maximum(jnp.abs(y_all[h][:, E:]), floor[h]) for h in heads]
        hh = [jnp.concatenate([y_all[h][:, j * LANES:(j + 1) * LANES] * inv_den[h]
                               for j in range(E // LANES)], axis=1) for h in heads]

        hm = [_sigmoid(om_ref[rows, hsl[h]]).astype(F32) * hh[h] for h in heads]
        mu = [jnp.mean(hm[h], axis=-1, keepdims=True) for h in heads]
        hc = [hm[h] - mu[h] for h in heads]
        var = [jnp.mean(hc[h] * hc[h], axis=-1, keepdims=True) for h in heads]
        for h in heads:
            hn = hc[h] * lax.rsqrt(var[h] + LN_EPS) * nw_ref[:, hsl[h]]
            y_ref[rows, hsl[h]] = ((hn + sk_ref[:, hsl[h]] * xc_ref[rows, hsl[h]].astype(F32)).astype(BF16)
                                   * _silu(zm_ref[rows, hsl[h]]))

    for c in range(q_ref.shape[0] // L):
        chunk(c)


def _mlstm(qm, kmt, vm, gates, rest3, xc, norm_w, skip):
    bsz, seq, _ = qm.shape
    tm = MLSTM_TM
    row_spec = pl.BlockSpec((None, tm, M_WIDTH), lambda b, i: (b, i, 0))
    return pl.pallas_call(
        _mlstm_kernel,
        grid=(bsz, seq // tm),
        in_specs=[
            row_spec,
            pl.BlockSpec((None, M_HEADS, M_HEAD_DIM, tm), lambda b, i: (b, 0, 0, i)),
            row_spec,
            pl.BlockSpec((None, tm, LANES), lambda b, i: (b, i, 0)),
            pl.BlockSpec((None, tm, M_WIDTH), lambda b, i: (b, i, 2)),
            pl.BlockSpec((None, tm, M_WIDTH), lambda b, i: (b, i, 1)),
            row_spec,
            pl.BlockSpec((1, M_WIDTH), lambda b, i: (0, 0)),
            pl.BlockSpec((1, M_WIDTH), lambda b, i: (0, 0)),
        ],
        out_specs=row_spec,
        out_shape=jax.ShapeDtypeStruct((bsz, seq, M_WIDTH), BF16),
        scratch_shapes=[pltpu.VMEM((M_HEADS, M_HEAD_DIM, M_HEAD_DIM + LANES), F32),
                        pltpu.VMEM((8, LANES), F32)],
        compiler_params=_params("arbitrary", "arbitrary"),
        name="mlstm",
    )(qm, kmt, vm, gates, rest3, rest3, xc, norm_w, skip)


def _post_kernel(o0_ref, o1_ref, o2_ref, s0_ref, s1_ref, s2_ref, za_ref, ym_ref, ga_ref, gm_ref,
                 x_ref, ada_ref, wpa_ref, wpm_ref, wout_ref, lng_ref, lnb_ref, out_ref,
                 *, alpha, d_model):
    hd = ATT_HEAD_DIM
    tm = x_ref.shape[0]
    gate = ada_ref[:, 2 * d_model:3 * d_model]
    for part in range(POST_PARTS):
        rs = slice(part * (tm // POST_PARTS), (part + 1) * (tm // POST_PARTS))
        parts = []
        for h in range(ATT_KV_HEADS):
            sl = slice(h * hd, (h + 1) * hd)
            l0 = s0_ref[rs, h:h + 1]
            l1 = s1_ref[rs, h:h + 1]
            l2 = s2_ref[rs, h:h + 1]
            mx = jnp.maximum(jnp.maximum(l0, l1), l2)
            e0 = jnp.exp(l0 - mx)
            e1 = jnp.exp(l1 - mx)
            e2 = jnp.exp(l2 - mx)
            tot = e0 + e1 + e2
            o = ((e0 / tot) * o0_ref[rs, sl].astype(F32) + (e1 / tot) * o1_ref[rs, sl].astype(F32)
                 + (e2 / tot) * o2_ref[rs, sl].astype(F32))
            parts.append(o.astype(BF16) * _silu(za_ref[rs, sl]))
        att_in = jnp.concatenate(parts, axis=1)
        y_att = jnp.dot(att_in, wpa_ref[...], preferred_element_type=F32)
        y_m = jnp.dot(ym_ref[rs, :], wpm_ref[...], preferred_element_type=F32)
        merged = (_sigmoid(ga_ref[rs, :]) * y_att.astype(BF16)
                  + _sigmoid(gm_ref[rs, :]) * y_m.astype(BF16))
        out = jnp.dot(merged, wout_ref[...], preferred_element_type=F32)
        res = alpha * x_ref[rs, :] + gate * out
        mu = jnp.mean(res, axis=-1, keepdims=True)
        rc = res - mu
        var = jnp.mean(rc * rc, axis=-1, keepdims=True)
        out_ref[rs, :] = (rc * lax.rsqrt(var + LN_EPS) * lng_ref[...] + lnb_ref[...]).astype(out_ref.dtype)


def _post(o_list, st_list, za2, rest2, ymin2, x2d, ada3, wpa, wpm, wout, ln_g, ln_b, seq, alpha):
    rows, d = x2d.shape
    tm = POST_TM
    tiles_per_seq = seq // tm
    kw = ATT_KV_WIDTH
    o_spec = pl.BlockSpec((tm, kw), lambda i: (i, 0))
    s_spec = pl.BlockSpec((tm, LANES), lambda i: (i, 0))
    const = lambda shape: pl.BlockSpec(shape, lambda i: (0,) * len(shape))
    return pl.pallas_call(
        functools.partial(_post_kernel, alpha=alpha, d_model=d),
        grid=(rows // tm,),
        in_specs=[
            o_spec, o_spec, o_spec, s_spec, s_spec, s_spec,
            pl.BlockSpec((tm, kw), lambda i: (i, 0)),
            pl.BlockSpec((tm, M_WIDTH), lambda i: (i, 0)),
            pl.BlockSpec((tm, d), lambda i: (i, 3)),
            pl.BlockSpec((tm, d), lambda i: (i, 4)),
            pl.BlockSpec((tm, d), lambda i: (i, 0)),
            pl.BlockSpec((None, 1, 3 * d), lambda i: (i // tiles_per_seq, 0, 0)),
            const((kw, d)), const((M_WIDTH, d)), const((d, d)),
            const((1, d)), const((1, d)),
        ],
        out_specs=pl.BlockSpec((tm, d), lambda i: (i, 0)),
        out_shape=jax.ShapeDtypeStruct((rows, d), x2d.dtype),
        compiler_params=_params("arbitrary"),
        name="post",
    )(*o_list, *st_list, za2, ymin2, rest2, rest2, x2d, ada3, wpa, wpm, wout, ln_g, ln_b)


def _layer(x, c, positions, w_ada, b_ada, w_in, conv_w, conv_b, w_qm, w_km, w_vm,
           w_if, b_if, mh_norm_w, skip_m, w_pa, w_pm, w_out, ln_g, ln_b, alpha):
    bsz, seq, d = x.shape
    rows = bsz * seq
    x2d = x.reshape(rows, d)

    ada3 = _ada(c, w_ada, b_ada).reshape(bsz, 1, 3 * d)

    half = ATT_HEAD_DIM // 2
    inv = jnp.power(ROPE_THETA, -jnp.arange(half, dtype=F32) / half)
    sign = jnp.concatenate([-jnp.ones((half,), F32), jnp.ones((half,), F32)])
    rope_tab = jnp.zeros((8, LANES), F32).at[0].set(jnp.concatenate([inv, inv])).at[1].set(sign)

    q0, q1r, q2r, kv, kv4r, kv16r, za, rest = _inproj(x2d, ada3, positions.reshape(rows, 1), rope_tab,
                                                      w_in.astype(BF16), bsz, seq)

    n_sub = seq // ATT_BLOCK
    o_list, st_list = [], []
    for q_g, kv_g, (_, dil) in zip((q0, q1r, q2r), (kv, kv4r, kv16r), ATT_GROUPS):
        kv_sub = kv_g.reshape(bsz, n_sub, ATT_BLOCK, 2 * ATT_KV_WIDTH)
        o_g, st_g = _attention_group(q_g.reshape(bsz, n_sub, ATT_BLOCK, ATT_KV_WIDTH), kv_sub, kv_sub,
                                     bsz, seq, dil)
        o_list.append(o_g.reshape(rows, ATT_KV_WIDTH))
        st_list.append(st_g.reshape(rows, LANES))

    rest3 = rest.reshape(bsz, seq, rest.shape[1])
    n_gate = 2 * M_HEADS
    wif3 = jnp.zeros((3, M_WIDTH, LANES), BF16).at[:, :, :n_gate].set(
        w_if.astype(BF16).reshape(3, M_WIDTH, n_gate))
    bif = jnp.zeros((1, LANES), F32).at[0, :n_gate].set(b_if.astype(F32))
    wk = (w_km * (M_HEAD_DIM ** -0.5)).astype(BF16)
    qm, kmt, vm, xc, gates = _mpre(rest3, conv_w.astype(F32), conv_b.astype(F32).reshape(1, M_WIDTH),
                                   w_qm.astype(BF16), wk, wk.transpose(0, 2, 1), w_vm.astype(BF16), wif3, bif)
    ymin = _mlstm(qm, kmt, vm, gates, rest3, xc,
                  mh_norm_w.astype(F32).reshape(1, M_WIDTH), skip_m.astype(F32).reshape(1, M_WIDTH))

    out = _post(o_list, st_list, za, rest, ymin.reshape(rows, M_WIDTH), x2d, ada3,
                w_pa.astype(BF16), w_pm.astype(BF16), w_out.astype(BF16),
                ln_g.astype(F32).reshape(1, d), ln_b.astype(F32).reshape(1, d), seq, alpha)
    return out.reshape(bsz, seq, d)


def kernel(x, c, positions, w_ada, b_ada, w_in, conv_w, conv_b, w_qm, w_km, w_vm, w_if, b_if,
           mh_norm_w, skip_m, w_pa, w_pm, w_out, ln_g, ln_b):
    depth = w_ada.shape[0]
    alpha = (2.0 * depth) ** 0.25
    for l in range(depth):
        x = _layer(x, c, positions, w_ada[l], b_ada[l], w_in[l], conv_w[l], conv_b[l],
                   w_qm[l], w_km[l], w_vm[l], w_if[l], b_if[l], mh_norm_w[l], skip_m[l],
                   w_pa[l], w_pm[l], w_out[l], ln_g[l], ln_b[l], alpha)
    return x
```

```python
import functools

import jax
import jax.numpy as jnp
from jax import lax
from jax.experimental import pallas as pl
from jax.experimental.pallas import tpu as pltpu

F32 = jnp.float32
BF16 = jnp.bfloat16

ATT_HEAD_DIM = 128
ATT_GROUPS = ((128, 1), (512, 4), (2048, 16))
ATT_KV_HEADS = 4
ATT_KV_WIDTH = ATT_KV_HEADS * ATT_HEAD_DIM
ATT_BLOCK = 128
ROPE_THETA = 10000.0
M_HEADS = 4
M_HEAD_DIM = 256
M_WIDTH = M_HEADS * M_HEAD_DIM
M_CONV = 4
M_CHUNK = 128
LN_EPS = 1e-5
NEG = -1e30

V7X_VMEM_LIMIT_BYTES = 56 * 1024 * 1024
LANES = 128

INPROJ_TM = 1024
INPROJ_TN = 1024
ATT_SUB = 4
ATT_SPAN_SUBS = 16
MPRE_TM = 512
MLSTM_TM = 512
POST_TM = 1024
POST_PARTS = 4
CONV_HALO = 16


def _sigmoid(v):
    return 0.5 * jnp.tanh(0.5 * v) + 0.5


def _silu(v):
    return v * _sigmoid(v)


def _params(*sem):
    return pltpu.CompilerParams(dimension_semantics=sem, vmem_limit_bytes=V7X_VMEM_LIMIT_BYTES)


def _ada_kernel(c_ref, w_ref, b_ref, o_ref):
    sc = _silu(c_ref[...])
    o_ref[...] = jnp.dot(sc, w_ref[...], precision=lax.Precision.HIGHEST,
                         preferred_element_type=F32) + b_ref[...]


def _ada(c, w_ada, b_ada):
    bsz, d = c.shape
    n = w_ada.shape[1]
    return pl.pallas_call(
        _ada_kernel,
        grid=(n // d,),
        in_specs=[pl.BlockSpec((bsz, d), lambda j: (0, 0)),
                  pl.BlockSpec((d, d), lambda j: (0, j)),
                  pl.BlockSpec((1, d), lambda j: (0, j))],
        out_specs=pl.BlockSpec((bsz, d), lambda j: (0, j)),
        out_shape=jax.ShapeDtypeStruct((bsz, n), F32),
        compiler_params=_params("arbitrary"),
        name="ada",
    )(c, w_ada, b_ada.reshape(1, n))


def _inproj_kernel(x_ref, ada_ref, pos_ref, tab_ref, w_ref,
                   q0_ref, q1_ref, q2_ref, kv_ref, kv4_ref, kv16_ref, za_ref, rest_ref,
                   h_scr, cos_scr, sin_scr, stage_scr, stage4_scr, *, d_model):
    j = pl.program_id(1)
    tm = x_ref.shape[0]
    hd, kw = ATT_HEAD_DIM, ATT_KV_WIDTH
    dil4, dil16 = ATT_GROUPS[1][1], ATT_GROUPS[2][1]
    outer = dil16 // dil4

    @pl.when(j == 0)
    def _():
        x = x_ref[...]
        mu = jnp.mean(x, axis=-1, keepdims=True)
        xc = x - mu
        var = jnp.mean(xc * xc, axis=-1, keepdims=True)
        xn = xc * lax.rsqrt(var + LN_EPS)
        shift = ada_ref[:, 0:d_model]
        scale = ada_ref[:, d_model:2 * d_model]
        h_scr[...] = (xn * (1.0 + scale) + shift).astype(BF16)
        half = tm // 2
        lo_lane = lax.broadcasted_iota(jnp.int32, (half, LANES), 1) < hd // 2
        pos2 = jnp.where(lo_lane, pos_ref[0:half, :], pos_ref[half:tm, :])
        ang = pos2.astype(F32) * tab_ref[0:1, :]
        for fn, dst in ((jnp.cos, cos_scr), (jnp.sin, sin_scr)):
            val = fn(ang)
            swapped = pltpu.roll(val, hd // 2, axis=1)
            dst[0:half, :] = jnp.where(lo_lane, val, swapped)
            dst[half:tm, :] = jnp.where(lo_lane, swapped, val)
        sin_scr[...] = sin_scr[...] * tab_ref[1:2, :]

    def project():
        return jnp.dot(h_scr[...], w_ref[...], preferred_element_type=F32)

    def heads(acc, rope, scale):
        for hh in range(ATT_KV_HEADS):
            xh = acc[:, hh * hd:(hh + 1) * hd]
            if rope:
                rot = pltpu.roll(xh, hd // 2, axis=1)
                xh = xh * cos_scr[...] + rot * sin_scr[...]
            if scale is not None:
                xh = xh * scale
            yield hh, xh

    def stage_heads(s, acc, rope, scale, nat_ref=None):
        for hh, val in heads(acc, rope, scale):
            stage_scr[s, hh] = val
            if nat_ref is not None:
                nat_ref[:, hh * hd:(hh + 1) * hd] = val.astype(nat_ref.dtype)
        for hh in range(ATT_KV_HEADS):
            for r in range(dil4):
                stage4_scr[s, hh * dil4 + r] = stage_scr[s, hh, pl.ds(r, tm // dil4, stride=dil4), :]

    def emit_d4(s, dst_ref):
        for hh in range(ATT_KV_HEADS):
            for r in range(dil4):
                for nn in range(tm // (ATT_BLOCK * dil4)):
                    rows = stage4_scr[s, hh * dil4 + r, nn * ATT_BLOCK:(nn + 1) * ATT_BLOCK, :]
                    dst_ref[nn, r, :, hh * hd:(hh + 1) * hd] = rows.astype(dst_ref.dtype)

    def emit_d16(s, dst_ref):
        for hh in range(ATT_KV_HEADS):
            for r in range(dil4):
                for a in range(outer):
                    rows = stage4_scr[s, hh * dil4 + r, pl.ds(a, tm // dil16, stride=outer), :]
                    dst_ref[a * dil4 + r, :, hh * hd:(hh + 1) * hd] = rows.astype(dst_ref.dtype)

    q_scale = hd ** -0.5

    @pl.when(j == 0)
    def _():
        acc = project()
        for hh, val in heads(acc[:, :kw], True, q_scale):
            q0_ref[:, hh * hd:(hh + 1) * hd] = val.astype(q0_ref.dtype)
        stage_heads(0, acc[:, kw:], True, q_scale)
        emit_d4(0, q1_ref)

    @pl.when(j == 1)
    def _():
        acc = project()
        stage_heads(0, acc[:, :kw], True, q_scale)
        emit_d16(0, q2_ref)
        stage_heads(1, acc[:, kw:], True, None, kv_ref)
        emit_d4(1, kv4_ref)
        emit_d16(1, kv16_ref)

    @pl.when(j == 2)
    def _():
        acc = project()
        stage_heads(0, acc[:, :kw], False, None, kv_ref)
        emit_d4(0, kv4_ref)
        emit_d16(0, kv16_ref)
        za_ref[...] = acc[:, kw:].astype(za_ref.dtype)

    @pl.when(j >= 3)
    def _():
        rest_ref[...] = project().astype(rest_ref.dtype)


def _inproj(x2d, ada3, pos2d, rope_tab, w_in_bf16, bsz, seq):
    rows, d = x2d.shape
    n_in = w_in_bf16.shape[1]
    tm, tn = INPROJ_TM, INPROJ_TN
    kw = ATT_KV_WIDTH
    n_j = n_in // tn
    n_rest = n_j - 3
    tps = seq // tm
    dil4, dil16 = ATT_GROUPS[1][1], ATT_GROUPS[2][1]
    blk4, blk16 = ATT_BLOCK * dil4, ATT_BLOCK * dil16
    assert tm % blk4 == 0 and blk16 % tm == 0 and tn == 2 * kw and dil16 % dil4 == 0
    t16 = blk16 // tm

    def kv_col(j):
        return jnp.clip(j - 1, 0, 1)

    out_shapes = (
        jax.ShapeDtypeStruct((rows, kw), BF16),
        jax.ShapeDtypeStruct((bsz, seq // blk4, dil4, ATT_BLOCK, kw), BF16),
        jax.ShapeDtypeStruct((bsz, seq // blk16, dil16, ATT_BLOCK, kw), BF16),
        jax.ShapeDtypeStruct((rows, 2 * kw), BF16),
        jax.ShapeDtypeStruct((bsz, seq // blk4, dil4, ATT_BLOCK, 2 * kw), BF16),
        jax.ShapeDtypeStruct((bsz, seq // blk16, dil16, ATT_BLOCK, 2 * kw), BF16),
        jax.ShapeDtypeStruct((rows, kw), BF16),
        jax.ShapeDtypeStruct((rows, n_rest * tn), BF16),
    )
    r4_block = (None, tm // blk4, dil4, ATT_BLOCK, kw)
    r16_block = (None, None, dil16, tm // dil16, kw)
    r4_map = lambda i, j: (i // tps, i % tps, 0, 0, 0)
    r16_map = lambda i, j: (i // tps, (i % tps) // t16, 0, (i % tps) % t16, 0)
    return pl.pallas_call(
        functools.partial(_inproj_kernel, d_model=d),
        grid=(rows // tm, n_j),
        in_specs=[
            pl.BlockSpec((tm, d), lambda i, j: (i, 0)),
            pl.BlockSpec((None, 1, 3 * d), lambda i, j: (i // tps, 0, 0)),
            pl.BlockSpec((tm, 1), lambda i, j: (i, 0)),
            pl.BlockSpec((8, LANES), lambda i, j: (0, 0)),
            pl.BlockSpec((d, tn), lambda i, j: (0, j)),
        ],
        out_specs=(
            pl.BlockSpec((tm, kw), lambda i, j: (i, 0)),
            pl.BlockSpec(r4_block, r4_map),
            pl.BlockSpec(r16_block, r16_map),
            pl.BlockSpec((tm, kw), lambda i, j: (i, kv_col(j))),
            pl.BlockSpec(r4_block, lambda i, j: r4_map(i, j)[:4] + (kv_col(j),)),
            pl.BlockSpec(r16_block, lambda i, j: r16_map(i, j)[:4] + (kv_col(j),)),
            pl.BlockSpec((tm, kw), lambda i, j: (i, 0)),
            pl.BlockSpec((tm, tn), lambda i, j: (i, jnp.clip(j - 3, 0, n_rest - 1))),
        ),
        out_shape=out_shapes,
        scratch_shapes=[pltpu.VMEM((tm, d), BF16),
                        pltpu.VMEM((tm, LANES), F32),
                        pltpu.VMEM((tm, LANES), F32),
                        pltpu.VMEM((2, ATT_KV_HEADS, tm, ATT_HEAD_DIM), F32),
                        pltpu.VMEM((2, ATT_KV_HEADS * dil4, tm // dil4, ATT_HEAD_DIM), F32)],
        compiler_params=_params("arbitrary", "arbitrary"),
        name="inproj",
    )(x2d, ada3, pos2d, rope_tab, w_in_bf16)


def _attn_kernel(q_ref, kv_ref, kvp_ref, o_ref, st_ref, onat_scr, *, dil):
    span = pl.program_id(1)
    g = pl.program_id(2)
    hd, kw = ATT_HEAD_DIM, ATT_KV_WIDTH
    groups_per_span = ATT_SPAN_SUBS // ATT_SUB
    if dil == 1:
        first = jnp.logical_and(span == 0, g == 0)
    else:
        first = (span * groups_per_span + g) < dil // ATT_SUB
    thr_first = jnp.where(first, 2 * ATT_BLOCK, 0).astype(jnp.int32)

    row = lax.broadcasted_iota(jnp.int32, (ATT_BLOCK, 2 * ATT_BLOCK), 0)
    col = lax.broadcasted_iota(jnp.int32, (ATT_BLOCK, 2 * ATT_BLOCK), 1)
    in_prev = col < ATT_BLOCK
    rel = col - row

    def mask_bias(thr):
        slack = jnp.where(in_prev, rel - thr, ATT_BLOCK - rel)
        return jnp.where(slack >= 0, 0.0, NEG).astype(F32)

    bias_inner = mask_bias(jnp.int32(0))
    bias_edge = mask_bias(thr_first)
    ones = jnp.ones((2 * ATT_BLOCK, hd), BF16)
    lane = lax.broadcasted_iota(jnp.int32, (ATT_BLOCK, LANES), 1)
    nt = (((1,), (1,)), ((), ()))

    def dst_rows(u):
        if dil == 1:
            return pl.ds(pl.multiple_of((g * ATT_SUB + u) * ATT_BLOCK, ATT_BLOCK), ATT_BLOCK)
        if dil == ATT_SUB:
            return pl.ds(g * (ATT_BLOCK * dil) + u, ATT_BLOCK, stride=dil)
        return pl.ds(g * ATT_SUB + u, ATT_BLOCK, stride=dil)

    units = [(u, h) for u in range(ATT_SUB) for h in range(ATT_KV_HEADS)]
    scores, vexts = {}, {}
    for u, h in units:
        ksl = slice(h * hd, (h + 1) * hd)
        vsl = slice(kw + h * hd, kw + (h + 1) * hd)
        q = q_ref[u, :, ksl]
        if dil == 1 and u > 0:
            k_prev, v_prev, bias = kv_ref[u - 1, :, ksl], kv_ref[u - 1, :, vsl], bias_inner
        elif dil == 1:
            k_prev, v_prev, bias = kvp_ref[0, :, ksl], kvp_ref[0, :, vsl], bias_edge
        else:
            k_prev, v_prev, bias = kvp_ref[u, :, ksl], kvp_ref[u, :, vsl], bias_edge
        k_cat = jnp.concatenate([k_prev, kv_ref[u, :, ksl]], axis=0)
        v_cat = jnp.concatenate([v_prev, kv_ref[u, :, vsl]], axis=0)
        vexts[u, h] = jnp.concatenate([v_cat, ones], axis=1)
        scores[u, h] = lax.dot_general(q, k_cat, nt, preferred_element_type=F32) + bias
    maxes = {un: jnp.max(scores[un], axis=-1, keepdims=True) for un in units}
    probs = {un: jnp.exp(scores[un] - maxes[un]).astype(BF16) for un in units}
    outs = {un: jnp.dot(probs[un], vexts[un], preferred_element_type=F32) for un in units}
    for u in range(ATT_SUB):
        dst = dst_rows(u)
        stats = jnp.zeros((ATT_BLOCK, LANES), F32)
        for h in range(ATT_KV_HEADS):
            den = outs[u, h][:, hd:]
            onat_scr[h, dst, :] = outs[u, h][:, :hd] / den
            stats = jnp.where(lane == h, maxes[u, h] + jnp.log(den[:, 0:1]), stats)
        st_ref[dst, :] = stats

    @pl.when(g == groups_per_span - 1)
    def _():
        for h in range(ATT_KV_HEADS):
            o_ref[:, h * hd:(h + 1) * hd] = onat_scr[h].astype(o_ref.dtype)


def _attention_group(q_sub, kv_sub, kv_prev, bsz, seq, dil):
    kw = ATT_KV_WIDTH
    span_rows = ATT_SPAN_SUBS * ATT_BLOCK
    gps = ATT_SPAN_SUBS // ATT_SUB
    assert ATT_SPAN_SUBS % dil == 0 and (dil == 1 or dil % ATT_SUB == 0)
    own_map = lambda b, s, g: (b, s * gps + g, 0, 0)
    if dil == 1:
        prev_spec = pl.BlockSpec((None, 1, ATT_BLOCK, 2 * kw),
                                 lambda b, s, g: (b, jnp.maximum((s * gps + g) * ATT_SUB - 1, 0), 0, 0))
    else:
        prev_spec = pl.BlockSpec((None, ATT_SUB, ATT_BLOCK, 2 * kw),
                                 lambda b, s, g: (b, jnp.maximum(s * gps + g - dil // ATT_SUB, 0), 0, 0))
    return pl.pallas_call(
        functools.partial(_attn_kernel, dil=dil),
        grid=(bsz, seq // span_rows, gps),
        in_specs=[
            pl.BlockSpec((None, ATT_SUB, ATT_BLOCK, kw), own_map),
            pl.BlockSpec((None, ATT_SUB, ATT_BLOCK, 2 * kw), own_map),
            prev_spec,
        ],
        out_specs=(
            pl.BlockSpec((None, span_rows, kw), lambda b, s, g: (b, s, 0)),
            pl.BlockSpec((None, span_rows, LANES), lambda b, s, g: (b, s, 0)),
        ),
        out_shape=(jax.ShapeDtypeStruct((bsz, seq, kw), BF16),
                   jax.ShapeDtypeStruct((bsz, seq, LANES), F32)),
        scratch_shapes=[pltpu.VMEM((ATT_KV_HEADS, span_rows, ATT_HEAD_DIM), F32)],
        compiler_params=_params("arbitrary", "arbitrary", "arbitrary"),
        name=f"attn_d{dil}",
    )(q_sub, kv_sub, kv_prev)


def _mpre_kernel(xm_ref, halo_ref, cw_ref, cb_ref, wq_ref, wk_ref, wkt_ref, wv_ref, wif_ref, bif_ref,
                 qm_ref, kt_ref, vm_ref, xc_ref, g_ref, ext_scr):
    i = pl.program_id(1)
    tm = xm_ref.shape[0]
    xm = xm_ref[...]
    halo = halo_ref[...].astype(F32)
    ext_scr[0:CONV_HALO, :] = jnp.where(i > 0, halo, 0.0)
    ext_scr[CONV_HALO:CONV_HALO + tm, :] = xm.astype(F32)
    acc = jnp.zeros((tm, M_WIDTH), F32) + cb_ref[...]
    for k in range(M_CONV):
        start = CONV_HALO - (M_CONV - 1) + k
        acc = acc + cw_ref[k:k + 1, :] * ext_scr[pl.ds(start, tm), :]
    xc = _silu(acc)
    xc_b = xc.astype(BF16)
    xc_ref[...] = xc_b
    gates = jnp.zeros((tm, LANES), F32) + bif_ref[...]
    for h in range(M_HEADS):
        sl = slice(h * M_HEAD_DIM, (h + 1) * M_HEAD_DIM)
        qh = jnp.dot(xc_b[:, sl], wq_ref[h], preferred_element_type=F32).astype(BF16)
        kh = jnp.dot(xc_b[:, sl], wk_ref[h], preferred_element_type=F32).astype(BF16)
        vh = jnp.dot(xm[:, sl], wv_ref[h], preferred_element_type=F32).astype(BF16)
        qm_ref[:, sl] = qh
        vm_ref[:, sl] = vh
        kt_ref[h] = lax.dot_general(wkt_ref[h], xc_b[:, sl], (((1,), (1,)), ((), ())),
                                    preferred_element_type=F32).astype(BF16)
        gates = gates + jnp.dot(qh, wif_ref[0, sl, :], preferred_element_type=F32)
        gates = gates + jnp.dot(kh, wif_ref[1, sl, :], preferred_element_type=F32)
        gates = gates + jnp.dot(vh, wif_ref[2, sl, :], preferred_element_type=F32)
    lane = lax.broadcasted_iota(jnp.int32, gates.shape, 1)
    log_f = jnp.minimum(gates, 0.0) - jnp.log1p(jnp.exp(-jnp.abs(gates)))
    g_ref[...] = jnp.where(lane >= M_HEADS, log_f, gates)


def _mpre(rest3, conv_w, conv_b, wq, wk, wkt, wv, wif3, bif):
    bsz, seq, _ = rest3.shape
    tm = MPRE_TM
    halo_blocks = tm // CONV_HALO
    full = lambda shape: pl.BlockSpec(shape, lambda b, i: (0,) * len(shape))
    big = jax.ShapeDtypeStruct((bsz, seq, M_WIDTH), BF16)
    return pl.pallas_call(
        _mpre_kernel,
        grid=(bsz, seq // tm),
        in_specs=[
            pl.BlockSpec((None, tm, M_WIDTH), lambda b, i: (b, i, 0)),
            pl.BlockSpec((None, CONV_HALO, M_WIDTH),
                         lambda b, i: (b, jnp.maximum(i * halo_blocks - 1, 0), 0)),
            full((M_CONV, M_WIDTH)), full((1, M_WIDTH)),
            full((M_HEADS, M_HEAD_DIM, M_HEAD_DIM)), full((M_HEADS, M_HEAD_DIM, M_HEAD_DIM)),
            full((M_HEADS, M_HEAD_DIM, M_HEAD_DIM)), full((M_HEADS, M_HEAD_DIM, M_HEAD_DIM)),
            full((3, M_WIDTH, LANES)), full((1, LANES)),
        ],
        out_specs=(
            pl.BlockSpec((None, tm, M_WIDTH), lambda b, i: (b, i, 0)),
            pl.BlockSpec((None, M_HEADS, M_HEAD_DIM, tm), lambda b, i: (b, 0, 0, i)),
            pl.BlockSpec((None, tm, M_WIDTH), lambda b, i: (b, i, 0)),
            pl.BlockSpec((None, tm, M_WIDTH), lambda b, i: (b, i, 0)),
            pl.BlockSpec((None, tm, LANES), lambda b, i: (b, i, 0)),
        ),
        out_shape=(big, jax.ShapeDtypeStruct((bsz, M_HEADS, M_HEAD_DIM, seq), BF16), big, big,
                   jax.ShapeDtypeStruct((bsz, seq, LANES), F32)),
        scratch_shapes=[pltpu.VMEM((tm + CONV_HALO, M_WIDTH), F32)],
        compiler_params=_params("arbitrary", "arbitrary"),
        name="mpre",
    )(rest3, rest3, conv_w, conv_b, wq, wk, wkt, wv, wif3, bif)


def _mlstm_kernel(q_ref, kt_ref, v_ref, g_ref, om_ref, zm_ref, xc_ref, nw_ref, sk_ref,
                  y_ref, c_scr, m_scr):
    @pl.when(pl.program_id(1) == 0)
    def _():
        c_scr[...] = jnp.zeros_like(c_scr)
        m_scr[...] = jnp.zeros_like(m_scr)

    L, E = M_CHUNK, M_HEAD_DIM
    row = lax.broadcasted_iota(jnp.int32, (L, L), 0)
    col = lax.broadcasted_iota(jnp.int32, (L, L), 1)
    causal = col <= row
    tri = causal.astype(F32)
    ones_blk = jnp.ones((L, LANES), BF16)

    def chunk(c):
        rows = pl.ds(c * L, L)
        heads = range(M_HEADS)
        hsl = [slice(h * E, (h + 1) * E) for h in heads]
        g = g_ref[rows, :]
        bm = jnp.dot(tri, g, precision=lax.Precision.HIGHEST,
                     preferred_element_type=F32)
        g_t = g.T
        bm_t = bm.T

        b_col = [bm[:, M_HEADS + h:M_HEADS + h + 1] for h in heads]
        li_row = [g_t[h:h + 1, :] for h in heads]
        b_row = [bm_t[M_HEADS + h:M_HEADS + h + 1, :] for h in heads]
        b_last = [b_col[h][L - 1:L, :] for h in heads]
        dmat = [jnp.where(causal, b_col[h] - b_row[h] + li_row[h], NEG) for h in heads]
        m_intra = [jnp.max(dmat[h], axis=-1, keepdims=True) for h in heads]
        e_intra = [jnp.exp(dmat[h] - m_intra[h]) for h in heads]
        w_src = [b_last[h] - b_row[h] + li_row[h] for h in heads]
        a = [jnp.max(w_src[h], axis=1, keepdims=True) for h in heads]
        m_prev = [m_scr[h:h + 1, 0:1] for h in heads]
        m_new = [jnp.maximum(b_last[h] + m_prev[h], a[h]) for h in heads]
        decay = [jnp.exp(b_last[h] + m_prev[h] - m_new[h]) for h in heads]
        e_key = [jnp.exp(w_src[h] - m_new[h]).astype(BF16) for h in heads]
        g_col = [b_col[h] + m_prev[h] for h in heads]
        m_t = [jnp.maximum(g_col[h], m_intra[h]) for h in heads]
        w_intra = [jnp.exp(m_intra[h] - m_t[h]) for h in heads]
        w_state = [jnp.exp(g_col[h] - m_t[h]) for h in heads]
        floor = [jnp.exp(-m_t[h]) for h in heads]

        q = [q_ref[rows, hsl[h]] for h in heads]
        kt = [kt_ref[h, :, c * L:(c + 1) * L] for h in heads]
        v_ext = [jnp.concatenate([v_ref[rows, hsl[h]], ones_blk], axis=1) for h in heads]
        qk = [jnp.dot(q[h], kt[h], preferred_element_type=F32) for h in heads]
        c_prev = [c_scr[h] for h in heads]
        y_state = [jnp.dot(q[h], c_prev[h].astype(BF16), preferred_element_type=F32) for h in heads]
        pmat = [(qk[h] * e_intra[h]).astype(BF16) for h in heads]
        y_intra = [jnp.dot(pmat[h], v_ext[h], preferred_element_type=F32) for h in heads]
        c_loc = [jnp.dot(kt[h] * e_key[h], v_ext[h], preferred_element_type=F32) for h in heads]

        y_all = [w_intra[h] * y_intra[h] + w_state[h] * y_state[h] for h in heads]
        for h in heads:
            c_scr[h] = decay[h] * c_prev[h] + c_loc[h]
            m_scr[h:h + 1, :] = jnp.broadcast_to(m_new[h], (1, LANES))
        inv_den = [1.0 / jnp.maximum(jnp.abs(y_all[h][:, E:]), floor[h]) for h in heads]
        hh = [jnp.concatenate([y_all[h][:, j * LANES:(j + 1) * LANES] * inv_den[h]
                               for j in range(E // LANES)], axis=1) for h in heads]

        hm = [_sigmoid(om_ref[rows, hsl[h]]).astype(F32) * hh[h] for h in heads]
        mu = [jnp.mean(hm[h], axis=-1, keepdims=True) for h in heads]
        hc = [hm[h] - mu[h] for h in heads]
        var = [jnp.mean(hc[h] * hc[h], axis=-1, keepdims=True) for h in heads]
        for h in heads:
            hn = hc[h] * lax.rsqrt(var[h] + LN_EPS) * nw_ref[:, hsl[h]]
            y_ref[rows, hsl[h]] = ((hn + sk_ref[:, hsl[h]] * xc_ref[rows, hsl[h]].astype(F32)).astype(BF16)
                                   * _silu(zm_ref[rows, hsl[h]]))

    for c in range(q_ref.shape[0] // L):
        chunk(c)


def _mlstm(qm, kmt, vm, gates, rest3, xc, norm_w, skip):
    bsz, seq, _ = qm.shape
    tm = MLSTM_TM
    row_spec = pl.BlockSpec((None, tm, M_WIDTH), lambda b, i: (b, i, 0))
    return pl.pallas_call(
        _mlstm_kernel,
        grid=(bsz, seq // tm),
        in_specs=[
            row_spec,
            pl.BlockSpec((None, M_HEADS, M_HEAD_DIM, tm), lambda b, i: (b, 0, 0, i)),
            row_spec,
            pl.BlockSpec((None, tm, LANES), lambda b, i: (b, i, 0)),
            pl.BlockSpec((None, tm, M_WIDTH), lambda b, i: (b, i, 2)),
            pl.BlockSpec((None, tm, M_WIDTH), lambda b, i: (b, i, 1)),
            row_spec,
            pl.BlockSpec((1, M_WIDTH), lambda b, i: (0, 0)),
            pl.BlockSpec((1, M_WIDTH), lambda b, i: (0, 0)),
        ],
        out_specs=row_spec,
        out_shape=jax.ShapeDtypeStruct((bsz, seq, M_WIDTH), BF16),
        scratch_shapes=[pltpu.VMEM((M_HEADS, M_HEAD_DIM, M_HEAD_DIM + LANES), F32),
                        pltpu.VMEM((8, LANES), F32)],
        compiler_params=_params("arbitrary", "arbitrary"),
        name="mlstm",
    )(qm, kmt, vm, gates, rest3, rest3, xc, norm_w, skip)


def _post_kernel(o0_ref, o1_ref, o2_ref, s0_ref, s1_ref, s2_ref, za_ref, ym_ref, ga_ref, gm_ref,
                 x_ref, ada_ref, wpa_ref, wpm_ref, wout_ref, lng_ref, lnb_ref, out_ref,
                 *, alpha, d_model):
    hd = ATT_HEAD_DIM
    tm = x_ref.shape[0]
    gate = ada_ref[:, 2 * d_model:3 * d_model]
    for part in range(POST_PARTS):
        rs = slice(part * (tm // POST_PARTS), (part + 1) * (tm // POST_PARTS))
        parts = []
        for h in range(ATT_KV_HEADS):
            sl = slice(h * hd, (h + 1) * hd)
            l0 = s0_ref[rs, h:h + 1]
            l1 = s1_ref[rs, h:h + 1]
            l2 = s2_ref[rs, h:h + 1]
            mx = jnp.maximum(jnp.maximum(l0, l1), l2)
            e0 = jnp.exp(l0 - mx)
            e1 = jnp.exp(l1 - mx)
            e2 = jnp.exp(l2 - mx)
            tot = e0 + e1 + e2
            o = ((e0 / tot) * o0_ref[rs, sl].astype(F32) + (e1 / tot) * o1_ref[rs, sl].astype(F32)
                 + (e2 / tot) * o2_ref[rs, sl].astype(F32))
            parts.append(o.astype(BF16) * _silu(za_ref[rs, sl]))
        att_in = jnp.concatenate(parts, axis=1)
        y_att = jnp.dot(att_in, wpa_ref[...], preferred_element_type=F32)
        y_m = jnp.dot(ym_ref[rs, :], wpm_ref[...], preferred_element_type=F32)
        merged = (_sigmoid(ga_ref[rs, :]) * y_att.astype(BF16)
                  + _sigmoid(gm_ref[rs, :]) * y_m.astype(BF16))
        out = jnp.dot(merged, wout_ref[...], preferred_element_type=F32)
        res = alpha * x_ref[rs, :] + gate * out
        mu = jnp.mean(res, axis=-1, keepdims=True)
        rc = res - mu
        var = jnp.mean(rc * rc, axis=-1, keepdims=True)
        out_ref[rs, :] = (rc * lax.rsqrt(var + LN_EPS) * lng_ref[...] + lnb_ref[...]).astype(out_ref.dtype)


def _post(o_list, st_list, za2, rest2, ymin2, x2d, ada3, wpa, wpm, wout, ln_g, ln_b, seq, alpha):
    rows, d = x2d.shape
    tm = POST_TM
    tiles_per_seq = seq // tm
    kw = ATT_KV_WIDTH
    o_spec = pl.BlockSpec((tm, kw), lambda i: (i, 0))
    s_spec = pl.BlockSpec((tm, LANES), lambda i: (i, 0))
    const = lambda shape: pl.BlockSpec(shape, lambda i: (0,) * len(shape))
    return pl.pallas_call(
        functools.partial(_post_kernel, alpha=alpha, d_model=d),
        grid=(rows // tm,),
        in_specs=[
            o_spec, o_spec, o_spec, s_spec, s_spec, s_spec,
            pl.BlockSpec((tm, kw), lambda i: (i, 0)),
            pl.BlockSpec((tm, M_WIDTH), lambda i: (i, 0)),
            pl.BlockSpec((tm, d), lambda i: (i, 3)),
            pl.BlockSpec((tm, d), lambda i: (i, 4)),
            pl.BlockSpec((tm, d), lambda i: (i, 0)),
            pl.BlockSpec((None, 1, 3 * d), lambda i: (i // tiles_per_seq, 0, 0)),
            const((kw, d)), const((M_WIDTH, d)), const((d, d)),
            const((1, d)), const((1, d)),
        ],
        out_specs=pl.BlockSpec((tm, d), lambda i: (i, 0)),
        out_shape=jax.ShapeDtypeStruct((rows, d), x2d.dtype),
        compiler_params=_params("arbitrary"),
        name="post",
    )(*o_list, *st_list, za2, ymin2, rest2, rest2, x2d, ada3, wpa, wpm, wout, ln_g, ln_b)


def _layer(x, c, positions, w_ada, b_ada, w_in, conv_w, conv_b, w_qm, w_km, w_vm,
           w_if, b_if, mh_norm_w, skip_m, w_pa, w_pm, w_out, ln_g, ln_b, alpha):
    bsz, seq, d = x.shape
    rows = bsz * seq
    x2d = x.reshape(rows, d)

    ada3 = _ada(c, w_ada, b_ada).reshape(bsz, 1, 3 * d)

    half = ATT_HEAD_DIM // 2
    inv = jnp.power(ROPE_THETA, -jnp.arange(half, dtype=F32) / half)
    sign = jnp.concatenate([-jnp.ones((half,), F32), jnp.ones((half,), F32)])
    rope_tab = jnp.zeros((8, LANES), F32).at[0].set(jnp.concatenate([inv, inv])).at[1].set(sign)

    q0, q1r, q2r, kv, kv4r, kv16r, za, rest = _inproj(x2d, ada3, positions.reshape(rows, 1), rope_tab,
                                                      w_in.astype(BF16), bsz, seq)

    n_sub = seq // ATT_BLOCK
    o_list, st_list = [], []
    for q_g, kv_g, (_, dil) in zip((q0, q1r, q2r), (kv, kv4r, kv16r), ATT_GROUPS):
        kv_sub = kv_g.reshape(bsz, n_sub, ATT_BLOCK, 2 * ATT_KV_WIDTH)
        o_g, st_g = _attention_group(q_g.reshape(bsz, n_sub, ATT_BLOCK, ATT_KV_WIDTH), kv_sub, kv_sub,
                                     bsz, seq, dil)
        o_list.append(o_g.reshape(rows, ATT_KV_WIDTH))
        st_list.append(st_g.reshape(rows, LANES))

    rest3 = rest.reshape(bsz, seq, rest.shape[1])
    n_gate = 2 * M_HEADS
    wif3 = jnp.zeros((3, M_WIDTH, LANES), BF16).at[:, :, :n_gate].set(
        w_if.astype(BF16).reshape(3, M_WIDTH, n_gate))
    bif = jnp.zeros((1, LANES), F32).at[0, :n_gate].set(b_if.astype(F32))
    wk = (w_km * (M_HEAD_DIM ** -0.5)).astype(BF16)
    qm, kmt, vm, xc, gates = _mpre(rest3, conv_w.astype(F32), conv_b.astype(F32).reshape(1, M_WIDTH),
                                   w_qm.astype(BF16), wk, wk.transpose(0, 2, 1), w_vm.astype(BF16), wif3, bif)
    ymin = _mlstm(qm, kmt, vm, gates, rest3, xc,
                  mh_norm_w.astype(F32).reshape(1, M_WIDTH), skip_m.astype(F32).reshape(1, M_WIDTH))

    out = _post(o_list, st_list, za, rest, ymin.reshape(rows, M_WIDTH), x2d, ada3,
                w_pa.astype(BF16), w_pm.astype(BF16), w_out.astype(BF16),
                ln_g.astype(F32).reshape(1, d), ln_b.astype(F32).reshape(1, d), seq, alpha)
    return out.reshape(bsz, seq, d)


def kernel(x, c, positions, w_ada, b_ada, w_in, conv_w, conv_b, w_qm, w_km, w_vm, w_if, b_if,
           mh_norm_w, skip_m, w_pa, w_pm, w_out, ln_g, ln_b):
    depth = w_ada.shape[0]
    alpha = (2.0 * depth) ** 0.25
    for l in range(depth):
        x = _layer(x, c, positions, w_ada[l], b_ada[l], w_in[l], conv_w[l], conv_b[l],
                   w_qm[l], w_km[l], w_vm[l], w_if[l], b_if[l], mh_norm_w[l], skip_m[l],
                   w_pa[l], w_pm[l], w_out[l], ln_g[l], ln_b[l], alpha)
    return x
```

```python
import functools

import jax
import jax.numpy as jnp
from jax import lax
from jax.experimental import pallas as pl
from jax.experimental.pallas import tpu as pltpu

F32 = jnp.float32
BF16 = jnp.bfloat16

ATT_HEAD_DIM = 128
ATT_GROUPS = ((128, 1), (512, 4), (2048, 16))
ATT_KV_HEADS = 4
ATT_KV_WIDTH = ATT_KV_HEADS * ATT_HEAD_DIM
ATT_BLOCK = 128
ROPE_THETA = 10000.0
M_HEADS = 4
M_HEAD_DIM = 256
M_WIDTH = M_HEADS * M_HEAD_DIM
M_CONV = 4
M_CHUNK = 128
LN_EPS = 1e-5
NEG = -1e30

V7X_VMEM_LIMIT_BYTES = 56 * 1024 * 1024
LANES = 128

INPROJ_TM = 1024
INPROJ_TN = 1024
REST_TM = 2048
REST_TN = 1024
ATT_SUB = 4
ATT_SPAN_SUBS = 16
MPRE_TM = 512
MLSTM_TM = 512
POST_TM = 1024
POST_PARTS = 4
CONV_HALO = 16


def _sigmoid(v):
    return 0.5 * jnp.tanh(0.5 * v) + 0.5


def _silu(v):
    return v * _sigmoid(v)


def _params(*sem):
    return pltpu.CompilerParams(dimension_semantics=sem, vmem_limit_bytes=V7X_VMEM_LIMIT_BYTES)


def _ada_kernel(c_ref, w_ref, b_ref, o_ref):
    sc = _silu(c_ref[...])
    o_ref[...] = jnp.dot(sc, w_ref[...], precision=lax.Precision.HIGHEST,
                         preferred_element_type=F32) + b_ref[...]


def _ada(c, w_ada, b_ada):
    bsz, d = c.shape
    n = w_ada.shape[1]
    return pl.pallas_call(
        _ada_kernel,
        grid=(n // d,),
        in_specs=[pl.BlockSpec((bsz, d), lambda j: (0, 0)),
                  pl.BlockSpec((d, d), lambda j: (0, j)),
                  pl.BlockSpec((1, d), lambda j: (0, j))],
        out_specs=pl.BlockSpec((bsz, d), lambda j: (0, j)),
        out_shape=jax.ShapeDtypeStruct((bsz, n), F32),
        compiler_params=_params("arbitrary"),
        name="ada",
    )(c, w_ada, b_ada.reshape(1, n))


def _inproj_kernel(x_ref, ada_ref, pos_ref, tab_ref, w_ref,
                   q0_ref, q1_ref, q2_ref, kv_ref, kv4_ref, kv16_ref, za_ref, h_ref,
                   cos_scr, sin_scr, stage_scr, stage4_scr, *, d_model):
    j = pl.program_id(1)
    tm = x_ref.shape[0]
    hd, kw = ATT_HEAD_DIM, ATT_KV_WIDTH
    dil4, dil16 = ATT_GROUPS[1][1], ATT_GROUPS[2][1]
    outer = dil16 // dil4

    @pl.when(j == 0)
    def _():
        x = x_ref[...]
        mu = jnp.mean(x, axis=-1, keepdims=True)
        xc = x - mu
        var = jnp.mean(xc * xc, axis=-1, keepdims=True)
        xn = xc * lax.rsqrt(var + LN_EPS)
        shift = ada_ref[:, 0:d_model]
        scale = ada_ref[:, d_model:2 * d_model]
        h_ref[...] = (xn * (1.0 + scale) + shift).astype(BF16)
        half = tm // 2
        lo_lane = lax.broadcasted_iota(jnp.int32, (half, LANES), 1) < hd // 2
        pos2 = jnp.where(lo_lane, pos_ref[0:half, :], pos_ref[half:tm, :])
        ang = pos2.astype(F32) * tab_ref[0:1, :]
        for fn, dst in ((jnp.cos, cos_scr), (jnp.sin, sin_scr)):
            val = fn(ang)
            swapped = pltpu.roll(val, hd // 2, axis=1)
            dst[0:half, :] = jnp.where(lo_lane, val, swapped)
            dst[half:tm, :] = jnp.where(lo_lane, swapped, val)
        sin_scr[...] = sin_scr[...] * tab_ref[1:2, :]

    def project():
        return jnp.dot(h_ref[...], w_ref[...], preferred_element_type=F32)

    def heads(acc, rope, scale):
        for hh in range(ATT_KV_HEADS):
            xh = acc[:, hh * hd:(hh + 1) * hd]
            if rope:
                rot = pltpu.roll(xh, hd // 2, axis=1)
                xh = xh * cos_scr[...] + rot * sin_scr[...]
            if scale is not None:
                xh = xh * scale
            yield hh, xh

    def stage_heads(s, acc, rope, scale, nat_ref=None):
        for hh, val in heads(acc, rope, scale):
            stage_scr[s, hh] = val
            if nat_ref is not None:
                nat_ref[:, hh * hd:(hh + 1) * hd] = val.astype(nat_ref.dtype)
        for hh in range(ATT_KV_HEADS):
            for r in range(dil4):
                stage4_scr[s, hh * dil4 + r] = stage_scr[s, hh, pl.ds(r, tm // dil4, stride=dil4), :]

    def emit_d4(s, dst_ref):
        for hh in range(ATT_KV_HEADS):
            for r in range(dil4):
                for nn in range(tm // (ATT_BLOCK * dil4)):
                    rows = stage4_scr[s, hh * dil4 + r, nn * ATT_BLOCK:(nn + 1) * ATT_BLOCK, :]
                    dst_ref[nn, r, :, hh * hd:(hh + 1) * hd] = rows.astype(dst_ref.dtype)

    def emit_d16(s, dst_ref):
        for hh in range(ATT_KV_HEADS):
            for r in range(dil4):
                for a in range(outer):
                    rows = stage4_scr[s, hh * dil4 + r, pl.ds(a, tm // dil16, stride=outer), :]
                    dst_ref[a * dil4 + r, :, hh * hd:(hh + 1) * hd] = rows.astype(dst_ref.dtype)

    q_scale = hd ** -0.5

    @pl.when(j == 0)
    def _():
        acc = project()
        for hh, val in heads(acc[:, :kw], True, q_scale):
            q0_ref[:, hh * hd:(hh + 1) * hd] = val.astype(q0_ref.dtype)
        stage_heads(0, acc[:, kw:], True, q_scale)
        emit_d4(0, q1_ref)

    @pl.when(j == 1)
    def _():
        acc = project()
        stage_heads(0, acc[:, :kw], True, q_scale)
        emit_d16(0, q2_ref)
        stage_heads(1, acc[:, kw:], True, None, kv_ref)
        emit_d4(1, kv4_ref)
        emit_d16(1, kv16_ref)

    @pl.when(j == 2)
    def _():
        acc = project()
        stage_heads(0, acc[:, :kw], False, None, kv_ref)
        emit_d4(0, kv4_ref)
        emit_d16(0, kv16_ref)
        za_ref[...] = acc[:, kw:].astype(za_ref.dtype)


def _inproj(x2d, ada3, pos2d, rope_tab, w_in_bf16, bsz, seq):
    rows, d = x2d.shape
    n_in = w_in_bf16.shape[1]
    tm, tn = INPROJ_TM, INPROJ_TN
    kw = ATT_KV_WIDTH
    n_j = (3 * kw + 3 * kw) // tn
    tps = seq // tm
    dil4, dil16 = ATT_GROUPS[1][1], ATT_GROUPS[2][1]
    blk4, blk16 = ATT_BLOCK * dil4, ATT_BLOCK * dil16
    assert tm % blk4 == 0 and blk16 % tm == 0 and tn == 2 * kw and dil16 % dil4 == 0
    t16 = blk16 // tm

    def kv_col(j):
        return jnp.clip(j - 1, 0, 1)

    out_shapes = (
        jax.ShapeDtypeStruct((rows, kw), BF16),
        jax.ShapeDtypeStruct((bsz, seq // blk4, dil4, ATT_BLOCK, kw), BF16),
        jax.ShapeDtypeStruct((bsz, seq // blk16, dil16, ATT_BLOCK, kw), BF16),
        jax.ShapeDtypeStruct((rows, 2 * kw), BF16),
        jax.ShapeDtypeStruct((bsz, seq // blk4, dil4, ATT_BLOCK, 2 * kw), BF16),
        jax.ShapeDtypeStruct((bsz, seq // blk16, dil16, ATT_BLOCK, 2 * kw), BF16),
        jax.ShapeDtypeStruct((rows, kw), BF16),
        jax.ShapeDtypeStruct((rows, d), BF16),
    )
    r4_block = (None, tm // blk4, dil4, ATT_BLOCK, kw)
    r16_block = (None, None, dil16, tm // dil16, kw)
    r4_map = lambda i, j: (i // tps, i % tps, 0, 0, 0)
    r16_map = lambda i, j: (i // tps, (i % tps) // t16, 0, (i % tps) % t16, 0)
    return pl.pallas_call(
        functools.partial(_inproj_kernel, d_model=d),
        grid=(rows // tm, n_j),
        in_specs=[
            pl.BlockSpec((tm, d), lambda i, j: (i, 0)),
            pl.BlockSpec((None, 1, 3 * d), lambda i, j: (i // tps, 0, 0)),
            pl.BlockSpec((tm, 1), lambda i, j: (i, 0)),
            pl.BlockSpec((8, LANES), lambda i, j: (0, 0)),
            pl.BlockSpec((d, tn), lambda i, j: (0, j)),
        ],
        out_specs=(
            pl.BlockSpec((tm, kw), lambda i, j: (i, 0)),
            pl.BlockSpec(r4_block, r4_map),
            pl.BlockSpec(r16_block, r16_map),
            pl.BlockSpec((tm, kw), lambda i, j: (i, kv_col(j))),
            pl.BlockSpec(r4_block, lambda i, j: r4_map(i, j)[:4] + (kv_col(j),)),
            pl.BlockSpec(r16_block, lambda i, j: r16_map(i, j)[:4] + (kv_col(j),)),
            pl.BlockSpec((tm, kw), lambda i, j: (i, 0)),
            pl.BlockSpec((tm, d), lambda i, j: (i, 0)),
        ),
        out_shape=out_shapes,
        scratch_shapes=[pltpu.VMEM((tm, LANES), F32),
                        pltpu.VMEM((tm, LANES), F32),
                        pltpu.VMEM((2, ATT_KV_HEADS, tm, ATT_HEAD_DIM), F32),
                        pltpu.VMEM((2, ATT_KV_HEADS * dil4, tm // dil4, ATT_HEAD_DIM), F32)],
        compiler_params=_params("arbitrary", "arbitrary"),
        name="inproj",
    )(x2d, ada3, pos2d, rope_tab, w_in_bf16)


def _restproj_kernel(h_ref, w_ref, o_ref):
    o_ref[...] = jnp.dot(h_ref[...], w_ref[...], preferred_element_type=F32).astype(o_ref.dtype)


def _restproj(h2d, w_in_bf16, col0):
    rows, d = h2d.shape
    tm, tn = REST_TM, REST_TN
    n_cols = w_in_bf16.shape[1] - col0
    assert col0 % tn == 0 and n_cols % tn == 0
    return pl.pallas_call(
        _restproj_kernel,
        grid=(rows // tm, n_cols // tn),
        in_specs=[pl.BlockSpec((tm, d), lambda i, j: (i, 0)),
                  pl.BlockSpec((d, tn), lambda i, j: (0, col0 // tn + j))],
        out_specs=pl.BlockSpec((tm, tn), lambda i, j: (i, j)),
        out_shape=jax.ShapeDtypeStruct((rows, n_cols), BF16),
        compiler_params=_params("arbitrary", "arbitrary"),
        name="restproj",
    )(h2d, w_in_bf16)


def _attn_kernel(q_ref, kv_ref, kvp_ref, o_ref, st_ref, onat_scr, *, dil):
    span = pl.program_id(1)
    g = pl.program_id(2)
    hd, kw = ATT_HEAD_DIM, ATT_KV_WIDTH
    groups_per_span = ATT_SPAN_SUBS // ATT_SUB
    if dil == 1:
        first = jnp.logical_and(span == 0, g == 0)
    else:
        first = (span * groups_per_span + g) < dil // ATT_SUB
    thr_first = jnp.where(first, 2 * ATT_BLOCK, 0).astype(jnp.int32)

    row = lax.broadcasted_iota(jnp.int32, (ATT_BLOCK, 2 * ATT_BLOCK), 0)
    col = lax.broadcasted_iota(jnp.int32, (ATT_BLOCK, 2 * ATT_BLOCK), 1)
    in_prev = col < ATT_BLOCK
    rel = col - row

    def mask_bias(thr):
        slack = jnp.where(in_prev, rel - thr, ATT_BLOCK - rel)
        return jnp.where(slack >= 0, 0.0, NEG).astype(F32)

    bias_inner = mask_bias(jnp.int32(0))
    bias_edge = mask_bias(thr_first)
    ones = jnp.ones((2 * ATT_BLOCK, hd), BF16)
    lane = lax.broadcasted_iota(jnp.int32, (ATT_BLOCK, LANES), 1)
    nt = (((1,), (1,)), ((), ()))

    def dst_rows(u):
        if dil == 1:
            return pl.ds(pl.multiple_of((g * ATT_SUB + u) * ATT_BLOCK, ATT_BLOCK), ATT_BLOCK)
        if dil == ATT_SUB:
            return pl.ds(g * (ATT_BLOCK * dil) + u, ATT_BLOCK, stride=dil)
        return pl.ds(g * ATT_SUB + u, ATT_BLOCK, stride=dil)

    units = [(u, h) for u in range(ATT_SUB) for h in range(ATT_KV_HEADS)]
    scores, vexts = {}, {}
    for u, h in units:
        ksl = slice(h * hd, (h + 1) * hd)
        vsl = slice(kw + h * hd, kw + (h + 1) * hd)
        q = q_ref[u, :, ksl]
        if dil == 1 and u > 0:
            k_prev, v_prev, bias = kv_ref[u - 1, :, ksl], kv_ref[u - 1, :, vsl], bias_inner
        elif dil == 1:
            k_prev, v_prev, bias = kvp_ref[0, :, ksl], kvp_ref[0, :, vsl], bias_edge
        else:
            k_prev, v_prev, bias = kvp_ref[u, :, ksl], kvp_ref[u, :, vsl], bias_edge
        k_cat = jnp.concatenate([k_prev, kv_ref[u, :, ksl]], axis=0)
        v_cat = jnp.concatenate([v_prev, kv_ref[u, :, vsl]], axis=0)
        vexts[u, h] = jnp.concatenate([v_cat, ones], axis=1)
        scores[u, h] = lax.dot_general(q, k_cat, nt, preferred_element_type=F32) + bias
    maxes = {un: jnp.max(scores[un], axis=-1, keepdims=True) for un in units}
    probs = {un: jnp.exp(scores[un] - maxes[un]).astype(BF16) for un in units}
    outs = {un: jnp.dot(probs[un], vexts[un], preferred_element_type=F32) for un in units}
    for u in range(ATT_SUB):
        dst = dst_rows(u)
        stats = jnp.zeros((ATT_BLOCK, LANES), F32)
        for h in range(ATT_KV_HEADS):
            den = outs[u, h][:, hd:]
            onat_scr[h, dst, :] = outs[u, h][:, :hd] / den
            stats = jnp.where(lane == h, maxes[u, h] + jnp.log(den[:, 0:1]), stats)
        st_ref[dst, :] = stats

    @pl.when(g == groups_per_span - 1)
    def _():
        for h in range(ATT_KV_HEADS):
            o_ref[:, h * hd:(h + 1) * hd] = onat_scr[h].astype(o_ref.dtype)


def _attention_group(q_sub, kv_sub, kv_prev, bsz, seq, dil):
    kw = ATT_KV_WIDTH
    span_rows = ATT_SPAN_SUBS * ATT_BLOCK
    gps = ATT_SPAN_SUBS // ATT_SUB
    assert ATT_SPAN_SUBS % dil == 0 and (dil == 1 or dil % ATT_SUB == 0)
    own_map = lambda b, s, g: (b, s * gps + g, 0, 0)
    if dil == 1:
        prev_spec = pl.BlockSpec((None, 1, ATT_BLOCK, 2 * kw),
                                 lambda b, s, g: (b, jnp.maximum((s * gps + g) * ATT_SUB - 1, 0), 0, 0))
    else:
        prev_spec = pl.BlockSpec((None, ATT_SUB, ATT_BLOCK, 2 * kw),
                                 lambda b, s, g: (b, jnp.maximum(s * gps + g - dil // ATT_SUB, 0), 0, 0))
    return pl.pallas_call(
        functools.partial(_attn_kernel, dil=dil),
        grid=(bsz, seq // span_rows, gps),
        in_specs=[
            pl.BlockSpec((None, ATT_SUB, ATT_BLOCK, kw), own_map),
            pl.BlockSpec((None, ATT_SUB, ATT_BLOCK, 2 * kw), own_map),
            prev_spec,
        ],
        out_specs=(
            pl.BlockSpec((None, span_rows, kw), lambda b, s, g: (b, s, 0)),
            pl.BlockSpec((None, span_rows, LANES), lambda b, s, g: (b, s, 0)),
        ),
        out_shape=(jax.ShapeDtypeStruct((bsz, seq, kw), BF16),
                   jax.ShapeDtypeStruct((bsz, seq, LANES), F32)),
        scratch_shapes=[pltpu.VMEM((ATT_KV_HEADS, span_rows, ATT_HEAD_DIM), F32)],
        compiler_params=_params("arbitrary", "arbitrary", "arbitrary"),
        name=f"attn_d{dil}",
    )(q_sub, kv_sub, kv_prev)


def _mpre_kernel(xm_ref, halo_ref, cw_ref, cb_ref, wq_ref, wk_ref, wkt_ref, wv_ref, wif_ref, bif_ref,
                 qm_ref, kt_ref, vm_ref, xc_ref, g_ref, ext_scr):
    i = pl.program_id(1)
    tm = xm_ref.shape[0]
    xm = xm_ref[...]
    halo = halo_ref[...].astype(F32)
    ext_scr[0:CONV_HALO, :] = jnp.where(i > 0, halo, 0.0)
    ext_scr[CONV_HALO:CONV_HALO + tm, :] = xm.astype(F32)
    acc = jnp.zeros((tm, M_WIDTH), F32) + cb_ref[...]
    for k in range(M_CONV):
        start = CONV_HALO - (M_CONV - 1) + k
        acc = acc + cw_ref[k:k + 1, :] * ext_scr[pl.ds(start, tm), :]
    xc = _silu(acc)
    xc_b = xc.astype(BF16)
    xc_ref[...] = xc_b
    gates = jnp.zeros((tm, LANES), F32) + bif_ref[...]
    for h in range(M_HEADS):
        sl = slice(h * M_HEAD_DIM, (h + 1) * M_HEAD_DIM)
        qh = jnp.dot(xc_b[:, sl], wq_ref[h], preferred_element_type=F32).astype(BF16)
        kh = jnp.dot(xc_b[:, sl], wk_ref[h], preferred_element_type=F32).astype(BF16)
        vh = jnp.dot(xm[:, sl], wv_ref[h], preferred_element_type=F32).astype(BF16)
        qm_ref[:, sl] = qh
        vm_ref[:, sl] = vh
        kt_ref[h] = lax.dot_general(wkt_ref[h], xc_b[:, sl], (((1,), (1,)), ((), ())),
                                    preferred_element_type=F32).astype(BF16)
        gates = gates + jnp.dot(qh, wif_ref[0, sl, :], preferred_element_type=F32)
        gates = gates + jnp.dot(kh, wif_ref[1, sl, :], preferred_element_type=F32)
        gates = gates + jnp.dot(vh, wif_ref[2, sl, :], preferred_element_type=F32)
    lane = lax.broadcasted_iota(jnp.int32, gates.shape, 1)
    log_f = jnp.minimum(gates, 0.0) - jnp.log1p(jnp.exp(-jnp.abs(gates)))
    g_ref[...] = jnp.where(lane >= M_HEADS, log_f, gates)


def _mpre(rest3, conv_w, conv_b, wq, wk, wkt, wv, wif3, bif):
    bsz, seq, _ = rest3.shape
    tm = MPRE_TM
    halo_blocks = tm // CONV_HALO
    full = lambda shape: pl.BlockSpec(shape, lambda b, i: (0,) * len(shape))
    big = jax.ShapeDtypeStruct((bsz, seq, M_WIDTH), BF16)
    return pl.pallas_call(
        _mpre_kernel,
        grid=(bsz, seq // tm),
        in_specs=[
            pl.BlockSpec((None, tm, M_WIDTH), lambda b, i: (b, i, 0)),
            pl.BlockSpec((None, CONV_HALO, M_WIDTH),
                         lambda b, i: (b, jnp.maximum(i * halo_blocks - 1, 0), 0)),
            full((M_CONV, M_WIDTH)), full((1, M_WIDTH)),
            full((M_HEADS, M_HEAD_DIM, M_HEAD_DIM)), full((M_HEADS, M_HEAD_DIM, M_HEAD_DIM)),
            full((M_HEADS, M_HEAD_DIM, M_HEAD_DIM)), full((M_HEADS, M_HEAD_DIM, M_HEAD_DIM)),
            full((3, M_WIDTH, LANES)), full((1, LANES)),
        ],
        out_specs=(
            pl.BlockSpec((None, tm, M_WIDTH), lambda b, i: (b, i, 0)),
            pl.BlockSpec((None, M_HEADS, M_HEAD_DIM, tm), lambda b, i: (b, 0, 0, i)),
            pl.BlockSpec((None, tm, M_WIDTH), lambda b, i: (b, i, 0)),
            pl.BlockSpec((None, tm, M_WIDTH), lambda b, i: (b, i, 0)),
            pl.BlockSpec((None, tm, LANES), lambda b, i: (b, i, 0)),
        ),
        out_shape=(big, jax.ShapeDtypeStruct((bsz, M_HEADS, M_HEAD_DIM, seq), BF16), big, big,
                   jax.ShapeDtypeStruct((bsz, seq, LANES), F32)),
        scratch_shapes=[pltpu.VMEM((tm + CONV_HALO, M_WIDTH), F32)],
        compiler_params=_params("arbitrary", "arbitrary"),
        name="mpre",
    )(rest3, rest3, conv_w, conv_b, wq, wk, wkt, wv, wif3, bif)


def _mlstm_kernel(q_ref, kt_ref, v_ref, g_ref, om_ref, zm_ref, xc_ref, nw_ref, sk_ref,
                  y_ref, c_scr, m_scr):
    @pl.when(pl.program_id(1) == 0)
    def _():
        c_scr[...] = jnp.zeros_like(c_scr)
        m_scr[...] = jnp.zeros_like(m_scr)

    L, E = M_CHUNK, M_HEAD_DIM
    row = lax.broadcasted_iota(jnp.int32, (L, L), 0)
    col = lax.broadcasted_iota(jnp.int32, (L, L), 1)
    causal = col <= row
    tri = causal.astype(F32)
    ones_blk = jnp.ones((L, LANES), BF16)

    def chunk(c):
        rows = pl.ds(c * L, L)
        heads = range(M_HEADS)
        hsl = [slice(h * E, (h + 1) * E) for h in heads]
        g = g_ref[rows, :]
        bm = jnp.dot(tri, g, precision=lax.Precision.HIGHEST,
                     preferred_element_type=F32)
        g_t = g.T
        bm_t = bm.T

        b_col = [bm[:, M_HEADS + h:M_HEADS + h + 1] for h in heads]
        li_row = [g_t[h:h + 1, :] for h in heads]
        b_row = [bm_t[M_HEADS + h:M_HEADS + h + 1, :] for h in heads]
        b_last = [b_col[h][L - 1:L, :] for h in heads]
        dmat = [jnp.where(causal, b_col[h] - b_row[h] + li_row[h], NEG) for h in heads]
        m_intra = [jnp.max(dmat[h], axis=-1, keepdims=True) for h in heads]
        e_intra = [jnp.exp(dmat[h] - m_intra[h]) for h in heads]
        w_src = [b_last[h] - b_row[h] + li_row[h] for h in heads]
        a = [jnp.max(w_src[h], axis=1, keepdims=True) for h in heads]
        m_prev = [m_scr[h:h + 1, 0:1] for h in heads]
        m_new = [jnp.maximum(b_last[h] + m_prev[h], a[h]) for h in heads]
        decay = [jnp.exp(b_last[h] + m_prev[h] - m_new[h]) for h in heads]
        e_key = [jnp.exp(w_src[h] - m_new[h]).astype(BF16) for h in heads]
        g_col = [b_col[h] + m_prev[h] for h in heads]
        m_t = [jnp.maximum(g_col[h], m_intra[h]) for h in heads]
        w_intra = [jnp.exp(m_intra[h] - m_t[h]) for h in heads]
        w_state = [jnp.exp(g_col[h] - m_t[h]) for h in heads]
        floor = [jnp.exp(-m_t[h]) for h in heads]

        q = [q_ref[rows, hsl[h]] for h in heads]
        kt = [kt_ref[h, :, c * L:(c + 1) * L] for h in heads]
        v_ext = [jnp.concatenate([v_ref[rows, hsl[h]], ones_blk], axis=1) for h in heads]
        qk = [jnp.dot(q[h], kt[h], preferred_element_type=F32) for h in heads]
        c_prev = [c_scr[h] for h in heads]
        y_state = [jnp.dot(q[h], c_prev[h].astype(BF16), preferred_element_type=F32) for h in heads]
        pmat = [(qk[h] * e_intra[h]).astype(BF16) for h in heads]
        y_intra = [jnp.dot(pmat[h], v_ext[h], preferred_element_type=F32) for h in heads]
        c_loc = [jnp.dot(kt[h] * e_key[h], v_ext[h], preferred_element_type=F32) for h in heads]

        y_all = [w_intra[h] * y_intra[h] + w_state[h] * y_state[h] for h in heads]
        for h in heads:
            c_scr[h] = decay[h] * c_prev[h] + c_loc[h]
            m_scr[h:h + 1, :] = jnp.broadcast_to(m_new[h], (1, LANES))
        inv_den = [1.0 / jnp.maximum(jnp.abs(y_all[h][:, E:]), floor[h]) for h in heads]
        hh = [jnp.concatenate([y_all[h][:, j * LANES:(j + 1) * LANES] * inv_den[h]
                               for j in range(E // LANES)], axis=1) for h in heads]

        hm = [_sigmoid(om_ref[rows, hsl[h]]).astype(F32) * hh[h] for h in heads]
        mu = [jnp.mean(hm[h], axis=-1, keepdims=True) for h in heads]
        hc = [hm[h] - mu[h] for h in heads]
        var = [jnp.mean(hc[h] * hc[h], axis=-1, keepdims=True) for h in heads]
        for h in heads:
            hn = hc[h] * lax.rsqrt(var[h] + LN_EPS) * nw_ref[:, hsl[h]]
            y_ref[rows, hsl[h]] = ((hn + sk_ref[:, hsl[h]] * xc_ref[rows, hsl[h]].astype(F32)).astype(BF16)
                                   * _silu(zm_ref[rows, hsl[h]]))

    for c in range(q_ref.shape[0] // L):
        chunk(c)


def _mlstm(qm, kmt, vm, gates, rest3, xc, norm_w, skip):
    bsz, seq, _ = qm.shape
    tm = MLSTM_TM
    row_spec = pl.BlockSpec((None, tm, M_WIDTH), lambda b, i: (b, i, 0))
    return pl.pallas_call(
        _mlstm_kernel,
        grid=(bsz, seq // tm),
        in_specs=[
            row_spec,
            pl.BlockSpec((None, M_HEADS, M_HEAD_DIM, tm), lambda b, i: (b, 0, 0, i)),
            row_spec,
            pl.BlockSpec((None, tm, LANES), lambda b, i: (b, i, 0)),
            pl.BlockSpec((None, tm, M_WIDTH), lambda b, i: (b, i, 2)),
            pl.BlockSpec((None, tm, M_WIDTH), lambda b, i: (b, i, 1)),
            row_spec,
            pl.BlockSpec((1, M_WIDTH), lambda b, i: (0, 0)),
            pl.BlockSpec((1, M_WIDTH), lambda b, i: (0, 0)),
        ],
        out_specs=row_spec,
        out_shape=jax.ShapeDtypeStruct((bsz, seq, M_WIDTH), BF16),
        scratch_shapes=[pltpu.VMEM((M_HEADS, M_HEAD_DIM, M_HEAD_DIM + LANES), F32),
                        pltpu.VMEM((8, LANES), F32)],
        compiler_params=_params("arbitrary", "arbitrary"),
        name="mlstm",
    )(qm, kmt, vm, gates, rest3, rest3, xc, norm_w, skip)


def _post_kernel(o0_ref, o1_ref, o2_ref, s0_ref, s1_ref, s2_ref, za_ref, ym_ref, ga_ref, gm_ref,
                 x_ref, ada_ref, wpa_ref, wpm_ref, wout_ref, lng_ref, lnb_ref, out_ref,
                 *, alpha, d_model):
    hd = ATT_HEAD_DIM
    tm = x_ref.shape[0]
    gate = ada_ref[:, 2 * d_model:3 * d_model]
    for part in range(POST_PARTS):
        rs = slice(part * (tm // POST_PARTS), (part + 1) * (tm // POST_PARTS))
        parts = []
        for h in range(ATT_KV_HEADS):
            sl = slice(h * hd, (h + 1) * hd)
            l0 = s0_ref[rs, h:h + 1]
            l1 = s1_ref[rs, h:h + 1]
            l2 = s2_ref[rs, h:h + 1]
            mx = jnp.maximum(jnp.maximum(l0, l1), l2)
            e0 = jnp.exp(l0 - mx)
            e1 = jnp.exp(l1 - mx)
            e2 = jnp.exp(l2 - mx)
            tot = e0 + e1 + e2
            o = ((e0 / tot) * o0_ref[rs, sl].astype(F32) + (e1 / tot) * o1_ref[rs, sl].astype(F32)
                 + (e2 / tot) * o2_ref[rs, sl].astype(F32))
            parts.append(o.astype(BF16) * _silu(za_ref[rs, sl]))
        att_in = jnp.concatenate(parts, axis=1)
        y_att = jnp.dot(att_in, wpa_ref[...], preferred_element_type=F32)
        y_m = jnp.dot(ym_ref[rs, :], wpm_ref[...], preferred_element_type=F32)
        merged = (_sigmoid(ga_ref[rs, :]) * y_att.astype(BF16)
                  + _sigmoid(gm_ref[rs, :]) * y_m.astype(BF16))
        out = jnp.dot(merged, wout_ref[...], preferred_element_type=F32)
        res = alpha * x_ref[rs, :] + gate * out
        mu = jnp.mean(res, axis=-1, keepdims=True)
        rc = res - mu
        var = jnp.mean(rc * rc, axis=-1, keepdims=True)
        out_ref[rs, :] = (rc * lax.rsqrt(var + LN_EPS) * lng_ref[...] + lnb_ref[...]).astype(out_ref.dtype)


def _post(o_list, st_list, za2, rest2, ymin2, x2d, ada3, wpa, wpm, wout, ln_g, ln_b, seq, alpha):
    rows, d = x2d.shape
    tm = POST_TM
    tiles_per_seq = seq // tm
    kw = ATT_KV_WIDTH
    o_spec = pl.BlockSpec((tm, kw), lambda i: (i, 0))
    s_spec = pl.BlockSpec((tm, LANES), lambda i: (i, 0))
    const = lambda shape: pl.BlockSpec(shape, lambda i: (0,) * len(shape))
    return pl.pallas_call(
        functools.partial(_post_kernel, alpha=alpha, d_model=d),
        grid=(rows // tm,),
        in_specs=[
            o_spec, o_spec, o_spec, s_spec, s_spec, s_spec,
            pl.BlockSpec((tm, kw), lambda i: (i, 0)),
            pl.BlockSpec((tm, M_WIDTH), lambda i: (i, 0)),
            pl.BlockSpec((tm, d), lambda i: (i, 3)),
            pl.BlockSpec((tm, d), lambda i: (i, 4)),
            pl.BlockSpec((tm, d), lambda i: (i, 0)),
            pl.BlockSpec((None, 1, 3 * d), lambda i: (i // tiles_per_seq, 0, 0)),
            const((kw, d)), const((M_WIDTH, d)), const((d, d)),
            const((1, d)), const((1, d)),
        ],
        out_specs=pl.BlockSpec((tm, d), lambda i: (i, 0)),
        out_shape=jax.ShapeDtypeStruct((rows, d), x2d.dtype),
        compiler_params=_params("arbitrary"),
        name="post",
    )(*o_list, *st_list, za2, ymin2, rest2, rest2, x2d, ada3, wpa, wpm, wout, ln_g, ln_b)


def _layer(x, c, positions, w_ada, b_ada, w_in, conv_w, conv_b, w_qm, w_km, w_vm,
           w_if, b_if, mh_norm_w, skip_m, w_pa, w_pm, w_out, ln_g, ln_b, alpha):
    bsz, seq, d = x.shape
    rows = bsz * seq
    x2d = x.reshape(rows, d)

    ada3 = _ada(c, w_ada, b_ada).reshape(bsz, 1, 3 * d)

    half = ATT_HEAD_DIM // 2
    inv = jnp.power(ROPE_THETA, -jnp.arange(half, dtype=F32) / half)
    sign = jnp.concatenate([-jnp.ones((half,), F32), jnp.ones((half,), F32)])
    rope_tab = jnp.zeros((8, LANES), F32).at[0].set(jnp.concatenate([inv, inv])).at[1].set(sign)

    w_in_b = w_in.astype(BF16)
    q0, q1r, q2r, kv, kv4r, kv16r, za, h2d = _inproj(x2d, ada3, positions.reshape(rows, 1), rope_tab,
                                                     w_in_b, bsz, seq)
    rest = _restproj(h2d, w_in_b, 6 * ATT_KV_WIDTH)

    n_sub = seq // ATT_BLOCK
    o_list, st_list = [], []
    for q_g, kv_g, (_, dil) in zip((q0, q1r, q2r), (kv, kv4r, kv16r), ATT_GROUPS):
        kv_sub = kv_g.reshape(bsz, n_sub, ATT_BLOCK, 2 * ATT_KV_WIDTH)
        o_g, st_g = _attention_group(q_g.reshape(bsz, n_sub, ATT_BLOCK, ATT_KV_WIDTH), kv_sub, kv_sub,
                                     bsz, seq, dil)
        o_list.append(o_g.reshape(rows, ATT_KV_WIDTH))
        st_list.append(st_g.reshape(rows, LANES))

    rest3 = rest.reshape(bsz, seq, rest.shape[1])
    n_gate = 2 * M_HEADS
    wif3 = jnp.zeros((3, M_WIDTH, LANES), BF16).at[:, :, :n_gate].set(
        w_if.astype(BF16).reshape(3, M_WIDTH, n_gate))
    bif = jnp.zeros((1, LANES), F32).at[0, :n_gate].set(b_if.astype(F32))
    wk = (w_km * (M_HEAD_DIM ** -0.5)).astype(BF16)
    qm, kmt, vm, xc, gates = _mpre(rest3, conv_w.astype(F32), conv_b.astype(F32).reshape(1, M_WIDTH),
                                   w_qm.astype(BF16), wk, wk.transpose(0, 2, 1), w_vm.astype(BF16), wif3, bif)
    ymin = _mlstm(qm, kmt, vm, gates, rest3, xc,
                  mh_norm_w.astype(F32).reshape(1, M_WIDTH), skip_m.astype(F32).reshape(1, M_WIDTH))

    out = _post(o_list, st_list, za, rest, ymin.reshape(rows, M_WIDTH), x2d, ada3,
                w_pa.astype(BF16), w_pm.astype(BF16), w_out.astype(BF16),
                ln_g.astype(F32).reshape(1, d), ln_b.astype(F32).reshape(1, d), seq, alpha)
    return out.reshape(bsz, seq, d)


def kernel(x, c, positions, w_ada, b_ada, w_in, conv_w, conv_b, w_qm, w_km, w_vm, w_if, b_if,
           mh_norm_w, skip_m, w_pa, w_pm, w_out, ln_g, ln_b):
    depth = w_ada.shape[0]
    alpha = (2.0 * depth) ** 0.25
    for l in range(depth):
        x = _layer(x, c, positions, w_ada[l], b_ada[l], w_in[l], conv_w[l], conv_b[l],
                   w_qm[l], w_km[l], w_vm[l], w_if[l], b_if[l], mh_norm_w[l], skip_m[l],
                   w_pa[l], w_pm[l], w_out[l], ln_g[l], ln_b[l], alpha)
    return x
```

```python
import functools

import jax
import jax.numpy as jnp
from jax import lax
from jax.experimental import pallas as pl
from jax.experimental.pallas import tpu as pltpu

F32 = jnp.float32
BF16 = jnp.bfloat16

ATT_HEAD_DIM = 128
ATT_GROUPS = ((128, 1), (512, 4), (2048, 16))
ATT_KV_HEADS = 4
ATT_KV_WIDTH = ATT_KV_HEADS * ATT_HEAD_DIM
ATT_BLOCK = 128
ROPE_THETA = 10000.0
M_HEADS = 4
M_HEAD_DIM = 256
M_WIDTH = M_HEADS * M_HEAD_DIM
M_CONV = 4
M_CHUNK = 128
LN_EPS = 1e-5
NEG = -1e30

V7X_VMEM_LIMIT_BYTES = 56 * 1024 * 1024
LANES = 128

INPROJ_TM = 1024
INPROJ_TN = 1024
REST_TM = 2048
REST_TN = 1024
ATT_SUB = 4
ATT_SPAN_SUBS = 16
MPRE_TM = 512
MLSTM_TM = 512
POST_TM = 1024
POST_PARTS = 4
CONV_HALO = 16


def _sigmoid(v):
    return 0.5 * jnp.tanh(0.5 * v) + 0.5


def _silu(v):
    return v * _sigmoid(v)


def _params(*sem):
    return pltpu.CompilerParams(dimension_semantics=sem, vmem_limit_bytes=V7X_VMEM_LIMIT_BYTES)


def _ada_kernel(c_ref, w_ref, b_ref, o_ref):
    sc = _silu(c_ref[...])
    o_ref[...] = jnp.dot(sc, w_ref[...], precision=lax.Precision.HIGHEST,
                         preferred_element_type=F32) + b_ref[...]


def _ada(c, w_ada, b_ada):
    bsz, d = c.shape
    n = w_ada.shape[1]
    return pl.pallas_call(
        _ada_kernel,
        grid=(n // d,),
        in_specs=[pl.BlockSpec((bsz, d), lambda j: (0, 0)),
                  pl.BlockSpec((d, d), lambda j: (0, j)),
                  pl.BlockSpec((1, d), lambda j: (0, j))],
        out_specs=pl.BlockSpec((bsz, d), lambda j: (0, j)),
        out_shape=jax.ShapeDtypeStruct((bsz, n), F32),
        compiler_params=_params("arbitrary"),
        name="ada",
    )(c, w_ada, b_ada.reshape(1, n))


def _inproj_kernel(x_ref, ada_ref, pos_ref, tab_ref, w01_ref, w2z_ref, wkv_ref,
                   q0_ref, q1_ref, q2_ref, kv_ref, kv4_ref, kv16_ref, za_ref, h_ref,
                   cos_scr, sin_scr, stage_scr, stage4_scr, *, d_model):
    j = pl.program_id(1)
    tm = x_ref.shape[0]
    hd, kw = ATT_HEAD_DIM, ATT_KV_WIDTH
    dil4, dil16 = ATT_GROUPS[1][1], ATT_GROUPS[2][1]
    outer = dil16 // dil4

    @pl.when(j == 0)
    def _():
        x = x_ref[...]
        mu = jnp.mean(x, axis=-1, keepdims=True)
        xc = x - mu
        var = jnp.mean(xc * xc, axis=-1, keepdims=True)
        xn = xc * lax.rsqrt(var + LN_EPS)
        shift = ada_ref[:, 0:d_model]
        scale = ada_ref[:, d_model:2 * d_model]
        h_ref[...] = (xn * (1.0 + scale) + shift).astype(BF16)
        half = tm // 2
        lo_lane = lax.broadcasted_iota(jnp.int32, (half, LANES), 1) < hd // 2
        pos2 = jnp.where(lo_lane, pos_ref[0:half, :], pos_ref[half:tm, :])
        ang = pos2.astype(F32) * tab_ref[0:1, :]
        for fn, dst in ((jnp.cos, cos_scr), (jnp.sin, sin_scr)):
            val = fn(ang)
            swapped = pltpu.roll(val, hd // 2, axis=1)
            dst[0:half, :] = jnp.where(lo_lane, val, swapped)
            dst[half:tm, :] = jnp.where(lo_lane, swapped, val)
        sin_scr[...] = sin_scr[...] * tab_ref[1:2, :]

    def project(w_ref):
        return jnp.dot(h_ref[...], w_ref[...], preferred_element_type=F32)

    def heads(acc, rope, scale):
        for hh in range(ATT_KV_HEADS):
            xh = acc[:, hh * hd:(hh + 1) * hd]
            if rope:
                rot = pltpu.roll(xh, hd // 2, axis=1)
                xh = xh * cos_scr[...] + rot * sin_scr[...]
            if scale is not None:
                xh = xh * scale
            yield hh, xh

    def stage_heads(s, acc, rope, scale, nat_ref=None, col0=0):
        for hh, val in heads(acc, rope, scale):
            stage_scr[s, hh] = val
            if nat_ref is not None:
                nat_ref[:, col0 + hh * hd:col0 + (hh + 1) * hd] = val.astype(nat_ref.dtype)
        for hh in range(ATT_KV_HEADS):
            for r in range(dil4):
                stage4_scr[s, hh * dil4 + r] = stage_scr[s, hh, pl.ds(r, tm // dil4, stride=dil4), :]

    def emit_d4(s, dst_ref, col0=0):
        for hh in range(ATT_KV_HEADS):
            for r in range(dil4):
                for nn in range(tm // (ATT_BLOCK * dil4)):
                    rows = stage4_scr[s, hh * dil4 + r, nn * ATT_BLOCK:(nn + 1) * ATT_BLOCK, :]
                    dst_ref[nn, r, :, col0 + hh * hd:col0 + (hh + 1) * hd] = rows.astype(dst_ref.dtype)

    def emit_d16(s, dst_ref, col0=0):
        for hh in range(ATT_KV_HEADS):
            for r in range(dil4):
                for a in range(outer):
                    rows = stage4_scr[s, hh * dil4 + r, pl.ds(a, tm // dil16, stride=outer), :]
                    dst_ref[a * dil4 + r, :, col0 + hh * hd:col0 + (hh + 1) * hd] = rows.astype(dst_ref.dtype)

    q_scale = hd ** -0.5

    @pl.when(j == 0)
    def _():
        acc = project(w01_ref)
        for hh, val in heads(acc[:, :kw], True, q_scale):
            q0_ref[:, hh * hd:(hh + 1) * hd] = val.astype(q0_ref.dtype)
        stage_heads(0, acc[:, kw:], True, q_scale)
        emit_d4(0, q1_ref)

    @pl.when(j == 1)
    def _():
        acc = project(w2z_ref)
        stage_heads(0, acc[:, :kw], True, q_scale)
        emit_d16(0, q2_ref)
        za_ref[...] = acc[:, kw:].astype(za_ref.dtype)

    @pl.when(j == 2)
    def _():
        acc = project(wkv_ref)
        stage_heads(0, acc[:, :kw], True, None, kv_ref, 0)
        emit_d4(0, kv4_ref, 0)
        emit_d16(0, kv16_ref, 0)
        stage_heads(1, acc[:, kw:], False, None, kv_ref, kw)
        emit_d4(1, kv4_ref, kw)
        emit_d16(1, kv16_ref, kw)


def _inproj(x2d, ada3, pos2d, rope_tab, w_in_bf16, bsz, seq):
    rows, d = x2d.shape
    tm, tn = INPROJ_TM, INPROJ_TN
    kw = ATT_KV_WIDTH
    n_j = 3
    w01 = w_in_bf16[:, 0:2 * kw]
    w2z = jnp.concatenate([w_in_bf16[:, 2 * kw:3 * kw], w_in_bf16[:, 5 * kw:6 * kw]], axis=1)
    wkv = w_in_bf16[:, 3 * kw:5 * kw]
    tps = seq // tm
    dil4, dil16 = ATT_GROUPS[1][1], ATT_GROUPS[2][1]
    blk4, blk16 = ATT_BLOCK * dil4, ATT_BLOCK * dil16
    assert tm % blk4 == 0 and blk16 % tm == 0 and tn == 2 * kw and dil16 % dil4 == 0
    t16 = blk16 // tm

    out_shapes = (
        jax.ShapeDtypeStruct((rows, kw), BF16),
        jax.ShapeDtypeStruct((bsz, seq // blk4, dil4, ATT_BLOCK, kw), BF16),
        jax.ShapeDtypeStruct((bsz, seq // blk16, dil16, ATT_BLOCK, kw), BF16),
        jax.ShapeDtypeStruct((rows, 2 * kw), BF16),
        jax.ShapeDtypeStruct((bsz, seq // blk4, dil4, ATT_BLOCK, 2 * kw), BF16),
        jax.ShapeDtypeStruct((bsz, seq // blk16, dil16, ATT_BLOCK, 2 * kw), BF16),
        jax.ShapeDtypeStruct((rows, kw), BF16),
        jax.ShapeDtypeStruct((rows, d), BF16),
    )
    r4_map = lambda i, j: (i // tps, i % tps, 0, 0, 0)
    r16_map = lambda i, j: (i // tps, (i % tps) // t16, 0, (i % tps) % t16, 0)
    r4_spec = lambda w: pl.BlockSpec((None, tm // blk4, dil4, ATT_BLOCK, w), r4_map)
    r16_spec = lambda w: pl.BlockSpec((None, None, dil16, tm // dil16, w), r16_map)
    w_spec = pl.BlockSpec((d, tn), lambda i, j: (0, 0))
    return pl.pallas_call(
        functools.partial(_inproj_kernel, d_model=d),
        grid=(rows // tm, n_j),
        in_specs=[
            pl.BlockSpec((tm, d), lambda i, j: (i, 0)),
            pl.BlockSpec((None, 1, 3 * d), lambda i, j: (i // tps, 0, 0)),
            pl.BlockSpec((tm, 1), lambda i, j: (i, 0)),
            pl.BlockSpec((8, LANES), lambda i, j: (0, 0)),
            w_spec, w_spec, w_spec,
        ],
        out_specs=(
            pl.BlockSpec((tm, kw), lambda i, j: (i, 0)),
            r4_spec(kw),
            r16_spec(kw),
            pl.BlockSpec((tm, 2 * kw), lambda i, j: (i, 0)),
            r4_spec(2 * kw),
            r16_spec(2 * kw),
            pl.BlockSpec((tm, kw), lambda i, j: (i, 0)),
            pl.BlockSpec((tm, d), lambda i, j: (i, 0)),
        ),
        out_shape=out_shapes,
        scratch_shapes=[pltpu.VMEM((tm, LANES), F32),
                        pltpu.VMEM((tm, LANES), F32),
                        pltpu.VMEM((2, ATT_KV_HEADS, tm, ATT_HEAD_DIM), F32),
                        pltpu.VMEM((2, ATT_KV_HEADS * dil4, tm // dil4, ATT_HEAD_DIM), F32)],
        compiler_params=_params("arbitrary", "arbitrary"),
        name="inproj",
    )(x2d, ada3, pos2d, rope_tab, w01, w2z, wkv)


def _restproj_kernel(h_ref, w_ref, o_ref):
    o_ref[...] = jnp.dot(h_ref[...], w_ref[...], preferred_element_type=F32).astype(o_ref.dtype)


def _restproj(h2d, w_in_bf16, col0):
    rows, d = h2d.shape
    tm, tn = REST_TM, REST_TN
    n_cols = w_in_bf16.shape[1] - col0
    assert col0 % tn == 0 and n_cols % tn == 0
    return pl.pallas_call(
        _restproj_kernel,
        grid=(rows // tm, n_cols // tn),
        in_specs=[pl.BlockSpec((tm, d), lambda i, j: (i, 0)),
                  pl.BlockSpec((d, tn), lambda i, j: (0, col0 // tn + j))],
        out_specs=pl.BlockSpec((tm, tn), lambda i, j: (i, j)),
        out_shape=jax.ShapeDtypeStruct((rows, n_cols), BF16),
        compiler_params=_params("arbitrary", "arbitrary"),
        name="restproj",
    )(h2d, w_in_bf16)


def _attn_kernel(q_ref, kv_ref, kvp_ref, o_ref, st_ref, onat_scr, *, dil):
    span = pl.program_id(1)
    g = pl.program_id(2)
    hd, kw = ATT_HEAD_DIM, ATT_KV_WIDTH
    groups_per_span = ATT_SPAN_SUBS // ATT_SUB
    if dil == 1:
        first = jnp.logical_and(span == 0, g == 0)
    else:
        first = (span * groups_per_span + g) < dil // ATT_SUB
    thr_first = jnp.where(first, 2 * ATT_BLOCK, 0).astype(jnp.int32)

    row = lax.broadcasted_iota(jnp.int32, (ATT_BLOCK, 2 * ATT_BLOCK), 0)
    col = lax.broadcasted_iota(jnp.int32, (ATT_BLOCK, 2 * ATT_BLOCK), 1)
    in_prev = col < ATT_BLOCK
    rel = col - row

    def mask_bias(thr):
        slack = jnp.where(in_prev, rel - thr, ATT_BLOCK - rel)
        return jnp.where(slack >= 0, 0.0, NEG).astype(F32)

    bias_inner = mask_bias(jnp.int32(0))
    bias_edge = mask_bias(thr_first)
    ones = jnp.ones((2 * ATT_BLOCK, hd), BF16)
    lane = lax.broadcasted_iota(jnp.int32, (ATT_BLOCK, LANES), 1)
    nt = (((1,), (1,)), ((), ()))

    def dst_rows(u):
        if dil == 1:
            return pl.ds(pl.multiple_of((g * ATT_SUB + u) * ATT_BLOCK, ATT_BLOCK), ATT_BLOCK)
        if dil == ATT_SUB:
            return pl.ds(g * (ATT_BLOCK * dil) + u, ATT_BLOCK, stride=dil)
        return pl.ds(g * ATT_SUB + u, ATT_BLOCK, stride=dil)

    units = [(u, h) for u in range(ATT_SUB) for h in range(ATT_KV_HEADS)]
    scores, vexts = {}, {}
    for u, h in units:
        ksl = slice(h * hd, (h + 1) * hd)
        vsl = slice(kw + h * hd, kw + (h + 1) * hd)
        q = q_ref[u, :, ksl]
        if dil == 1 and u > 0:
            k_prev, v_prev, bias = kv_ref[u - 1, :, ksl], kv_ref[u - 1, :, vsl], bias_inner
        elif dil == 1:
            k_prev, v_prev, bias = kvp_ref[0, :, ksl], kvp_ref[0, :, vsl], bias_edge
        else:
            k_prev, v_prev, bias = kvp_ref[u, :, ksl], kvp_ref[u, :, vsl], bias_edge
        k_cat = jnp.concatenate([k_prev, kv_ref[u, :, ksl]], axis=0)
        v_cat = jnp.concatenate([v_prev, kv_ref[u, :, vsl]], axis=0)
        vexts[u, h] = jnp.concatenate([v_cat, ones], axis=1)
        scores[u, h] = lax.dot_general(q, k_cat, nt, preferred_element_type=F32) + bias
    maxes = {un: jnp.max(scores[un], axis=-1, keepdims=True) for un in units}
    probs = {un: jnp.exp(scores[un] - maxes[un]).astype(BF16) for un in units}
    outs = {un: jnp.dot(probs[un], vexts[un], preferred_element_type=F32) for un in units}
    for u in range(ATT_SUB):
        dst = dst_rows(u)
        stats = jnp.zeros((ATT_BLOCK, LANES), F32)
        for h in range(ATT_KV_HEADS):
            den = outs[u, h][:, hd:]
            onat_scr[h, dst, :] = outs[u, h][:, :hd] / den
            stats = jnp.where(lane == h, maxes[u, h] + jnp.log(den[:, 0:1]), stats)
        st_ref[dst, :] = stats

    @pl.when(g == groups_per_span - 1)
    def _():
        for h in range(ATT_KV_HEADS):
            o_ref[:, h * hd:(h + 1) * hd] = onat_scr[h].astype(o_ref.dtype)


def _attention_group(q_sub, kv_sub, kv_prev, bsz, seq, dil):
    kw = ATT_KV_WIDTH
    span_rows = ATT_SPAN_SUBS * ATT_BLOCK
    gps = ATT_SPAN_SUBS // ATT_SUB
    assert ATT_SPAN_SUBS % dil == 0 and (dil == 1 or dil % ATT_SUB == 0)
    own_map = lambda b, s, g: (b, s * gps + g, 0, 0)
    if dil == 1:
        prev_spec = pl.BlockSpec((None, 1, ATT_BLOCK, 2 * kw),
                                 lambda b, s, g: (b, jnp.maximum((s * gps + g) * ATT_SUB - 1, 0), 0, 0))
    else:
        prev_spec = pl.BlockSpec((None, ATT_SUB, ATT_BLOCK, 2 * kw),
                                 lambda b, s, g: (b, jnp.maximum(s * gps + g - dil // ATT_SUB, 0), 0, 0))
    return pl.pallas_call(
        functools.partial(_attn_kernel, dil=dil),
        grid=(bsz, seq // span_rows, gps),
        in_specs=[
            pl.BlockSpec((None, ATT_SUB, ATT_BLOCK, kw), own_map),
            pl.BlockSpec((None, ATT_SUB, ATT_BLOCK, 2 * kw), own_map),
            prev_spec,
        ],
        out_specs=(
            pl.BlockSpec((None, span_rows, kw), lambda b, s, g: (b, s, 0)),
            pl.BlockSpec((None, span_rows, LANES), lambda b, s, g: (b, s, 0)),
        ),
        out_shape=(jax.ShapeDtypeStruct((bsz, seq, kw), BF16),
                   jax.ShapeDtypeStruct((bsz, seq, LANES), F32)),
        scratch_shapes=[pltpu.VMEM((ATT_KV_HEADS, span_rows, ATT_HEAD_DIM), F32)],
        compiler_params=_params("arbitrary", "arbitrary", "arbitrary"),
        name=f"attn_d{dil}",
    )(q_sub, kv_sub, kv_prev)


def _mpre_kernel(xm_ref, halo_ref, cw_ref, cb_ref, wq_ref, wk_ref, wkt_ref, wv_ref, wif_ref, bif_ref,
                 qm_ref, kt_ref, vm_ref, xc_ref, g_ref, ext_scr):
    i = pl.program_id(1)
    tm = xm_ref.shape[0]
    xm = xm_ref[...]
    halo = halo_ref[...].astype(F32)
    ext_scr[0:CONV_HALO, :] = jnp.where(i > 0, halo, 0.0)
    ext_scr[CONV_HALO:CONV_HALO + tm, :] = xm.astype(F32)
    acc = jnp.zeros((tm, M_WIDTH), F32) + cb_ref[...]
    for k in range(M_CONV):
        start = CONV_HALO - (M_CONV - 1) + k
        acc = acc + cw_ref[k:k + 1, :] * ext_scr[pl.ds(start, tm), :]
    xc = _silu(acc)
    xc_b = xc.astype(BF16)
    xc_ref[...] = xc_b
    gates = jnp.zeros((tm, LANES), F32) + bif_ref[...]
    for h in range(M_HEADS):
        sl = slice(h * M_HEAD_DIM, (h + 1) * M_HEAD_DIM)
        qh = jnp.dot(xc_b[:, sl], wq_ref[h], preferred_element_type=F32).astype(BF16)
        kh = jnp.dot(xc_b[:, sl], wk_ref[h], preferred_element_type=F32).astype(BF16)
        vh = jnp.dot(xm[:, sl], wv_ref[h], preferred_element_type=F32).astype(BF16)
        qm_ref[:, sl] = qh
        vm_ref[:, sl] = vh
        kt_ref[h] = lax.dot_general(wkt_ref[h], xc_b[:, sl], (((1,), (1,)), ((), ())),
                                    preferred_element_type=F32).astype(BF16)
        gates = gates + jnp.dot(qh, wif_ref[0, sl, :], preferred_element_type=F32)
        gates = gates + jnp.dot(kh, wif_ref[1, sl, :], preferred_element_type=F32)
        gates = gates + jnp.dot(vh, wif_ref[2, sl, :], preferred_element_type=F32)
    lane = lax.broadcasted_iota(jnp.int32, gates.shape, 1)
    log_f = jnp.minimum(gates, 0.0) - jnp.log1p(jnp.exp(-jnp.abs(gates)))
    g_ref[...] = jnp.where(lane >= M_HEADS, log_f, gates)


def _mpre(rest3, conv_w, conv_b, wq, wk, wkt, wv, wif3, bif):
    bsz, seq, _ = rest3.shape
    tm = MPRE_TM
    halo_blocks = tm // CONV_HALO
    full = lambda shape: pl.BlockSpec(shape, lambda b, i: (0,) * len(shape))
    big = jax.ShapeDtypeStruct((bsz, seq, M_WIDTH), BF16)
    return pl.pallas_call(
        _mpre_kernel,
        grid=(bsz, seq // tm),
        in_specs=[
            pl.BlockSpec((None, tm, M_WIDTH), lambda b, i: (b, i, 0)),
            pl.BlockSpec((None, CONV_HALO, M_WIDTH),
                         lambda b, i: (b, jnp.maximum(i * halo_blocks - 1, 0), 0)),
            full((M_CONV, M_WIDTH)), full((1, M_WIDTH)),
            full((M_HEADS, M_HEAD_DIM, M_HEAD_DIM)), full((M_HEADS, M_HEAD_DIM, M_HEAD_DIM)),
            full((M_HEADS, M_HEAD_DIM, M_HEAD_DIM)), full((M_HEADS, M_HEAD_DIM, M_HEAD_DIM)),
            full((3, M_WIDTH, LANES)), full((1, LANES)),
        ],
        out_specs=(
            pl.BlockSpec((None, tm, M_WIDTH), lambda b, i: (b, i, 0)),
            pl.BlockSpec((None, M_HEADS, M_HEAD_DIM, tm), lambda b, i: (b, 0, 0, i)),
            pl.BlockSpec((None, tm, M_WIDTH), lambda b, i: (b, i, 0)),
            pl.BlockSpec((None, tm, M_WIDTH), lambda b, i: (b, i, 0)),
            pl.BlockSpec((None, tm, LANES), lambda b, i: (b, i, 0)),
        ),
        out_shape=(big, jax.ShapeDtypeStruct((bsz, M_HEADS, M_HEAD_DIM, seq), BF16), big, big,
                   jax.ShapeDtypeStruct((bsz, seq, LANES), F32)),
        scratch_shapes=[pltpu.VMEM((tm + CONV_HALO, M_WIDTH), F32)],
        compiler_params=_params("arbitrary", "arbitrary"),
        name="mpre",
    )(rest3, rest3, conv_w, conv_b, wq, wk, wkt, wv, wif3, bif)


def _mlstm_kernel(q_ref, kt_ref, v_ref, g_ref, om_ref, zm_ref, xc_ref, nw_ref, sk_ref,
                  y_ref, c_scr, m_scr):
    @pl.when(pl.program_id(1) == 0)
    def _():
        c_scr[...] = jnp.zeros_like(c_scr)
        m_scr[...] = jnp.zeros_like(m_scr)

    L, E = M_CHUNK, M_HEAD_DIM
    row = lax.broadcasted_iota(jnp.int32, (L, L), 0)
    col = lax.broadcasted_iota(jnp.int32, (L, L), 1)
    causal = col <= row
    tri = causal.astype(F32)
    ones_blk = jnp.ones((L, LANES), BF16)

    def chunk(c):
        rows = pl.ds(c * L, L)
        heads = range(M_HEADS)
        hsl = [slice(h * E, (h + 1) * E) for h in heads]
        g = g_ref[rows, :]
        bm = jnp.dot(tri, g, precision=lax.Precision.HIGHEST,
                     preferred_element_type=F32)
        g_t = g.T
        bm_t = bm.T

        b_col = [bm[:, M_HEADS + h:M_HEADS + h + 1] for h in heads]
        li_row = [g_t[h:h + 1, :] for h in heads]
        b_row = [bm_t[M_HEADS + h:M_HEADS + h + 1, :] for h in heads]
        b_last = [b_col[h][L - 1:L, :] for h in heads]
        dmat = [jnp.where(causal, b_col[h] - b_row[h] + li_row[h], NEG) for h in heads]
        m_intra = [jnp.max(dmat[h], axis=-1, keepdims=True) for h in heads]
        e_intra = [jnp.exp(dmat[h] - m_intra[h]) for h in heads]
        w_src = [b_last[h] - b_row[h] + li_row[h] for h in heads]
        a = [jnp.max(w_src[h], axis=1, keepdims=True) for h in heads]
        m_prev = [m_scr[h:h + 1, 0:1] for h in heads]
        m_new = [jnp.maximum(b_last[h] + m_prev[h], a[h]) for h in heads]
        decay = [jnp.exp(b_last[h] + m_prev[h] - m_new[h]) for h in heads]
        e_key = [jnp.exp(w_src[h] - m_new[h]).astype(BF16) for h in heads]
        g_col = [b_col[h] + m_prev[h] for h in heads]
        m_t = [jnp.maximum(g_col[h], m_intra[h]) for h in heads]
        w_intra = [jnp.exp(m_intra[h] - m_t[h]) for h in heads]
        w_state = [jnp.exp(g_col[h] - m_t[h]) for h in heads]
        floor = [jnp.exp(-m_t[h]) for h in heads]

        q = [q_ref[rows, hsl[h]] for h in heads]
        kt = [kt_ref[h, :, c * L:(c + 1) * L] for h in heads]
        v_ext = [jnp.concatenate([v_ref[rows, hsl[h]], ones_blk], axis=1) for h in heads]
        qk = [jnp.dot(q[h], kt[h], preferred_element_type=F32) for h in heads]
        c_prev = [c_scr[h] for h in heads]
        y_state = [jnp.dot(q[h], c_prev[h].astype(BF16), preferred_element_type=F32) for h in heads]
        pmat = [(qk[h] * e_intra[h]).astype(BF16) for h in heads]
        y_intra = [jnp.dot(pmat[h], v_ext[h], preferred_element_type=F32) for h in heads]
        c_loc = [jnp.dot(kt[h] * e_key[h], v_ext[h], preferred_element_type=F32) for h in heads]

        y_all = [w_intra[h] * y_intra[h] + w_state[h] * y_state[h] for h in heads]
        for h in heads:
            c_scr[h] = decay[h] * c_prev[h] + c_loc[h]
            m_scr[h:h + 1, :] = jnp.broadcast_to(m_new[h], (1, LANES))
        inv_den = [1.0 / jnp.maximum(jnp.abs(y_all[h][:, E:]), floor[h]) for h in heads]
        hh = [jnp.concatenate([y_all[h][:, j * LANES:(j + 1) * LANES] * inv_den[h]
                               for j in range(E // LANES)], axis=1) for h in heads]

        hm = [_sigmoid(om_ref[rows, hsl[h]]).astype(F32) * hh[h] for h in heads]
        mu = [jnp.mean(hm[h], axis=-1, keepdims=True) for h in heads]
        hc = [hm[h] - mu[h] for h in heads]
        var = [jnp.mean(hc[h] * hc[h], axis=-1, keepdims=True) for h in heads]
        for h in heads:
            hn = hc[h] * lax.rsqrt(var[h] + LN_EPS) * nw_ref[:, hsl[h]]
            y_ref[rows, hsl[h]] = ((hn + sk_ref[:, hsl[h]] * xc_ref[rows, hsl[h]].astype(F32)).astype(BF16)
                                   * _silu(zm_ref[rows, hsl[h]]))

    for c in range(q_ref.shape[0] // L):
        chunk(c)


def _mlstm(qm, kmt, vm, gates, rest3, xc, norm_w, skip):
    bsz, seq, _ = qm.shape
    tm = MLSTM_TM
    row_spec = pl.BlockSpec((None, tm, M_WIDTH), lambda b, i: (b, i, 0))
    return pl.pallas_call(
        _mlstm_kernel,
        grid=(bsz, seq // tm),
        in_specs=[
            row_spec,
            pl.BlockSpec((None, M_HEADS, M_HEAD_DIM, tm), lambda b, i: (b, 0, 0, i)),
            row_spec,
            pl.BlockSpec((None, tm, LANES), lambda b, i: (b, i, 0)),
            pl.BlockSpec((None, tm, M_WIDTH), lambda b, i: (b, i, 2)),
            pl.BlockSpec((None, tm, M_WIDTH), lambda b, i: (b, i, 1)),
            row_spec,
            pl.BlockSpec((1, M_WIDTH), lambda b, i: (0, 0)),
            pl.BlockSpec((1, M_WIDTH), lambda b, i: (0, 0)),
        ],
        out_specs=row_spec,
        out_shape=jax.ShapeDtypeStruct((bsz, seq, M_WIDTH), BF16),
        scratch_shapes=[pltpu.VMEM((M_HEADS, M_HEAD_DIM, M_HEAD_DIM + LANES), F32),
                        pltpu.VMEM((8, LANES), F32)],
        compiler_params=_params("arbitrary", "arbitrary"),
        name="mlstm",
    )(qm, kmt, vm, gates, rest3, rest3, xc, norm_w, skip)


def _post_kernel(o0_ref, o1_ref, o2_ref, s0_ref, s1_ref, s2_ref, za_ref, ym_ref, ga_ref, gm_ref,
                 x_ref, ada_ref, wpa_ref, wpm_ref, wout_ref, lng_ref, lnb_ref, out_ref,
                 *, alpha, d_model):
    hd = ATT_HEAD_DIM
    tm = x_ref.shape[0]
    gate = ada_ref[:, 2 * d_model:3 * d_model]
    for part in range(POST_PARTS):
        rs = slice(part * (tm // POST_PARTS), (part + 1) * (tm // POST_PARTS))
        parts = []
        for h in range(ATT_KV_HEADS):
            sl = slice(h * hd, (h + 1) * hd)
            l0 = s0_ref[rs, h:h + 1]
            l1 = s1_ref[rs, h:h + 1]
            l2 = s2_ref[rs, h:h + 1]
            mx = jnp.maximum(jnp.maximum(l0, l1), l2)
            e0 = jnp.exp(l0 - mx)
            e1 = jnp.exp(l1 - mx)
            e2 = jnp.exp(l2 - mx)
            tot = e0 + e1 + e2
            o = ((e0 / tot) * o0_ref[rs, sl].astype(F32) + (e1 / tot) * o1_ref[rs, sl].astype(F32)
                 + (e2 / tot) * o2_ref[rs, sl].astype(F32))
            parts.append(o.astype(BF16) * _silu(za_ref[rs, sl]))
        att_in = jnp.concatenate(parts, axis=1)
        y_att = jnp.dot(att_in, wpa_ref[...], preferred_element_type=F32)
        y_m = jnp.dot(ym_ref[rs, :], wpm_ref[...], preferred_element_type=F32)
        merged = (_sigmoid(ga_ref[rs, :]) * y_att.astype(BF16)
                  + _sigmoid(gm_ref[rs, :]) * y_m.astype(BF16))
        out = jnp.dot(merged, wout_ref[...], preferred_element_type=F32)
        res = alpha * x_ref[rs, :] + gate * out
        mu = jnp.mean(res, axis=-1, keepdims=True)
        rc = res - mu
        var = jnp.mean(rc * rc, axis=-1, keepdims=True)
        out_ref[rs, :] = (rc * lax.rsqrt(var + LN_EPS) * lng_ref[...] + lnb_ref[...]).astype(out_ref.dtype)


def _post(o_list, st_list, za2, rest2, ymin2, x2d, ada3, wpa, wpm, wout, ln_g, ln_b, seq, alpha):
    rows, d = x2d.shape
    tm = POST_TM
    tiles_per_seq = seq // tm
    kw = ATT_KV_WIDTH
    o_spec = pl.BlockSpec((tm, kw), lambda i: (i, 0))
    s_spec = pl.BlockSpec((tm, LANES), lambda i: (i, 0))
    const = lambda shape: pl.BlockSpec(shape, lambda i: (0,) * len(shape))
    return pl.pallas_call(
        functools.partial(_post_kernel, alpha=alpha, d_model=d),
        grid=(rows // tm,),
        in_specs=[
            o_spec, o_spec, o_spec, s_spec, s_spec, s_spec,
            pl.BlockSpec((tm, kw), lambda i: (i, 0)),
            pl.BlockSpec((tm, M_WIDTH), lambda i: (i, 0)),
            pl.BlockSpec((tm, d), lambda i: (i, 3)),
            pl.BlockSpec((tm, d), lambda i: (i, 4)),
            pl.BlockSpec((tm, d), lambda i: (i, 0)),
            pl.BlockSpec((None, 1, 3 * d), lambda i: (i // tiles_per_seq, 0, 0)),
            const((kw, d)), const((M_WIDTH, d)), const((d, d)),
            const((1, d)), const((1, d)),
        ],
        out_specs=pl.BlockSpec((tm, d), lambda i: (i, 0)),
        out_shape=jax.ShapeDtypeStruct((rows, d), x2d.dtype),
        compiler_params=_params("arbitrary"),
        name="post",
    )(*o_list, *st_list, za2, ymin2, rest2, rest2, x2d, ada3, wpa, wpm, wout, ln_g, ln_b)


def _layer(x, c, positions, w_ada, b_ada, w_in, conv_w, conv_b, w_qm, w_km, w_vm,
           w_if, b_if, mh_norm_w, skip_m, w_pa, w_pm, w_out, ln_g, ln_b, alpha):
    bsz, seq, d = x.shape
    rows = bsz * seq
    x2d = x.reshape(rows, d)

    ada3 = _ada(c, w_ada, b_ada).reshape(bsz, 1, 3 * d)

    half = ATT_HEAD_DIM // 2
    inv = jnp.power(ROPE_THETA, -jnp.arange(half, dtype=F32) / half)
    sign = jnp.concatenate([-jnp.ones((half,), F32), jnp.ones((half,), F32)])
    rope_tab = jnp.zeros((8, LANES), F32).at[0].set(jnp.concatenate([inv, inv])).at[1].set(sign)

    w_in_b = w_in.astype(BF16)
    q0, q1r, q2r, kv, kv4r, kv16r, za, h2d = _inproj(x2d, ada3, positions.reshape(rows, 1), rope_tab,
                                                     w_in_b, bsz, seq)
    rest = _restproj(h2d, w_in_b, 6 * ATT_KV_WIDTH)

    n_sub = seq // ATT_BLOCK
    o_list, st_list = [], []
    for q_g, kv_g, (_, dil) in zip((q0, q1r, q2r), (kv, kv4r, kv16r), ATT_GROUPS):
        kv_sub = kv_g.reshape(bsz, n_sub, ATT_BLOCK, 2 * ATT_KV_WIDTH)
        o_g, st_g = _attention_group(q_g.reshape(bsz, n_sub, ATT_BLOCK, ATT_KV_WIDTH), kv_sub, kv_sub,
                                     bsz, seq, dil)
        o_list.append(o_g.reshape(rows, ATT_KV_WIDTH))
        st_list.append(st_g.reshape(rows, LANES))

    rest3 = rest.reshape(bsz, seq, rest.shape[1])
    n_gate = 2 * M_HEADS
    wif3 = jnp.zeros((3, M_WIDTH, LANES), BF16).at[:, :, :n_gate].set(
        w_if.astype(BF16).reshape(3, M_WIDTH, n_gate))
    bif = jnp.zeros((1, LANES), F32).at[0, :n_gate].set(b_if.astype(F32))
    wk = (w_km * (M_HEAD_DIM ** -0.5)).astype(BF16)
    qm, kmt, vm, xc, gates = _mpre(rest3, conv_w.astype(F32), conv_b.astype(F32).reshape(1, M_WIDTH),
                                   w_qm.astype(BF16), wk, wk.transpose(0, 2, 1), w_vm.astype(BF16), wif3, bif)
    ymin = _mlstm(qm, kmt, vm, gates, rest3, xc,
                  mh_norm_w.astype(F32).reshape(1, M_WIDTH), skip_m.astype(F32).reshape(1, M_WIDTH))

    out = _post(o_list, st_list, za, rest, ymin.reshape(rows, M_WIDTH), x2d, ada3,
                w_pa.astype(BF16), w_pm.astype(BF16), w_out.astype(BF16),
                ln_g.astype(F32).reshape(1, d), ln_b.astype(F32).reshape(1, d), seq, alpha)
    return out.reshape(bsz, seq, d)


def kernel(x, c, positions, w_ada, b_ada, w_in, conv_w, conv_b, w_qm, w_km, w_vm, w_if, b_if,
           mh_norm_w, skip_m, w_pa, w_pm, w_out, ln_g, ln_b):
    depth = w_ada.shape[0]
    alpha = (2.0 * depth) ** 0.25
    for l in range(depth):
        x = _layer(x, c, positions, w_ada[l], b_ada[l], w_in[l], conv_w[l], conv_b[l],
                   w_qm[l], w_km[l], w_vm[l], w_if[l], b_if[l], mh_norm_w[l], skip_m[l],
                   w_pa[l], w_pm[l], w_out[l], ln_g[l], ln_b[l], alpha)
    return x
```

```python
import functools

import jax
import jax.numpy as jnp
from jax import lax
from jax.experimental import pallas as pl
from jax.experimental.pallas import tpu as pltpu

F32 = jnp.float32
BF16 = jnp.bfloat16

ATT_HEAD_DIM = 128
ATT_GROUPS = ((128, 1), (512, 4), (2048, 16))
ATT_KV_HEADS = 4
ATT_KV_WIDTH = ATT_KV_HEADS * ATT_HEAD_DIM
ATT_BLOCK = 128
ROPE_THETA = 10000.0
M_HEADS = 4
M_HEAD_DIM = 256
M_WIDTH = M_HEADS * M_HEAD_DIM
M_CONV = 4
M_CHUNK = 128
LN_EPS = 1e-5
NEG = -1e30

V7X_VMEM_LIMIT_BYTES = 56 * 1024 * 1024
LANES = 128

INPROJ_TM = 1024
INPROJ_TN = 1024
INPROJ_PARTS = 4
REST_TM = 2048
REST_TN = 1024
ATT_SUB = 4
ATT_SPAN_SUBS = 16
MPRE_TM = 512
MLSTM_TM = 512
POST_TM = 1024
POST_PARTS = 4
CONV_HALO = 16


def _sigmoid(v):
    return 0.5 * jnp.tanh(0.5 * v) + 0.5


def _silu(v):
    return v * _sigmoid(v)


def _params(*sem):
    return pltpu.CompilerParams(dimension_semantics=sem, vmem_limit_bytes=V7X_VMEM_LIMIT_BYTES)


def _ada_kernel(c_ref, w_ref, b_ref, o_ref):
    sc = _silu(c_ref[...])
    o_ref[...] = jnp.dot(sc, w_ref[...], precision=lax.Precision.HIGHEST,
                         preferred_element_type=F32) + b_ref[...]


def _ada(c, w_ada, b_ada):
    bsz, d = c.shape
    n = w_ada.shape[1]
    return pl.pallas_call(
        _ada_kernel,
        grid=(n // d,),
        in_specs=[pl.BlockSpec((bsz, d), lambda j: (0, 0)),
                  pl.BlockSpec((d, d), lambda j: (0, j)),
                  pl.BlockSpec((1, d), lambda j: (0, j))],
        out_specs=pl.BlockSpec((bsz, d), lambda j: (0, j)),
        out_shape=jax.ShapeDtypeStruct((bsz, n), F32),
        compiler_params=_params("arbitrary"),
        name="ada",
    )(c, w_ada, b_ada.reshape(1, n))


def _inproj_kernel(x_ref, ada_ref, pos_ref, tab_ref, w01_ref, w2z_ref, wkv_ref,
                   q0_ref, q1_ref, q2_ref, kv_ref, kv4_ref, kv16_ref, za_ref, h_ref,
                   cos_scr, sin_scr, stage_scr, stage4_scr, *, d_model):
    j = pl.program_id(1)
    tm = x_ref.shape[0]
    hd, kw = ATT_HEAD_DIM, ATT_KV_WIDTH
    dil4, dil16 = ATT_GROUPS[1][1], ATT_GROUPS[2][1]
    outer = dil16 // dil4
    pr = tm // INPROJ_PARTS
    p4, p16 = pr // dil4, pr // dil16
    q_scale = hd ** -0.5

    def prepare(p):
        rs = slice(p * pr, (p + 1) * pr)
        x = x_ref[rs, :]
        mu = jnp.mean(x, axis=-1, keepdims=True)
        xc = x - mu
        var = jnp.mean(xc * xc, axis=-1, keepdims=True)
        xn = xc * lax.rsqrt(var + LN_EPS)
        shift = ada_ref[:, 0:d_model]
        scale = ada_ref[:, d_model:2 * d_model]
        h_ref[rs, :] = (xn * (1.0 + scale) + shift).astype(BF16)
        half = pr // 2
        lo, hi = slice(p * pr, p * pr + half), slice(p * pr + half, (p + 1) * pr)
        lo_lane = lax.broadcasted_iota(jnp.int32, (half, LANES), 1) < hd // 2
        ang = jnp.where(lo_lane, pos_ref[lo, :], pos_ref[hi, :]).astype(F32) * tab_ref[0:1, :]
        sign = tab_ref[1:2, :]
        for val, dst, sgn in ((jnp.cos(ang), cos_scr, None), (jnp.sin(ang), sin_scr, sign)):
            swapped = pltpu.roll(val, hd // 2, axis=1)
            first, second = jnp.where(lo_lane, val, swapped), jnp.where(lo_lane, swapped, val)
            dst[lo, :] = first if sgn is None else first * sgn
            dst[hi, :] = second if sgn is None else second * sgn

    def project(p, w_ref):
        return jnp.dot(h_ref[p * pr:(p + 1) * pr, :], w_ref[...], preferred_element_type=F32)

    def heads(p, acc, rope, scale):
        rs = slice(p * pr, (p + 1) * pr)
        for hh in range(ATT_KV_HEADS):
            xh = acc[:, hh * hd:(hh + 1) * hd]
            if rope:
                rot = pltpu.roll(xh, hd // 2, axis=1)
                xh = xh * cos_scr[rs, :] + rot * sin_scr[rs, :]
            if scale is not None:
                xh = xh * scale
            yield hh, xh

    def stage_heads(p, s, acc, rope, scale, nat_ref=None, col0=0):
        rs = slice(p * pr, (p + 1) * pr)
        for hh, val in heads(p, acc, rope, scale):
            stage_scr[s, hh, rs, :] = val
            if nat_ref is not None:
                nat_ref[rs, col0 + hh * hd:col0 + (hh + 1) * hd] = val.astype(nat_ref.dtype)
        for hh in range(ATT_KV_HEADS):
            for r in range(dil4):
                stage4_scr[s, hh * dil4 + r, p * p4:(p + 1) * p4, :] = (
                    stage_scr[s, hh, pl.ds(p * pr + r, p4, stride=dil4), :])

    def emit_d4(p, s, dst_ref, col0=0):
        nn, off = divmod(p * p4, ATT_BLOCK)
        for hh in range(ATT_KV_HEADS):
            for r in range(dil4):
                rows = stage4_scr[s, hh * dil4 + r, p * p4:(p + 1) * p4, :]
                dst_ref[nn, r, off:off + p4, col0 + hh * hd:col0 + (hh + 1) * hd] = rows.astype(dst_ref.dtype)

    def emit_d16(p, s, dst_ref, col0=0):
        for hh in range(ATT_KV_HEADS):
            for r in range(dil4):
                for a in range(outer):
                    rows = stage4_scr[s, hh * dil4 + r, pl.ds(p * p4 + a, p16, stride=outer), :]
                    dst_ref[a * dil4 + r, p * p16:(p + 1) * p16,
                            col0 + hh * hd:col0 + (hh + 1) * hd] = rows.astype(dst_ref.dtype)

    def tile_q01(p, acc):
        for hh, val in heads(p, acc[:, :kw], True, q_scale):
            q0_ref[p * pr:(p + 1) * pr, hh * hd:(hh + 1) * hd] = val.astype(q0_ref.dtype)
        stage_heads(p, 0, acc[:, kw:], True, q_scale)
        emit_d4(p, 0, q1_ref)

    def tile_q2z(p, acc):
        stage_heads(p, 0, acc[:, :kw], True, q_scale)
        emit_d16(p, 0, q2_ref)
        za_ref[p * pr:(p + 1) * pr, :] = acc[:, kw:].astype(za_ref.dtype)

    def tile_kv(p, acc):
        stage_heads(p, 0, acc[:, :kw], True, None, kv_ref, 0)
        emit_d4(p, 0, kv4_ref, 0)
        emit_d16(p, 0, kv16_ref, 0)
        stage_heads(p, 1, acc[:, kw:], False, None, kv_ref, kw)
        emit_d4(p, 1, kv4_ref, kw)
        emit_d16(p, 1, kv16_ref, kw)

    def run(w_ref, epilogue, prologue=None):
        accs = {}
        for p in range(INPROJ_PARTS + 1):
            if p < INPROJ_PARTS:
                if prologue is not None:
                    prologue(p)
                accs[p] = project(p, w_ref)
            if p > 0:
                epilogue(p - 1, accs.pop(p - 1))

    @pl.when(j == 0)
    def _():
        run(w01_ref, tile_q01, prepare)

    @pl.when(j == 1)
    def _():
        run(w2z_ref, tile_q2z)

    @pl.when(j == 2)
    def _():
        run(wkv_ref, tile_kv)


def _inproj(x2d, ada3, pos2d, rope_tab, w_in_bf16, bsz, seq):
    rows, d = x2d.shape
    tm, tn = INPROJ_TM, INPROJ_TN
    kw = ATT_KV_WIDTH
    n_j = 3
    w01 = w_in_bf16[:, 0:2 * kw]
    w2z = jnp.concatenate([w_in_bf16[:, 2 * kw:3 * kw], w_in_bf16[:, 5 * kw:6 * kw]], axis=1)
    wkv = w_in_bf16[:, 3 * kw:5 * kw]
    tps = seq // tm
    dil4, dil16 = ATT_GROUPS[1][1], ATT_GROUPS[2][1]
    blk4, blk16 = ATT_BLOCK * dil4, ATT_BLOCK * dil16
    assert tm % blk4 == 0 and blk16 % tm == 0 and tn == 2 * kw and dil16 % dil4 == 0
    t16 = blk16 // tm

    out_shapes = (
        jax.ShapeDtypeStruct((rows, kw), BF16),
        jax.ShapeDtypeStruct((bsz, seq // blk4, dil4, ATT_BLOCK, kw), BF16),
        jax.ShapeDtypeStruct((bsz, seq // blk16, dil16, ATT_BLOCK, kw), BF16),
        jax.ShapeDtypeStruct((rows, 2 * kw), BF16),
        jax.ShapeDtypeStruct((bsz, seq // blk4, dil4, ATT_BLOCK, 2 * kw), BF16),
        jax.ShapeDtypeStruct((bsz, seq // blk16, dil16, ATT_BLOCK, 2 * kw), BF16),
        jax.ShapeDtypeStruct((rows, kw), BF16),
        jax.ShapeDtypeStruct((rows, d), BF16),
    )
    r4_map = lambda i, j: (i // tps, i % tps, 0, 0, 0)
    r16_map = lambda i, j: (i // tps, (i % tps) // t16, 0, (i % tps) % t16, 0)
    r4_spec = lambda w: pl.BlockSpec((None, tm // blk4, dil4, ATT_BLOCK, w), r4_map)
    r16_spec = lambda w: pl.BlockSpec((None, None, dil16, tm // dil16, w), r16_map)
    w_spec = pl.BlockSpec((d, tn), lambda i, j: (0, 0))
    return pl.pallas_call(
        functools.partial(_inproj_kernel, d_model=d),
        grid=(rows // tm, n_j),
        in_specs=[
            pl.BlockSpec((tm, d), lambda i, j: (i, 0)),
            pl.BlockSpec((None, 1, 3 * d), lambda i, j: (i // tps, 0, 0)),
            pl.BlockSpec((tm, 1), lambda i, j: (i, 0)),
            pl.BlockSpec((8, LANES), lambda i, j: (0, 0)),
            w_spec, w_spec, w_spec,
        ],
        out_specs=(
            pl.BlockSpec((tm, kw), lambda i, j: (i, 0)),
            r4_spec(kw),
            r16_spec(kw),
            pl.BlockSpec((tm, 2 * kw), lambda i, j: (i, 0)),
            r4_spec(2 * kw),
            r16_spec(2 * kw),
            pl.BlockSpec((tm, kw), lambda i, j: (i, 0)),
            pl.BlockSpec((tm, d), lambda i, j: (i, 0)),
        ),
        out_shape=out_shapes,
        scratch_shapes=[pltpu.VMEM((tm, LANES), F32),
                        pltpu.VMEM((tm, LANES), F32),
                        pltpu.VMEM((2, ATT_KV_HEADS, tm, ATT_HEAD_DIM), F32),
                        pltpu.VMEM((2, ATT_KV_HEADS * dil4, tm // dil4, ATT_HEAD_DIM), F32)],
        compiler_params=_params("arbitrary", "arbitrary"),
        name="inproj",
    )(x2d, ada3, pos2d, rope_tab, w01, w2z, wkv)


def _restproj_kernel(h_ref, w_ref, o_ref):
    o_ref[...] = jnp.dot(h_ref[...], w_ref[...], preferred_element_type=F32).astype(o_ref.dtype)


def _restproj(h2d, w_in_bf16, col0):
    rows, d = h2d.shape
    tm, tn = REST_TM, REST_TN
    n_cols = w_in_bf16.shape[1] - col0
    assert col0 % tn == 0 and n_cols % tn == 0
    return pl.pallas_call(
        _restproj_kernel,
        grid=(rows // tm, n_cols // tn),
        in_specs=[pl.BlockSpec((tm, d), lambda i, j: (i, 0)),
                  pl.BlockSpec((d, tn), lambda i, j: (0, col0 // tn + j))],
        out_specs=pl.BlockSpec((tm, tn), lambda i, j: (i, j)),
        out_shape=jax.ShapeDtypeStruct((rows, n_cols), BF16),
        compiler_params=_params("arbitrary", "arbitrary"),
        name="restproj",
    )(h2d, w_in_bf16)


def _attn_kernel(q_ref, kv_ref, kvp_ref, o_ref, st_ref, onat_scr, *, dil):
    span = pl.program_id(1)
    g = pl.program_id(2)
    hd, kw = ATT_HEAD_DIM, ATT_KV_WIDTH
    groups_per_span = ATT_SPAN_SUBS // ATT_SUB
    if dil == 1:
        first = jnp.logical_and(span == 0, g == 0)
    else:
        first = (span * groups_per_span + g) < dil // ATT_SUB
    thr_first = jnp.where(first, 2 * ATT_BLOCK, 0).astype(jnp.int32)

    row = lax.broadcasted_iota(jnp.int32, (ATT_BLOCK, 2 * ATT_BLOCK), 0)
    col = lax.broadcasted_iota(jnp.int32, (ATT_BLOCK, 2 * ATT_BLOCK), 1)
    in_prev = col < ATT_BLOCK
    rel = col - row

    def mask_bias(thr):
        slack = jnp.where(in_prev, rel - thr, ATT_BLOCK - rel)
        return jnp.where(slack >= 0, 0.0, NEG).astype(F32)

    bias_inner = mask_bias(jnp.int32(0))
    bias_edge = mask_bias(thr_first)
    ones = jnp.ones((2 * ATT_BLOCK, hd), BF16)
    lane = lax.broadcasted_iota(jnp.int32, (ATT_BLOCK, LANES), 1)
    nt = (((1,), (1,)), ((), ()))

    def dst_rows(u):
        if dil == 1:
            return pl.ds(pl.multiple_of((g * ATT_SUB + u) * ATT_BLOCK, ATT_BLOCK), ATT_BLOCK)
        if dil == ATT_SUB:
            return pl.ds(g * (ATT_BLOCK * dil) + u, ATT_BLOCK, stride=dil)
        return pl.ds(g * ATT_SUB + u, ATT_BLOCK, stride=dil)

    units = [(u, h) for u in range(ATT_SUB) for h in range(ATT_KV_HEADS)]
    scores, vexts = {}, {}
    for u, h in units:
        ksl = slice(h * hd, (h + 1) * hd)
        vsl = slice(kw + h * hd, kw + (h + 1) * hd)
        q = q_ref[u, :, ksl]
        if dil == 1 and u > 0:
            k_prev, v_prev, bias = kv_ref[u - 1, :, ksl], kv_ref[u - 1, :, vsl], bias_inner
        elif dil == 1:
            k_prev, v_prev, bias = kvp_ref[0, :, ksl], kvp_ref[0, :, vsl], bias_edge
        else:
            k_prev, v_prev, bias = kvp_ref[u, :, ksl], kvp_ref[u, :, vsl], bias_edge
        k_cat = jnp.concatenate([k_prev, kv_ref[u, :, ksl]], axis=0)
        v_cat = jnp.concatenate([v_prev, kv_ref[u, :, vsl]], axis=0)
        vexts[u, h] = jnp.concatenate([v_cat, ones], axis=1)
        scores[u, h] = lax.dot_general(q, k_cat, nt, preferred_element_type=F32) + bias
    maxes = {un: jnp.max(scores[un], axis=-1, keepdims=True) for un in units}
    probs = {un: jnp.exp(scores[un] - maxes[un]).astype(BF16) for un in units}
    outs = {un: jnp.dot(probs[un], vexts[un], preferred_element_type=F32) for un in units}
    for u in range(ATT_SUB):
        dst = dst_rows(u)
        stats = jnp.zeros((ATT_BLOCK, LANES), F32)
        for h in range(ATT_KV_HEADS):
            den = outs[u, h][:, hd:]
            onat_scr[h, dst, :] = outs[u, h][:, :hd] / den
            stats = jnp.where(lane == h, maxes[u, h] + jnp.log(den[:, 0:1]), stats)
        st_ref[dst, :] = stats

    @pl.when(g == groups_per_span - 1)
    def _():
        for h in range(ATT_KV_HEADS):
            o_ref[:, h * hd:(h + 1) * hd] = onat_scr[h].astype(o_ref.dtype)


def _attention_group(q_sub, kv_sub, kv_prev, bsz, seq, dil):
    kw = ATT_KV_WIDTH
    span_rows = ATT_SPAN_SUBS * ATT_BLOCK
    gps = ATT_SPAN_SUBS // ATT_SUB
    assert ATT_SPAN_SUBS % dil == 0 and (dil == 1 or dil % ATT_SUB == 0)
    own_map = lambda b, s, g: (b, s * gps + g, 0, 0)
    if dil == 1:
        prev_spec = pl.BlockSpec((None, 1, ATT_BLOCK, 2 * kw),
                                 lambda b, s, g: (b, jnp.maximum((s * gps + g) * ATT_SUB - 1, 0), 0, 0))
    else:
        prev_spec = pl.BlockSpec((None, ATT_SUB, ATT_BLOCK, 2 * kw),
                                 lambda b, s, g: (b, jnp.maximum(s * gps + g - dil // ATT_SUB, 0), 0, 0))
    return pl.pallas_call(
        functools.partial(_attn_kernel, dil=dil),
        grid=(bsz, seq // span_rows, gps),
        in_specs=[
            pl.BlockSpec((None, ATT_SUB, ATT_BLOCK, kw), own_map),
            pl.BlockSpec((None, ATT_SUB, ATT_BLOCK, 2 * kw), own_map),
            prev_spec,
        ],
        out_specs=(
            pl.BlockSpec((None, span_rows, kw), lambda b, s, g: (b, s, 0)),
            pl.BlockSpec((None, span_rows, LANES), lambda b, s, g: (b, s, 0)),
        ),
        out_shape=(jax.ShapeDtypeStruct((bsz, seq, kw), BF16),
                   jax.ShapeDtypeStruct((bsz, seq, LANES), F32)),
        scratch_shapes=[pltpu.VMEM((ATT_KV_HEADS, span_rows, ATT_HEAD_DIM), F32)],
        compiler_params=_params("arbitrary", "arbitrary", "arbitrary"),
        name=f"attn_d{dil}",
    )(q_sub, kv_sub, kv_prev)


def _mpre_kernel(xm_ref, halo_ref, cw_ref, cb_ref, wq_ref, wk_ref, wkt_ref, wv_ref, wif_ref, bif_ref,
                 qm_ref, kt_ref, vm_ref, xc_ref, g_ref, ext_scr):
    i = pl.program_id(1)
    tm = xm_ref.shape[0]
    xm = xm_ref[...]
    halo = halo_ref[...].astype(F32)
    ext_scr[0:CONV_HALO, :] = jnp.where(i > 0, halo, 0.0)
    ext_scr[CONV_HALO:CONV_HALO + tm, :] = xm.astype(F32)
    acc = jnp.zeros((tm, M_WIDTH), F32) + cb_ref[...]
    for k in range(M_CONV):
        start = CONV_HALO - (M_CONV - 1) + k
        acc = acc + cw_ref[k:k + 1, :] * ext_scr[pl.ds(start, tm), :]
    xc = _silu(acc)
    xc_b = xc.astype(BF16)
    xc_ref[...] = xc_b
    gates = jnp.zeros((tm, LANES), F32) + bif_ref[...]
    for h in range(M_HEADS):
        sl = slice(h * M_HEAD_DIM, (h + 1) * M_HEAD_DIM)
        qh = jnp.dot(xc_b[:, sl], wq_ref[h], preferred_element_type=F32).astype(BF16)
        kh = jnp.dot(xc_b[:, sl], wk_ref[h], preferred_element_type=F32).astype(BF16)
        vh = jnp.dot(xm[:, sl], wv_ref[h], preferred_element_type=F32).astype(BF16)
        qm_ref[:, sl] = qh
        vm_ref[:, sl] = vh
        kt_ref[h] = lax.dot_general(wkt_ref[h], xc_b[:, sl], (((1,), (1,)), ((), ())),
                                    preferred_element_type=F32).astype(BF16)
        gates = gates + jnp.dot(qh, wif_ref[0, sl, :], preferred_element_type=F32)
        gates = gates + jnp.dot(kh, wif_ref[1, sl, :], preferred_element_type=F32)
        gates = gates + jnp.dot(vh, wif_ref[2, sl, :], preferred_element_type=F32)
    lane = lax.broadcasted_iota(jnp.int32, gates.shape, 1)
    log_f = jnp.minimum(gates, 0.0) - jnp.log1p(jnp.exp(-jnp.abs(gates)))
    g_ref[...] = jnp.where(lane >= M_HEADS, log_f, gates)


def _mpre(rest3, conv_w, conv_b, wq, wk, wkt, wv, wif3, bif):
    bsz, seq, _ = rest3.shape
    tm = MPRE_TM
    halo_blocks = tm // CONV_HALO
    full = lambda shape: pl.BlockSpec(shape, lambda b, i: (0,) * len(shape))
    big = jax.ShapeDtypeStruct((bsz, seq, M_WIDTH), BF16)
    return pl.pallas_call(
        _mpre_kernel,
        grid=(bsz, seq // tm),
        in_specs=[
            pl.BlockSpec((None, tm, M_WIDTH), lambda b, i: (b, i, 0)),
            pl.BlockSpec((None, CONV_HALO, M_WIDTH),
                         lambda b, i: (b, jnp.maximum(i * halo_blocks - 1, 0), 0)),
            full((M_CONV, M_WIDTH)), full((1, M_WIDTH)),
            full((M_HEADS, M_HEAD_DIM, M_HEAD_DIM)), full((M_HEADS, M_HEAD_DIM, M_HEAD_DIM)),
            full((M_HEADS, M_HEAD_DIM, M_HEAD_DIM)), full((M_HEADS, M_HEAD_DIM, M_HEAD_DIM)),
            full((3, M_WIDTH, LANES)), full((1, LANES)),
        ],
        out_specs=(
            pl.BlockSpec((None, tm, M_WIDTH), lambda b, i: (b, i, 0)),
            pl.BlockSpec((None, M_HEADS, M_HEAD_DIM, tm), lambda b, i: (b, 0, 0, i)),
            pl.BlockSpec((None, tm, M_WIDTH), lambda b, i: (b, i, 0)),
            pl.BlockSpec((None, tm, M_WIDTH), lambda b, i: (b, i, 0)),
            pl.BlockSpec((None, tm, LANES), lambda b, i: (b, i, 0)),
        ),
        out_shape=(big, jax.ShapeDtypeStruct((bsz, M_HEADS, M_HEAD_DIM, seq), BF16), big, big,
                   jax.ShapeDtypeStruct((bsz, seq, LANES), F32)),
        scratch_shapes=[pltpu.VMEM((tm + CONV_HALO, M_WIDTH), F32)],
        compiler_params=_params("arbitrary", "arbitrary"),
        name="mpre",
    )(rest3, rest3, conv_w, conv_b, wq, wk, wkt, wv, wif3, bif)


def _mlstm_kernel(q_ref, kt_ref, v_ref, g_ref, om_ref, zm_ref, xc_ref, nw_ref, sk_ref,
                  y_ref, c_scr, m_scr):
    @pl.when(pl.program_id(1) == 0)
    def _():
        c_scr[...] = jnp.zeros_like(c_scr)
        m_scr[...] = jnp.zeros_like(m_scr)

    L, E = M_CHUNK, M_HEAD_DIM
    row = lax.broadcasted_iota(jnp.int32, (L, L), 0)
    col = lax.broadcasted_iota(jnp.int32, (L, L), 1)
    causal = col <= row
    tri = causal.astype(F32)
    ones_blk = jnp.ones((L, LANES), BF16)

    def chunk(c):
        rows = pl.ds(c * L, L)
        heads = range(M_HEADS)
        hsl = [slice(h * E, (h + 1) * E) for h in heads]
        g = g_ref[rows, :]
        bm = jnp.dot(tri, g, precision=lax.Precision.HIGHEST,
                     preferred_element_type=F32)
        g_t = g.T
        bm_t = bm.T

        b_col = [bm[:, M_HEADS + h:M_HEADS + h + 1] for h in heads]
        li_row = [g_t[h:h + 1, :] for h in heads]
        b_row = [bm_t[M_HEADS + h:M_HEADS + h + 1, :] for h in heads]
        b_last = [b_col[h][L - 1:L, :] for h in heads]
        dmat = [jnp.where(causal, b_col[h] - b_row[h] + li_row[h], NEG) for h in heads]
        m_intra = [jnp.max(dmat[h], axis=-1, keepdims=True) for h in heads]
        e_intra = [jnp.exp(dmat[h] - m_intra[h]) for h in heads]
        w_src = [b_last[h] - b_row[h] + li_row[h] for h in heads]
        a = [jnp.max(w_src[h], axis=1, keepdims=True) for h in heads]
        m_prev = [m_scr[h:h + 1, 0:1] for h in heads]
        m_new = [jnp.maximum(b_last[h] + m_prev[h], a[h]) for h in heads]
        decay = [jnp.exp(b_last[h] + m_prev[h] - m_new[h]) for h in heads]
        e_key = [jnp.exp(w_src[h] - m_new[h]).astype(BF16) for h in heads]
        g_col = [b_col[h] + m_prev[h] for h in heads]
        m_t = [jnp.maximum(g_col[h], m_intra[h]) for h in heads]
        w_intra = [jnp.exp(m_intra[h] - m_t[h]) for h in heads]
        w_state = [jnp.exp(g_col[h] - m_t[h]) for h in heads]
        floor = [jnp.exp(-m_t[h]) for h in heads]

        q = [q_ref[rows, hsl[h]] for h in heads]
        kt = [kt_ref[h, :, c * L:(c + 1) * L] for h in heads]
        v_ext = [jnp.concatenate([v_ref[rows, hsl[h]], ones_blk], axis=1) for h in heads]
        qk = [jnp.dot(q[h], kt[h], preferred_element_type=F32) for h in heads]
        c_prev = [c_scr[h] for h in heads]
        y_state = [jnp.dot(q[h], c_prev[h].astype(BF16), preferred_element_type=F32) for h in heads]
        pmat = [(qk[h] * e_intra[h]).astype(BF16) for h in heads]
        y_intra = [jnp.dot(pmat[h], v_ext[h], preferred_element_type=F32) for h in heads]
        c_loc = [jnp.dot(kt[h] * e_key[h], v_ext[h], preferred_element_type=F32) for h in heads]

        y_all = [w_intra[h] * y_intra[h] + w_state[h] * y_state[h] for h in heads]
        for h in heads:
            c_scr[h] = decay[h] * c_prev[h] + c_loc[h]
            m_scr[h:h + 1, :] = jnp.broadcast_to(m_new[h], (1, LANES))
        inv_den = [1.0 / jnp.maximum(jnp.abs(y_all[h][:, E:]), floor[h]) for h in heads]
        hh = [jnp.concatenate([y_all[h][:, j * LANES:(j + 1) * LANES] * inv_den[h]
                               for j in range(E // LANES)], axis=1) for h in heads]

        hm = [_sigmoid(om_ref[rows, hsl[h]]).astype(F32) * hh[h] for h in heads]
        mu = [jnp.mean(hm[h], axis=-1, keepdims=True) for h in heads]
        hc = [hm[h] - mu[h] for h in heads]
        var = [jnp.mean(hc[h] * hc[h], axis=-1, keepdims=True) for h in heads]
        for h in heads:
            hn = hc[h] * lax.rsqrt(var[h] + LN_EPS) * nw_ref[:, hsl[h]]
            y_ref[rows, hsl[h]] = ((hn + sk_ref[:, hsl[h]] * xc_ref[rows, hsl[h]].astype(F32)).astype(BF16)
                                   * _silu(zm_ref[rows, hsl[h]]))

    for c in range(q_ref.shape[0] // L):
        chunk(c)


def _mlstm(qm, kmt, vm, gates, rest3, xc, norm_w, skip):
    bsz, seq, _ = qm.shape
    tm = MLSTM_TM
    row_spec = pl.BlockSpec((None, tm, M_WIDTH), lambda b, i: (b, i, 0))
    return pl.pallas_call(
        _mlstm_kernel,
        grid=(bsz, seq // tm),
        in_specs=[
            row_spec,
            pl.BlockSpec((None, M_HEADS, M_HEAD_DIM, tm), lambda b, i: (b, 0, 0, i)),
            row_spec,
            pl.BlockSpec((None, tm, LANES), lambda b, i: (b, i, 0)),
            pl.BlockSpec((None, tm, M_WIDTH), lambda b, i: (b, i, 2)),
            pl.BlockSpec((None, tm, M_WIDTH), lambda b, i: (b, i, 1)),
            row_spec,
            pl.BlockSpec((1, M_WIDTH), lambda b, i: (0, 0)),
            pl.BlockSpec((1, M_WIDTH), lambda b, i: (0, 0)),
        ],
        out_specs=row_spec,
        out_shape=jax.ShapeDtypeStruct((bsz, seq, M_WIDTH), BF16),
        scratch_shapes=[pltpu.VMEM((M_HEADS, M_HEAD_DIM, M_HEAD_DIM + LANES), F32),
                        pltpu.VMEM((8, LANES), F32)],
        compiler_params=_params("arbitrary", "arbitrary"),
        name="mlstm",
    )(qm, kmt, vm, gates, rest3, rest3, xc, norm_w, skip)


def _post_kernel(o0_ref, o1_ref, o2_ref, s0_ref, s1_ref, s2_ref, za_ref, ym_ref, ga_ref, gm_ref,
                 x_ref, ada_ref, wpa_ref, wpm_ref, wout_ref, lng_ref, lnb_ref, out_ref,
                 *, alpha, d_model):
    hd = ATT_HEAD_DIM
    tm = x_ref.shape[0]
    gate = ada_ref[:, 2 * d_model:3 * d_model]
    for part in range(POST_PARTS):
        rs = slice(part * (tm // POST_PARTS), (part + 1) * (tm // POST_PARTS))
        parts = []
        for h in range(ATT_KV_HEADS):
            sl = slice(h * hd, (h + 1) * hd)
            l0 = s0_ref[rs, h:h + 1]
            l1 = s1_ref[rs, h:h + 1]
            l2 = s2_ref[rs, h:h + 1]
            mx = jnp.maximum(jnp.maximum(l0, l1), l2)
            e0 = jnp.exp(l0 - mx)
            e1 = jnp.exp(l1 - mx)
            e2 = jnp.exp(l2 - mx)
            tot = e0 + e1 + e2
            o = ((e0 / tot) * o0_ref[rs, sl].astype(F32) + (e1 / tot) * o1_ref[rs, sl].astype(F32)
                 + (e2 / tot) * o2_ref[rs, sl].astype(F32))
            parts.append(o.astype(BF16) * _silu(za_ref[rs, sl]))
        att_in = jnp.concatenate(parts, axis=1)
        y_att = jnp.dot(att_in, wpa_ref[...], preferred_element_type=F32)
        y_m = jnp.dot(ym_ref[rs, :], wpm_ref[...], preferred_element_type=F32)
        merged = (_sigmoid(ga_ref[rs, :]) * y_att.astype(BF16)
                  + _sigmoid(gm_ref[rs, :]) * y_m.astype(BF16))
        out = jnp.dot(merged, wout_ref[...], preferred_element_type=F32)
        res = alpha * x_ref[rs, :] + gate * out
        mu = jnp.mean(res, axis=-1, keepdims=True)
        rc = res - mu
        var = jnp.mean(rc * rc, axis=-1, keepdims=True)
        out_ref[rs, :] = (rc * lax.rsqrt(var + LN_EPS) * lng_ref[...] + lnb_ref[...]).astype(out_ref.dtype)


def _post(o_list, st_list, za2, rest2, ymin2, x2d, ada3, wpa, wpm, wout, ln_g, ln_b, seq, alpha):
    rows, d = x2d.shape
    tm = POST_TM
    tiles_per_seq = seq // tm
    kw = ATT_KV_WIDTH
    o_spec = pl.BlockSpec((tm, kw), lambda i: (i, 0))
    s_spec = pl.BlockSpec((tm, LANES), lambda i: (i, 0))
    const = lambda shape: pl.BlockSpec(shape, lambda i: (0,) * len(shape))
    return pl.pallas_call(
        functools.partial(_post_kernel, alpha=alpha, d_model=d),
        grid=(rows // tm,),
        in_specs=[
            o_spec, o_spec, o_spec, s_spec, s_spec, s_spec,
            pl.BlockSpec((tm, kw), lambda i: (i, 0)),
            pl.BlockSpec((tm, M_WIDTH), lambda i: (i, 0)),
            pl.BlockSpec((tm, d), lambda i: (i, 3)),
            pl.BlockSpec((tm, d), lambda i: (i, 4)),
            pl.BlockSpec((tm, d), lambda i: (i, 0)),
            pl.BlockSpec((None, 1, 3 * d), lambda i: (i // tiles_per_seq, 0, 0)),
            const((kw, d)), const((M_WIDTH, d)), const((d, d)),
            const((1, d)), const((1, d)),
        ],
        out_specs=pl.BlockSpec((tm, d), lambda i: (i, 0)),
        out_shape=jax.ShapeDtypeStruct((rows, d), x2d.dtype),
        compiler_params=_params("arbitrary"),
        name="post",
    )(*o_list, *st_list, za2, ymin2, rest2, rest2, x2d, ada3, wpa, wpm, wout, ln_g, ln_b)


def _layer(x, c, positions, w_ada, b_ada, w_in, conv_w, conv_b, w_qm, w_km, w_vm,
           w_if, b_if, mh_norm_w, skip_m, w_pa, w_pm, w_out, ln_g, ln_b, alpha):
    bsz, seq, d = x.shape
    rows = bsz * seq
    x2d = x.reshape(rows, d)

    ada3 = _ada(c, w_ada, b_ada).reshape(bsz, 1, 3 * d)

    half = ATT_HEAD_DIM // 2
    inv = jnp.power(ROPE_THETA, -jnp.arange(half, dtype=F32) / half)
    sign = jnp.concatenate([-jnp.ones((half,), F32), jnp.ones((half,), F32)])
    rope_tab = jnp.zeros((8, LANES), F32).at[0].set(jnp.concatenate([inv, inv])).at[1].set(sign)

    w_in_b = w_in.astype(BF16)
    q0, q1r, q2r, kv, kv4r, kv16r, za, h2d = _inproj(x2d, ada3, positions.reshape(rows, 1), rope_tab,
                                                     w_in_b, bsz, seq)
    rest = _restproj(h2d, w_in_b, 6 * ATT_KV_WIDTH)

    n_sub = seq // ATT_BLOCK
    o_list, st_list = [], []
    for q_g, kv_g, (_, dil) in zip((q0, q1r, q2r), (kv, kv4r, kv16r), ATT_GROUPS):
        kv_sub = kv_g.reshape(bsz, n_sub, ATT_BLOCK, 2 * ATT_KV_WIDTH)
        o_g, st_g = _attention_group(q_g.reshape(bsz, n_sub, ATT_BLOCK, ATT_KV_WIDTH), kv_sub, kv_sub,
                                     bsz, seq, dil)
        o_list.append(o_g.reshape(rows, ATT_KV_WIDTH))
        st_list.append(st_g.reshape(rows, LANES))

    rest3 = rest.reshape(bsz, seq, rest.shape[1])
    n_gate = 2 * M_HEADS
    wif3 = jnp.zeros((3, M_WIDTH, LANES), BF16).at[:, :, :n_gate].set(
        w_if.astype(BF16).reshape(3, M_WIDTH, n_gate))
    bif = jnp.zeros((1, LANES), F32).at[0, :n_gate].set(b_if.astype(F32))
    wk = (w_km * (M_HEAD_DIM ** -0.5)).astype(BF16)
    qm, kmt, vm, xc, gates = _mpre(rest3, conv_w.astype(F32), conv_b.astype(F32).reshape(1, M_WIDTH),
                                   w_qm.astype(BF16), wk, wk.transpose(0, 2, 1), w_vm.astype(BF16), wif3, bif)
    ymin = _mlstm(qm, kmt, vm, gates, rest3, xc,
                  mh_norm_w.astype(F32).reshape(1, M_WIDTH), skip_m.astype(F32).reshape(1, M_WIDTH))

    out = _post(o_list, st_list, za, rest, ymin.reshape(rows, M_WIDTH), x2d, ada3,
                w_pa.astype(BF16), w_pm.astype(BF16), w_out.astype(BF16),
                ln_g.astype(F32).reshape(1, d), ln_b.astype(F32).reshape(1, d), seq, alpha)
    return out.reshape(bsz, seq, d)


def kernel(x, c, positions, w_ada, b_ada, w_in, conv_w, conv_b, w_qm, w_km, w_vm, w_if, b_if,
           mh_norm_w, skip_m, w_pa, w_pm, w_out, ln_g, ln_b):
    depth = w_ada.shape[0]
    alpha = (2.0 * depth) ** 0.25
    for l in range(depth):
        x = _layer(x, c, positions, w_ada[l], b_ada[l], w_in[l], conv_w[l], conv_b[l],
                   w_qm[l], w_km[l], w_vm[l], w_if[l], b_if[l], mh_norm_w[l], skip_m[l],
                   w_pa[l], w_pm[l], w_out[l], ln_g[l], ln_b[l], alpha)
    return x
```

```python
import functools

import jax
import jax.numpy as jnp
from jax import lax
from jax.experimental import pallas as pl
from jax.experimental.pallas import tpu as pltpu

F32 = jnp.float32
BF16 = jnp.bfloat16

ATT_HEAD_DIM = 128
ATT_GROUPS = ((128, 1), (512, 4), (2048, 16))
ATT_KV_HEADS = 4
ATT_KV_WIDTH = ATT_KV_HEADS * ATT_HEAD_DIM
ATT_BLOCK = 128
ROPE_THETA = 10000.0
M_HEADS = 4
M_HEAD_DIM = 256
M_WIDTH = M_HEADS * M_HEAD_DIM
M_CONV = 4
M_CHUNK = 128
LN_EPS = 1e-5
NEG = -1e30

V7X_VMEM_LIMIT_BYTES = 56 * 1024 * 1024
LANES = 128

INPROJ_TM = 1024
INPROJ_TN = 1024
REST_TM = 2048
REST_TN = 1024
ATT_SUB = 8
ATT_BATCH = 4
ATT_SPAN_SUBS = 16
MPRE_TM = 1024
MLSTM_TM = 1024
POST_TM = 1024
POST_PARTS = 4
CONV_HALO = 16


def _sigmoid(v):
    return 0.5 * jnp.tanh(0.5 * v) + 0.5


def _silu(v):
    return v * _sigmoid(v)


def _params(*sem):
    return pltpu.CompilerParams(dimension_semantics=sem, vmem_limit_bytes=V7X_VMEM_LIMIT_BYTES)


def _ada_kernel(c_ref, w_ref, b_ref, o_ref):
    sc = _silu(c_ref[...])
    o_ref[...] = jnp.dot(sc, w_ref[...], precision=lax.Precision.HIGHEST,
                         preferred_element_type=F32) + b_ref[...]


def _ada(c, w_ada, b_ada):
    bsz, d = c.shape
    n = w_ada.shape[1]
    return pl.pallas_call(
        _ada_kernel,
        grid=(n // d,),
        in_specs=[pl.BlockSpec((bsz, d), lambda j: (0, 0)),
                  pl.BlockSpec((d, d), lambda j: (0, j)),
                  pl.BlockSpec((1, d), lambda j: (0, j))],
        out_specs=pl.BlockSpec((bsz, d), lambda j: (0, j)),
        out_shape=jax.ShapeDtypeStruct((bsz, n), F32),
        compiler_params=_params("arbitrary"),
        name="ada",
    )(c, w_ada, b_ada.reshape(1, n))


def _inproj_kernel(x_ref, ada_ref, pos_ref, tab_ref, w01_ref, w2z_ref, wkv_ref,
                   q0_ref, q1_ref, q2_ref, kv_ref, kv4_ref, kv16_ref, za_ref, h_ref,
                   cos_scr, sin_scr, stage_scr, stage4_scr, *, d_model):
    j = pl.program_id(1)
    tm = x_ref.shape[0]
    hd, kw = ATT_HEAD_DIM, ATT_KV_WIDTH
    dil4, dil16 = ATT_GROUPS[1][1], ATT_GROUPS[2][1]
    outer = dil16 // dil4

    @pl.when(j == 0)
    def _():
        x = x_ref[...]
        mu = jnp.mean(x, axis=-1, keepdims=True)
        xc = x - mu
        var = jnp.mean(xc * xc, axis=-1, keepdims=True)
        xn = xc * lax.rsqrt(var + LN_EPS)
        shift = ada_ref[:, 0:d_model]
        scale = ada_ref[:, d_model:2 * d_model]
        h_ref[...] = (xn * (1.0 + scale) + shift).astype(BF16)
        half = tm // 2
        lo_lane = lax.broadcasted_iota(jnp.int32, (half, LANES), 1) < hd // 2
        pos2 = jnp.where(lo_lane, pos_ref[0:half, :], pos_ref[half:tm, :])
        ang = pos2.astype(F32) * tab_ref[0:1, :]
        for fn, dst in ((jnp.cos, cos_scr), (jnp.sin, sin_scr)):
            val = fn(ang)
            swapped = pltpu.roll(val, hd // 2, axis=1)
            dst[0:half, :] = jnp.where(lo_lane, val, swapped)
            dst[half:tm, :] = jnp.where(lo_lane, swapped, val)
        sin_scr[...] = sin_scr[...] * tab_ref[1:2, :]

    def project(w_ref):
        return jnp.dot(h_ref[...], w_ref[...], preferred_element_type=F32)

    def heads(acc, rope, scale):
        for hh in range(ATT_KV_HEADS):
            xh = acc[:, hh * hd:(hh + 1) * hd]
            if rope:
                rot = pltpu.roll(xh, hd // 2, axis=1)
                xh = xh * cos_scr[...] + rot * sin_scr[...]
            if scale is not None:
                xh = xh * scale
            yield hh, xh

    def stage_heads(s, acc, rope, scale, nat_ref=None, col0=0):
        for hh, val in heads(acc, rope, scale):
            stage_scr[s, hh] = val
            if nat_ref is not None:
                nat_ref[:, col0 + hh * hd:col0 + (hh + 1) * hd] = val.astype(nat_ref.dtype)
        for hh in range(ATT_KV_HEADS):
            for r in range(dil4):
                stage4_scr[s, hh * dil4 + r] = stage_scr[s, hh, pl.ds(r, tm // dil4, stride=dil4), :]

    def emit_d4(s, dst_ref, col0=0):
        for hh in range(ATT_KV_HEADS):
            for r in range(dil4):
                for nn in range(tm // (ATT_BLOCK * dil4)):
                    rows = stage4_scr[s, hh * dil4 + r, nn * ATT_BLOCK:(nn + 1) * ATT_BLOCK, :]
                    dst_ref[nn, r, :, col0 + hh * hd:col0 + (hh + 1) * hd] = rows.astype(dst_ref.dtype)

    def emit_d16(s, dst_ref, col0=0):
        for hh in range(ATT_KV_HEADS):
            for r in range(dil4):
                for a in range(outer):
                    rows = stage4_scr[s, hh * dil4 + r, pl.ds(a, tm // dil16, stride=outer), :]
                    dst_ref[a * dil4 + r, :, col0 + hh * hd:col0 + (hh + 1) * hd] = rows.astype(dst_ref.dtype)

    q_scale = hd ** -0.5

    @pl.when(j == 0)
    def _():
        acc = project(w01_ref)
        for hh, val in heads(acc[:, :kw], True, q_scale):
            q0_ref[:, hh * hd:(hh + 1) * hd] = val.astype(q0_ref.dtype)
        stage_heads(0, acc[:, kw:], True, q_scale)
        emit_d4(0, q1_ref)

    @pl.when(j == 1)
    def _():
        acc = project(w2z_ref)
        stage_heads(0, acc[:, :kw], True, q_scale)
        emit_d16(0, q2_ref)
        za_ref[...] = acc[:, kw:].astype(za_ref.dtype)

    @pl.when(j == 2)
    def _():
        acc = project(wkv_ref)
        stage_heads(0, acc[:, :kw], True, None, kv_ref, 0)
        emit_d4(0, kv4_ref, 0)
        emit_d16(0, kv16_ref, 0)
        stage_heads(1, acc[:, kw:], False, None, kv_ref, kw)
        emit_d4(1, kv4_ref, kw)
        emit_d16(1, kv16_ref, kw)


def _inproj(x2d, ada3, pos2d, rope_tab, w_in_bf16, bsz, seq):
    rows, d = x2d.shape
    tm, tn = INPROJ_TM, INPROJ_TN
    kw = ATT_KV_WIDTH
    n_j = 3
    w01 = w_in_bf16[:, 0:2 * kw]
    w2z = jnp.concatenate([w_in_bf16[:, 2 * kw:3 * kw], w_in_bf16[:, 5 * kw:6 * kw]], axis=1)
    wkv = w_in_bf16[:, 3 * kw:5 * kw]
    tps = seq // tm
    dil4, dil16 = ATT_GROUPS[1][1], ATT_GROUPS[2][1]
    blk4, blk16 = ATT_BLOCK * dil4, ATT_BLOCK * dil16
    assert tm % blk4 == 0 and blk16 % tm == 0 and tn == 2 * kw and dil16 % dil4 == 0
    t16 = blk16 // tm

    out_shapes = (
        jax.ShapeDtypeStruct((rows, kw), BF16),
        jax.ShapeDtypeStruct((bsz, seq // blk4, dil4, ATT_BLOCK, kw), BF16),
        jax.ShapeDtypeStruct((bsz, seq // blk16, dil16, ATT_BLOCK, kw), BF16),
        jax.ShapeDtypeStruct((rows, 2 * kw), BF16),
        jax.ShapeDtypeStruct((bsz, seq // blk4, dil4, ATT_BLOCK, 2 * kw), BF16),
        jax.ShapeDtypeStruct((bsz, seq // blk16, dil16, ATT_BLOCK, 2 * kw), BF16),
        jax.ShapeDtypeStruct((rows, kw), BF16),
        jax.ShapeDtypeStruct((rows, d), BF16),
    )
    r4_map = lambda i, j: (i // tps, i % tps, 0, 0, 0)
    r16_map = lambda i, j: (i // tps, (i % tps) // t16, 0, (i % tps) % t16, 0)
    r4_spec = lambda w: pl.BlockSpec((None, tm // blk4, dil4, ATT_BLOCK, w), r4_map)
    r16_spec = lambda w: pl.BlockSpec((None, None, dil16, tm // dil16, w), r16_map)
    w_spec = pl.BlockSpec((d, tn), lambda i, j: (0, 0))
    return pl.pallas_call(
        functools.partial(_inproj_kernel, d_model=d),
        grid=(rows // tm, n_j),
        in_specs=[
            pl.BlockSpec((tm, d), lambda i, j: (i, 0)),
            pl.BlockSpec((None, 1, 3 * d), lambda i, j: (i // tps, 0, 0)),
            pl.BlockSpec((tm, 1), lambda i, j: (i, 0)),
            pl.BlockSpec((8, LANES), lambda i, j: (0, 0)),
            w_spec, w_spec, w_spec,
        ],
        out_specs=(
            pl.BlockSpec((tm, kw), lambda i, j: (i, 0)),
            r4_spec(kw),
            r16_spec(kw),
            pl.BlockSpec((tm, 2 * kw), lambda i, j: (i, 0)),
            r4_spec(2 * kw),
            r16_spec(2 * kw),
            pl.BlockSpec((tm, kw), lambda i, j: (i, 0)),
            pl.BlockSpec((tm, d), lambda i, j: (i, 0)),
        ),
        out_shape=out_shapes,
        scratch_shapes=[pltpu.VMEM((tm, LANES), F32),
                        pltpu.VMEM((tm, LANES), F32),
                        pltpu.VMEM((2, ATT_KV_HEADS, tm, ATT_HEAD_DIM), F32),
                        pltpu.VMEM((2, ATT_KV_HEADS * dil4, tm // dil4, ATT_HEAD_DIM), F32)],
        compiler_params=_params("arbitrary", "arbitrary"),
        name="inproj",
    )(x2d, ada3, pos2d, rope_tab, w01, w2z, wkv)


def _restproj_kernel(h_ref, w_ref, o_ref):
    o_ref[...] = jnp.dot(h_ref[...], w_ref[...], preferred_element_type=F32).astype(o_ref.dtype)


def _restproj(h2d, w_in_bf16, col0):
    rows, d = h2d.shape
    tm, tn = REST_TM, REST_TN
    n_cols = w_in_bf16.shape[1] - col0
    assert col0 % tn == 0 and n_cols % tn == 0
    return pl.pallas_call(
        _restproj_kernel,
        grid=(rows // tm, n_cols // tn),
        in_specs=[pl.BlockSpec((tm, d), lambda i, j: (i, 0)),
                  pl.BlockSpec((d, tn), lambda i, j: (0, col0 // tn + j))],
        out_specs=pl.BlockSpec((tm, tn), lambda i, j: (i, j)),
        out_shape=jax.ShapeDtypeStruct((rows, n_cols), BF16),
        compiler_params=_params("arbitrary", "arbitrary"),
        name="restproj",
    )(h2d, w_in_bf16)


def _attn_kernel(q_ref, kv_ref, kvp_ref, o_ref, st_ref, onat_scr, *, dil):
    span = pl.program_id(1)
    g = pl.program_id(2)
    hd, kw = ATT_HEAD_DIM, ATT_KV_WIDTH
    groups_per_span = ATT_SPAN_SUBS // ATT_SUB
    first = (span * ATT_SPAN_SUBS + g * ATT_SUB - dil) < 0
    thr_first = jnp.where(first, 2 * ATT_BLOCK, 0).astype(jnp.int32)

    row = lax.broadcasted_iota(jnp.int32, (ATT_BLOCK, 2 * ATT_BLOCK), 0)
    col = lax.broadcasted_iota(jnp.int32, (ATT_BLOCK, 2 * ATT_BLOCK), 1)
    in_prev = col < ATT_BLOCK
    rel = col - row

    def mask_bias(thr):
        slack = jnp.where(in_prev, rel - thr, ATT_BLOCK - rel)
        return jnp.where(slack >= 0, 0.0, NEG).astype(F32)

    bias_inner = mask_bias(jnp.int32(0))
    bias_edge = mask_bias(thr_first)
    ones = jnp.ones((2 * ATT_BLOCK, hd), BF16)
    lane = lax.broadcasted_iota(jnp.int32, (ATT_BLOCK, LANES), 1)
    nt = (((1,), (1,)), ((), ()))

    def dst_rows(u):
        if dil <= ATT_SUB:
            start = (g * (ATT_SUB // dil) + u // dil) * (ATT_BLOCK * dil) + u % dil
        else:
            start = g * ATT_SUB + u
        if dil == 1:
            return pl.ds(pl.multiple_of(start, ATT_BLOCK), ATT_BLOCK)
        return pl.ds(start, ATT_BLOCK, stride=dil)

    for u0 in range(0, ATT_SUB, ATT_BATCH):
        units = [(u, h) for u in range(u0, u0 + ATT_BATCH) for h in range(ATT_KV_HEADS)]
        scores, vexts = {}, {}
        for u, h in units:
            ksl = slice(h * hd, (h + 1) * hd)
            vsl = slice(kw + h * hd, kw + (h + 1) * hd)
            q = q_ref[u, :, ksl]
            if u >= dil:
                k_prev, v_prev, bias = kv_ref[u - dil, :, ksl], kv_ref[u - dil, :, vsl], bias_inner
            else:
                k_prev, v_prev, bias = kvp_ref[u, :, ksl], kvp_ref[u, :, vsl], bias_edge
            k_cat = jnp.concatenate([k_prev, kv_ref[u, :, ksl]], axis=0)
            v_cat = jnp.concatenate([v_prev, kv_ref[u, :, vsl]], axis=0)
            vexts[u, h] = jnp.concatenate([v_cat, ones], axis=1)
            scores[u, h] = lax.dot_general(q, k_cat, nt, preferred_element_type=F32) + bias
        maxes = {un: jnp.max(scores[un], axis=-1, keepdims=True) for un in units}
        probs = {un: jnp.exp(scores[un] - maxes[un]).astype(BF16) for un in units}
        outs = {un: jnp.dot(probs[un], vexts[un], preferred_element_type=F32) for un in units}
        for u in range(u0, u0 + ATT_BATCH):
            dst = dst_rows(u)
            stats = jnp.zeros((ATT_BLOCK, LANES), F32)
            for h in range(ATT_KV_HEADS):
                den = outs[u, h][:, hd:]
                onat_scr[h, dst, :] = outs[u, h][:, :hd] / den
                stats = jnp.where(lane == h, maxes[u, h] + jnp.log(den[:, 0:1]), stats)
            st_ref[dst, :] = stats

    @pl.when(g == groups_per_span - 1)
    def _():
        for h in range(ATT_KV_HEADS):
            o_ref[:, h * hd:(h + 1) * hd] = onat_scr[h].astype(o_ref.dtype)


def _attention_group(q_sub, kv_sub, kv_prev, bsz, seq, dil):
    kw = ATT_KV_WIDTH
    span_rows = ATT_SPAN_SUBS * ATT_BLOCK
    gps = ATT_SPAN_SUBS // ATT_SUB
    n_prev = min(dil, ATT_SUB)
    assert ATT_SPAN_SUBS % ATT_SUB == 0 and ATT_SUB % ATT_BATCH == 0
    assert (ATT_SUB % dil == 0) if dil <= ATT_SUB else (dil == ATT_SPAN_SUBS and dil % ATT_SUB == 0)
    own_map = lambda b, s, g: (b, s * gps + g, 0, 0)
    prev_map = lambda b, s, g: (b, jnp.maximum(s * (ATT_SPAN_SUBS // n_prev) + g * (ATT_SUB // n_prev)
                                               - dil // n_prev, 0), 0, 0)
    prev_spec = pl.BlockSpec((None, n_prev, ATT_BLOCK, 2 * kw), prev_map)
    return pl.pallas_call(
        functools.partial(_attn_kernel, dil=dil),
        grid=(bsz, seq // span_rows, gps),
        in_specs=[
            pl.BlockSpec((None, ATT_SUB, ATT_BLOCK, kw), own_map),
            pl.BlockSpec((None, ATT_SUB, ATT_BLOCK, 2 * kw), own_map),
            prev_spec,
        ],
        out_specs=(
            pl.BlockSpec((None, span_rows, kw), lambda b, s, g: (b, s, 0)),
            pl.BlockSpec((None, span_rows, LANES), lambda b, s, g: (b, s, 0)),
        ),
        out_shape=(jax.ShapeDtypeStruct((bsz, seq, kw), BF16),
                   jax.ShapeDtypeStruct((bsz, seq, LANES), F32)),
        scratch_shapes=[pltpu.VMEM((ATT_KV_HEADS, span_rows, ATT_HEAD_DIM), F32)],
        compiler_params=_params("arbitrary", "arbitrary", "arbitrary"),
        name=f"attn_d{dil}",
    )(q_sub, kv_sub, kv_prev)


def _mpre_kernel(xm_ref, halo_ref, cw_ref, cb_ref, wq_ref, wk_ref, wkt_ref, wv_ref, wif_ref, bif_ref,
                 qm_ref, kt_ref, vm_ref, xc_ref, g_ref, ext_scr):
    i = pl.program_id(1)
    tm = xm_ref.shape[0]
    xm = xm_ref[...]
    halo = halo_ref[...].astype(F32)
    ext_scr[0:CONV_HALO, :] = jnp.where(i > 0, halo, 0.0)
    ext_scr[CONV_HALO:CONV_HALO + tm, :] = xm.astype(F32)
    acc = jnp.zeros((tm, M_WIDTH), F32) + cb_ref[...]
    for k in range(M_CONV):
        start = CONV_HALO - (M_CONV - 1) + k
        acc = acc + cw_ref[k:k + 1, :] * ext_scr[pl.ds(start, tm), :]
    xc = _silu(acc)
    xc_b = xc.astype(BF16)
    xc_ref[...] = xc_b
    gates = jnp.zeros((tm, LANES), F32) + bif_ref[...]
    for h in range(M_HEADS):
        sl = slice(h * M_HEAD_DIM, (h + 1) * M_HEAD_DIM)
        qh = jnp.dot(xc_b[:, sl], wq_ref[h], preferred_element_type=F32).astype(BF16)
        kh = jnp.dot(xc_b[:, sl], wk_ref[h], preferred_element_type=F32).astype(BF16)
        vh = jnp.dot(xm[:, sl], wv_ref[h], preferred_element_type=F32).astype(BF16)
        qm_ref[:, sl] = qh
        vm_ref[:, sl] = vh
        kt_ref[h] = lax.dot_general(wkt_ref[h], xc_b[:, sl], (((1,), (1,)), ((), ())),
                                    preferred_element_type=F32).astype(BF16)
        gates = gates + jnp.dot(qh, wif_ref[0, sl, :], preferred_element_type=F32)
        gates = gates + jnp.dot(kh, wif_ref[1, sl, :], preferred_element_type=F32)
        gates = gates + jnp.dot(vh, wif_ref[2, sl, :], preferred_element_type=F32)
    lane = lax.broadcasted_iota(jnp.int32, gates.shape, 1)
    log_f = jnp.minimum(gates, 0.0) - jnp.log1p(jnp.exp(-jnp.abs(gates)))
    g_ref[...] = jnp.where(lane >= M_HEADS, log_f, gates)


def _mpre(rest3, conv_w, conv_b, wq, wk, wkt, wv, wif3, bif):
    bsz, seq, _ = rest3.shape
    tm = MPRE_TM
    halo_blocks = tm // CONV_HALO
    full = lambda shape: pl.BlockSpec(shape, lambda b, i: (0,) * len(shape))
    big = jax.ShapeDtypeStruct((bsz, seq, M_WIDTH), BF16)
    return pl.pallas_call(
        _mpre_kernel,
        grid=(bsz, seq // tm),
        in_specs=[
            pl.BlockSpec((None, tm, M_WIDTH), lambda b, i: (b, i, 0)),
            pl.BlockSpec((None, CONV_HALO, M_WIDTH),
                         lambda b, i: (b, jnp.maximum(i * halo_blocks - 1, 0), 0)),
            full((M_CONV, M_WIDTH)), full((1, M_WIDTH)),
            full((M_HEADS, M_HEAD_DIM, M_HEAD_DIM)), full((M_HEADS, M_HEAD_DIM, M_HEAD_DIM)),
            full((M_HEADS, M_HEAD_DIM, M_HEAD_DIM)), full((M_HEADS, M_HEAD_DIM, M_HEAD_DIM)),
            full((3, M_WIDTH, LANES)), full((1, LANES)),
        ],
        out_specs=(
            pl.BlockSpec((None, tm, M_WIDTH), lambda b, i: (b, i, 0)),
            pl.BlockSpec((None, M_HEADS, M_HEAD_DIM, tm), lambda b, i: (b, 0, 0, i)),
            pl.BlockSpec((None, tm, M_WIDTH), lambda b, i: (b, i, 0)),
            pl.BlockSpec((None, tm, M_WIDTH), lambda b, i: (b, i, 0)),
            pl.BlockSpec((None, tm, LANES), lambda b, i: (b, i, 0)),
        ),
        out_shape=(big, jax.ShapeDtypeStruct((bsz, M_HEADS, M_HEAD_DIM, seq), BF16), big, big,
                   jax.ShapeDtypeStruct((bsz, seq, LANES), F32)),
        scratch_shapes=[pltpu.VMEM((tm + CONV_HALO, M_WIDTH), F32)],
        compiler_params=_params("arbitrary", "arbitrary"),
        name="mpre",
    )(rest3, rest3, conv_w, conv_b, wq, wk, wkt, wv, wif3, bif)


def _mlstm_kernel(q_ref, kt_ref, v_ref, g_ref, om_ref, zm_ref, xc_ref, nw_ref, sk_ref,
                  y_ref, c_scr, m_scr):
    @pl.when(pl.program_id(1) == 0)
    def _():
        c_scr[...] = jnp.zeros_like(c_scr)
        m_scr[...] = jnp.zeros_like(m_scr)

    L, E = M_CHUNK, M_HEAD_DIM
    row = lax.broadcasted_iota(jnp.int32, (L, L), 0)
    col = lax.broadcasted_iota(jnp.int32, (L, L), 1)
    causal = col <= row
    tri = causal.astype(F32)
    ones_blk = jnp.ones((L, LANES), BF16)

    def chunk(c):
        rows = pl.ds(c * L, L)
        heads = range(M_HEADS)
        hsl = [slice(h * E, (h + 1) * E) for h in heads]
        g = g_ref[rows, :]
        bm = jnp.dot(tri, g, precision=lax.Precision.HIGHEST,
                     preferred_element_type=F32)
        g_t = g.T
        bm_t = bm.T

        b_col = [bm[:, M_HEADS + h:M_HEADS + h + 1] for h in heads]
        li_row = [g_t[h:h + 1, :] for h in heads]
        b_row = [bm_t[M_HEADS + h:M_HEADS + h + 1, :] for h in heads]
        b_last = [b_col[h][L - 1:L, :] for h in heads]
        dmat = [jnp.where(causal, b_col[h] - b_row[h] + li_row[h], NEG) for h in heads]
        m_intra = [jnp.max(dmat[h], axis=-1, keepdims=True) for h in heads]
        e_intra = [jnp.exp(dmat[h] - m_intra[h]) for h in heads]
        w_src = [b_last[h] - b_row[h] + li_row[h] for h in heads]
        a = [jnp.max(w_src[h], axis=1, keepdims=True) for h in heads]
        m_prev = [m_scr[h:h + 1, 0:1] for h in heads]
        m_new = [jnp.maximum(b_last[h] + m_prev[h], a[h]) for h in heads]
        decay = [jnp.exp(b_last[h] + m_prev[h] - m_new[h]) for h in heads]
        e_key = [jnp.exp(w_src[h] - m_new[h]).astype(BF16) for h in heads]
        g_col = [b_col[h] + m_prev[h] for h in heads]
        m_t = [jnp.maximum(g_col[h], m_intra[h]) for h in heads]
        w_intra = [jnp.exp(m_intra[h] - m_t[h]) for h in heads]
        w_state = [jnp.exp(g_col[h] - m_t[h]) for h in heads]
        floor = [jnp.exp(-m_t[h]) for h in heads]

        q = [q_ref[rows, hsl[h]] for h in heads]
        kt = [kt_ref[h, :, c * L:(c + 1) * L] for h in heads]
        v_ext = [jnp.concatenate([v_ref[rows, hsl[h]], ones_blk], axis=1) for h in heads]
        qk = [jnp.dot(q[h], kt[h], preferred_element_type=F32) for h in heads]
        c_prev = [c_scr[h] for h in heads]
        y_state = [jnp.dot(q[h], c_prev[h].astype(BF16), preferred_element_type=F32) for h in heads]
        pmat = [(qk[h] * e_intra[h]).astype(BF16) for h in heads]
        y_intra = [jnp.dot(pmat[h], v_ext[h], preferred_element_type=F32) for h in heads]
        c_loc = [jnp.dot(kt[h] * e_key[h], v_ext[h], preferred_element_type=F32) for h in heads]

        y_all = [w_intra[h] * y_intra[h] + w_state[h] * y_state[h] for h in heads]
        for h in heads:
            c_scr[h] = decay[h] * c_prev[h] + c_loc[h]
            m_scr[h:h + 1, :] = jnp.broadcast_to(m_new[h], (1, LANES))
        inv_den = [1.0 / jnp.maximum(jnp.abs(y_all[h][:, E:]), floor[h]) for h in heads]
        hh = [jnp.concatenate([y_all[h][:, j * LANES:(j + 1) * LANES] * inv_den[h]
                               for j in range(E // LANES)], axis=1) for h in heads]

        hm = [_sigmoid(om_ref[rows, hsl[h]]).astype(F32) * hh[h] for h in heads]
        mu = [jnp.mean(hm[h], axis=-1, keepdims=True) for h in heads]
        hc = [hm[h] - mu[h] for h in heads]
        var = [jnp.mean(hc[h] * hc[h], axis=-1, keepdims=True) for h in heads]
        for h in heads:
            hn = hc[h] * lax.rsqrt(var[h] + LN_EPS) * nw_ref[:, hsl[h]]
            y_ref[rows, hsl[h]] = ((hn + sk_ref[:, hsl[h]] * xc_ref[rows, hsl[h]].astype(F32)).astype(BF16)
                                   * _silu(zm_ref[rows, hsl[h]]))

    for c in range(q_ref.shape[0] // L):
        chunk(c)


def _mlstm(qm, kmt, vm, gates, rest3, xc, norm_w, skip):
    bsz, seq, _ = qm.shape
    tm = MLSTM_TM
    row_spec = pl.BlockSpec((None, tm, M_WIDTH), lambda b, i: (b, i, 0))
    return pl.pallas_call(
        _mlstm_kernel,
        grid=(bsz, seq // tm),
        in_specs=[
            row_spec,
            pl.BlockSpec((None, M_HEADS, M_HEAD_DIM, tm), lambda b, i: (b, 0, 0, i)),
            row_spec,
            pl.BlockSpec((None, tm, LANES), lambda b, i: (b, i, 0)),
            pl.BlockSpec((None, tm, M_WIDTH), lambda b, i: (b, i, 2)),
            pl.BlockSpec((None, tm, M_WIDTH), lambda b, i: (b, i, 1)),
            row_spec,
            pl.BlockSpec((1, M_WIDTH), lambda b, i: (0, 0)),
            pl.BlockSpec((1, M_WIDTH), lambda b, i: (0, 0)),
        ],
        out_specs=row_spec,
        out_shape=jax.ShapeDtypeStruct((bsz, seq, M_WIDTH), BF16),
        scratch_shapes=[pltpu.VMEM((M_HEADS, M_HEAD_DIM, M_HEAD_DIM + LANES), F32),
                        pltpu.VMEM((8, LANES), F32)],
        compiler_params=_params("arbitrary", "arbitrary"),
        name="mlstm",
    )(qm, kmt, vm, gates, rest3, rest3, xc, norm_w, skip)


def _post_kernel(o0_ref, o1_ref, o2_ref, s0_ref, s1_ref, s2_ref, za_ref, ym_ref, ga_ref, gm_ref,
                 x_ref, ada_ref, wpa_ref, wpm_ref, wout_ref, lng_ref, lnb_ref, out_ref,
                 *, alpha, d_model):
    hd = ATT_HEAD_DIM
    tm = x_ref.shape[0]
    gate = ada_ref[:, 2 * d_model:3 * d_model]
    for part in range(POST_PARTS):
        rs = slice(part * (tm // POST_PARTS), (part + 1) * (tm // POST_PARTS))
        parts = []
        for h in range(ATT_KV_HEADS):
            sl = slice(h * hd, (h + 1) * hd)
            l0 = s0_ref[rs, h:h + 1]
            l1 = s1_ref[rs, h:h + 1]
            l2 = s2_ref[rs, h:h + 1]
            mx = jnp.maximum(jnp.maximum(l0, l1), l2)
            e0 = jnp.exp(l0 - mx)
            e1 = jnp.exp(l1 - mx)
            e2 = jnp.exp(l2 - mx)
            tot = e0 + e1 + e2
            o = ((e0 / tot) * o0_ref[rs, sl].astype(F32) + (e1 / tot) * o1_ref[rs, sl].astype(F32)
                 + (e2 / tot) * o2_ref[rs, sl].astype(F32))
            parts.append(o.astype(BF16) * _silu(za_ref[rs, sl]))
        att_in = jnp.concatenate(parts, axis=1)
        y_att = jnp.dot(att_in, wpa_ref[...], preferred_element_type=F32)
        y_m = jnp.dot(ym_ref[rs, :], wpm_ref[...], preferred_element_type=F32)
        merged = (_sigmoid(ga_ref[rs, :]) * y_att.astype(BF16)
                  + _sigmoid(gm_ref[rs, :]) * y_m.astype(BF16))
        out = jnp.dot(merged, wout_ref[...], preferred_element_type=F32)
        res = alpha * x_ref[rs, :] + gate * out
        mu = jnp.mean(res, axis=-1, keepdims=True)
        rc = res - mu
        var = jnp.mean(rc * rc, axis=-1, keepdims=True)
        out_ref[rs, :] = (rc * lax.rsqrt(var + LN_EPS) * lng_ref[...] + lnb_ref[...]).astype(out_ref.dtype)


def _post(o_list, st_list, za2, rest2, ymin2, x2d, ada3, wpa, wpm, wout, ln_g, ln_b, seq, alpha):
    rows, d = x2d.shape
    tm = POST_TM
    tiles_per_seq = seq // tm
    kw = ATT_KV_WIDTH
    o_spec = pl.BlockSpec((tm, kw), lambda i: (i, 0))
    s_spec = pl.BlockSpec((tm, LANES), lambda i: (i, 0))
    const = lambda shape: pl.BlockSpec(shape, lambda i: (0,) * len(shape))
    return pl.pallas_call(
        functools.partial(_post_kernel, alpha=alpha, d_model=d),
        grid=(rows // tm,),
        in_specs=[
            o_spec, o_spec, o_spec, s_spec, s_spec, s_spec,
            pl.BlockSpec((tm, kw), lambda i: (i, 0)),
            pl.BlockSpec((tm, M_WIDTH), lambda i: (i, 0)),
            pl.BlockSpec((tm, d), lambda i: (i, 3)),
            pl.BlockSpec((tm, d), lambda i: (i, 4)),
            pl.BlockSpec((tm, d), lambda i: (i, 0)),
            pl.BlockSpec((None, 1, 3 * d), lambda i: (i // tiles_per_seq, 0, 0)),
            const((kw, d)), const((M_WIDTH, d)), const((d, d)),
            const((1, d)), const((1, d)),
        ],
        out_specs=pl.BlockSpec((tm, d), lambda i: (i, 0)),
        out_shape=jax.ShapeDtypeStruct((rows, d), x2d.dtype),
        compiler_params=_params("arbitrary"),
        name="post",
    )(*o_list, *st_list, za2, ymin2, rest2, rest2, x2d, ada3, wpa, wpm, wout, ln_g, ln_b)


def _layer(x, c, positions, w_ada, b_ada, w_in, conv_w, conv_b, w_qm, w_km, w_vm,
           w_if, b_if, mh_norm_w, skip_m, w_pa, w_pm, w_out, ln_g, ln_b, alpha):
    bsz, seq, d = x.shape
    rows = bsz * seq
    x2d = x.reshape(rows, d)

    ada3 = _ada(c, w_ada, b_ada).reshape(bsz, 1, 3 * d)

    half = ATT_HEAD_DIM // 2
    inv = jnp.power(ROPE_THETA, -jnp.arange(half, dtype=F32) / half)
    sign = jnp.concatenate([-jnp.ones((half,), F32), jnp.ones((half,), F32)])
    rope_tab = jnp.zeros((8, LANES), F32).at[0].set(jnp.concatenate([inv, inv])).at[1].set(sign)

    w_in_b = w_in.astype(BF16)
    q0, q1r, q2r, kv, kv4r, kv16r, za, h2d = _inproj(x2d, ada3, positions.reshape(rows, 1), rope_tab,
                                                     w_in_b, bsz, seq)
    rest = _restproj(h2d, w_in_b, 6 * ATT_KV_WIDTH)

    n_sub = seq // ATT_BLOCK
    o_list, st_list = [], []
    for q_g, kv_g, (_, dil) in zip((q0, q1r, q2r), (kv, kv4r, kv16r), ATT_GROUPS):
        kv_sub = kv_g.reshape(bsz, n_sub, ATT_BLOCK, 2 * ATT_KV_WIDTH)
        o_g, st_g = _attention_group(q_g.reshape(bsz, n_sub, ATT_BLOCK, ATT_KV_WIDTH), kv_sub, kv_sub,
                                     bsz, seq, dil)
        o_list.append(o_g.reshape(rows, ATT_KV_WIDTH))
        st_list.append(st_g.reshape(rows, LANES))

    rest3 = rest.reshape(bsz, seq, rest.shape[1])
    n_gate = 2 * M_HEADS
    wif3 = jnp.zeros((3, M_WIDTH, LANES), BF16).at[:, :, :n_gate].set(
        w_if.astype(BF16).reshape(3, M_WIDTH, n_gate))
    bif = jnp.zeros((1, LANES), F32).at[0, :n_gate].set(b_if.astype(F32))
    wk = (w_km * (M_HEAD_DIM ** -0.5)).astype(BF16)
    qm, kmt, vm, xc, gates = _mpre(rest3, conv_w.astype(F32), conv_b.astype(F32).reshape(1, M_WIDTH),
                                   w_qm.astype(BF16), wk, wk.transpose(0, 2, 1), w_vm.astype(BF16), wif3, bif)
    ymin = _mlstm(qm, kmt, vm, gates, rest3, xc,
                  mh_norm_w.astype(F32).reshape(1, M_WIDTH), skip_m.astype(F32).reshape(1, M_WIDTH))

    out = _post(o_list, st_list, za, rest, ymin.reshape(rows, M_WIDTH), x2d, ada3,
                w_pa.astype(BF16), w_pm.astype(BF16), w_out.astype(BF16),
                ln_g.astype(F32).reshape(1, d), ln_b.astype(F32).reshape(1, d), seq, alpha)
    return out.reshape(bsz, seq, d)


def kernel(x, c, positions, w_ada, b_ada, w_in, conv_w, conv_b, w_qm, w_km, w_vm, w_if, b_if,
           mh_norm_w, skip_m, w_pa, w_pm, w_out, ln_g, ln_b):
    depth = w_ada.shape[0]
    alpha = (2.0 * depth) ** 0.25
    for l in range(depth):
        x = _layer(x, c, positions, w_ada[l], b_ada[l], w_in[l], conv_w[l], conv_b[l],
                   w_qm[l], w_km[l], w_vm[l], w_if[l], b_if[l], mh_norm_w[l], skip_m[l],
                   w_pa[l], w_pm[l], w_out[l], ln_g[l], ln_b[l], alpha)
    return x
```

```python
import functools

import jax
import jax.numpy as jnp
from jax import lax
from jax.experimental import pallas as pl
from jax.experimental.pallas import tpu as pltpu

F32 = jnp.float32
BF16 = jnp.bfloat16

ATT_HEAD_DIM = 128
ATT_GROUPS = ((128, 1), (512, 4), (2048, 16))
ATT_KV_HEADS = 4
ATT_KV_WIDTH = ATT_KV_HEADS * ATT_HEAD_DIM
ATT_BLOCK = 128
ROPE_THETA = 10000.0
M_HEADS = 4
M_HEAD_DIM = 256
M_WIDTH = M_HEADS * M_HEAD_DIM
M_CONV = 4
M_CHUNK = 128
LN_EPS = 1e-5
NEG = -1e30

V7X_VMEM_LIMIT_BYTES = 56 * 1024 * 1024
LANES = 128

INPROJ_TM = 1024
INPROJ_TN = 1024
REST_TM = 2048
REST_TN = 1024
ATT_SUB = 16
ATT_BATCH = 4
ATT_SPAN_SUBS = 16
MPRE_TM = 1024
MLSTM_TM = 512
POST_TM = 1024
POST_PARTS = 4
CONV_HALO = 16


def _sigmoid(v):
    return 0.5 * jnp.tanh(0.5 * v) + 0.5


def _silu(v):
    return v * _sigmoid(v)


def _params(*sem):
    return pltpu.CompilerParams(dimension_semantics=sem, vmem_limit_bytes=V7X_VMEM_LIMIT_BYTES)


def _ada_kernel(c_ref, w_ref, b_ref, o_ref):
    sc = _silu(c_ref[...])
    o_ref[...] = jnp.dot(sc, w_ref[...], precision=lax.Precision.HIGHEST,
                         preferred_element_type=F32) + b_ref[...]


def _ada(c, w_ada, b_ada):
    bsz, d = c.shape
    n = w_ada.shape[1]
    return pl.pallas_call(
        _ada_kernel,
        grid=(n // d,),
        in_specs=[pl.BlockSpec((bsz, d), lambda j: (0, 0)),
                  pl.BlockSpec((d, d), lambda j: (0, j)),
                  pl.BlockSpec((1, d), lambda j: (0, j))],
        out_specs=pl.BlockSpec((bsz, d), lambda j: (0, j)),
        out_shape=jax.ShapeDtypeStruct((bsz, n), F32),
        compiler_params=_params("arbitrary"),
        name="ada",
    )(c, w_ada, b_ada.reshape(1, n))


def _inproj_kernel(x_ref, ada_ref, pos_ref, tab_ref, w01_ref, w2z_ref, wkv_ref,
                   q0_ref, q1_ref, q2_ref, kv_ref, kv4_ref, kv16_ref, za_ref, h_ref,
                   cos_scr, sin_scr, stage_scr, stage4_scr, *, d_model):
    j = pl.program_id(1)
    tm = x_ref.shape[0]
    hd, kw = ATT_HEAD_DIM, ATT_KV_WIDTH
    dil4, dil16 = ATT_GROUPS[1][1], ATT_GROUPS[2][1]
    outer = dil16 // dil4

    @pl.when(j == 0)
    def _():
        x = x_ref[...]
        mu = jnp.mean(x, axis=-1, keepdims=True)
        xc = x - mu
        var = jnp.mean(xc * xc, axis=-1, keepdims=True)
        xn = xc * lax.rsqrt(var + LN_EPS)
        shift = ada_ref[:, 0:d_model]
        scale = ada_ref[:, d_model:2 * d_model]
        h_ref[...] = (xn * (1.0 + scale) + shift).astype(BF16)
        half = tm // 2
        lo_lane = lax.broadcasted_iota(jnp.int32, (half, LANES), 1) < hd // 2
        pos2 = jnp.where(lo_lane, pos_ref[0:half, :], pos_ref[half:tm, :])
        ang = pos2.astype(F32) * tab_ref[0:1, :]
        for fn, dst in ((jnp.cos, cos_scr), (jnp.sin, sin_scr)):
            val = fn(ang)
            swapped = pltpu.roll(val, hd // 2, axis=1)
            dst[0:half, :] = jnp.where(lo_lane, val, swapped)
            dst[half:tm, :] = jnp.where(lo_lane, swapped, val)
        sin_scr[...] = sin_scr[...] * tab_ref[1:2, :]

    def project(w_ref):
        return jnp.dot(h_ref[...], w_ref[...], preferred_element_type=F32)

    def heads(acc, rope, scale):
        for hh in range(ATT_KV_HEADS):
            xh = acc[:, hh * hd:(hh + 1) * hd]
            if rope:
                rot = pltpu.roll(xh, hd // 2, axis=1)
                xh = xh * cos_scr[...] + rot * sin_scr[...]
            if scale is not None:
                xh = xh * scale
            yield hh, xh

    def stage_heads(s, acc, rope, scale, nat_ref=None, col0=0):
        for hh, val in heads(acc, rope, scale):
            stage_scr[s, hh] = val
            if nat_ref is not None:
                nat_ref[:, col0 + hh * hd:col0 + (hh + 1) * hd] = val.astype(nat_ref.dtype)
        for hh in range(ATT_KV_HEADS):
            for r in range(dil4):
                stage4_scr[s, hh * dil4 + r] = stage_scr[s, hh, pl.ds(r, tm // dil4, stride=dil4), :]

    def emit_d4(s, dst_ref, col0=0):
        for hh in range(ATT_KV_HEADS):
            for r in range(dil4):
                for nn in range(tm // (ATT_BLOCK * dil4)):
                    rows = stage4_scr[s, hh * dil4 + r, nn * ATT_BLOCK:(nn + 1) * ATT_BLOCK, :]
                    dst_ref[nn, r, :, col0 + hh * hd:col0 + (hh + 1) * hd] = rows.astype(dst_ref.dtype)

    def emit_d16(s, dst_ref, col0=0):
        for hh in range(ATT_KV_HEADS):
            for r in range(dil4):
                for a in range(outer):
                    rows = stage4_scr[s, hh * dil4 + r, pl.ds(a, tm // dil16, stride=outer), :]
                    dst_ref[a * dil4 + r, :, col0 + hh * hd:col0 + (hh + 1) * hd] = rows.astype(dst_ref.dtype)

    q_scale = hd ** -0.5

    @pl.when(j == 0)
    def _():
        acc = project(w01_ref)
        for hh, val in heads(acc[:, :kw], True, q_scale):
            q0_ref[:, hh * hd:(hh + 1) * hd] = val.astype(q0_ref.dtype)
        stage_heads(0, acc[:, kw:], True, q_scale)
        emit_d4(0, q1_ref)

    @pl.when(j == 1)
    def _():
        acc = project(w2z_ref)
        stage_heads(0, acc[:, :kw], True, q_scale)
        emit_d16(0, q2_ref)
        za_ref[...] = acc[:, kw:].astype(za_ref.dtype)

    @pl.when(j == 2)
    def _():
        acc = project(wkv_ref)
        stage_heads(0, acc[:, :kw], True, None, kv_ref, 0)
        emit_d4(0, kv4_ref, 0)
        emit_d16(0, kv16_ref, 0)
        stage_heads(1, acc[:, kw:], False, None, kv_ref, kw)
        emit_d4(1, kv4_ref, kw)
        emit_d16(1, kv16_ref, kw)


def _inproj(x2d, ada3, pos2d, rope_tab, w_in_bf16, bsz, seq):
    rows, d = x2d.shape
    tm, tn = INPROJ_TM, INPROJ_TN
    kw = ATT_KV_WIDTH
    n_j = 3
    w01 = w_in_bf16[:, 0:2 * kw]
    w2z = jnp.concatenate([w_in_bf16[:, 2 * kw:3 * kw], w_in_bf16[:, 5 * kw:6 * kw]], axis=1)
    wkv = w_in_bf16[:, 3 * kw:5 * kw]
    tps = seq // tm
    dil4, dil16 = ATT_GROUPS[1][1], ATT_GROUPS[2][1]
    blk4, blk16 = ATT_BLOCK * dil4, ATT_BLOCK * dil16
    assert tm % blk4 == 0 and blk16 % tm == 0 and tn == 2 * kw and dil16 % dil4 == 0
    t16 = blk16 // tm

    out_shapes = (
        jax.ShapeDtypeStruct((rows, kw), BF16),
        jax.ShapeDtypeStruct((bsz, seq // blk4, dil4, ATT_BLOCK, kw), BF16),
        jax.ShapeDtypeStruct((bsz, seq // blk16, dil16, ATT_BLOCK, kw), BF16),
        jax.ShapeDtypeStruct((rows, 2 * kw), BF16),
        jax.ShapeDtypeStruct((bsz, seq // blk4, dil4, ATT_BLOCK, 2 * kw), BF16),
        jax.ShapeDtypeStruct((bsz, seq // blk16, dil16, ATT_BLOCK, 2 * kw), BF16),
        jax.ShapeDtypeStruct((rows, kw), BF16),
        jax.ShapeDtypeStruct((rows, d), BF16),
    )
    r4_map = lambda i, j: (i // tps, i % tps, 0, 0, 0)
    r16_map = lambda i, j: (i // tps, (i % tps) // t16, 0, (i % tps) % t16, 0)
    r4_spec = lambda w: pl.BlockSpec((None, tm // blk4, dil4, ATT_BLOCK, w), r4_map)
    r16_spec = lambda w: pl.BlockSpec((None, None, dil16, tm // dil16, w), r16_map)
    w_spec = pl.BlockSpec((d, tn), lambda i, j: (0, 0))
    return pl.pallas_call(
        functools.partial(_inproj_kernel, d_model=d),
        grid=(rows // tm, n_j),
        in_specs=[
            pl.BlockSpec((tm, d), lambda i, j: (i, 0)),
            pl.BlockSpec((None, 1, 3 * d), lambda i, j: (i // tps, 0, 0)),
            pl.BlockSpec((tm, 1), lambda i, j: (i, 0)),
            pl.BlockSpec((8, LANES), lambda i, j: (0, 0)),
            w_spec, w_spec, w_spec,
        ],
        out_specs=(
            pl.BlockSpec((tm, kw), lambda i, j: (i, 0)),
            r4_spec(kw),
            r16_spec(kw),
            pl.BlockSpec((tm, 2 * kw), lambda i, j: (i, 0)),
            r4_spec(2 * kw),
            r16_spec(2 * kw),
            pl.BlockSpec((tm, kw), lambda i, j: (i, 0)),
            pl.BlockSpec((tm, d), lambda i, j: (i, 0)),
        ),
        out_shape=out_shapes,
        scratch_shapes=[pltpu.VMEM((tm, LANES), F32),
                        pltpu.VMEM((tm, LANES), F32),
                        pltpu.VMEM((2, ATT_KV_HEADS, tm, ATT_HEAD_DIM), F32),
                        pltpu.VMEM((2, ATT_KV_HEADS * dil4, tm // dil4, ATT_HEAD_DIM), F32)],
        compiler_params=_params("arbitrary", "arbitrary"),
        name="inproj",
    )(x2d, ada3, pos2d, rope_tab, w01, w2z, wkv)


def _restproj_kernel(h_ref, w_ref, o_ref):
    o_ref[...] = jnp.dot(h_ref[...], w_ref[...], preferred_element_type=F32).astype(o_ref.dtype)


def _restproj(h2d, w_in_bf16, col0):
    rows, d = h2d.shape
    tm, tn = REST_TM, REST_TN
    n_cols = w_in_bf16.shape[1] - col0
    assert col0 % tn == 0 and n_cols % tn == 0
    return pl.pallas_call(
        _restproj_kernel,
        grid=(rows // tm, n_cols // tn),
        in_specs=[pl.BlockSpec((tm, d), lambda i, j: (i, 0)),
                  pl.BlockSpec((d, tn), lambda i, j: (0, col0 // tn + j))],
        out_specs=pl.BlockSpec((tm, tn), lambda i, j: (i, j)),
        out_shape=jax.ShapeDtypeStruct((rows, n_cols), BF16),
        compiler_params=_params("arbitrary", "arbitrary"),
        name="restproj",
    )(h2d, w_in_bf16)


def _attn_kernel(q_ref, kv_ref, kvp_ref, o_ref, st_ref, onat_scr, *, dil):
    span = pl.program_id(1)
    g = pl.program_id(2)
    hd, kw = ATT_HEAD_DIM, ATT_KV_WIDTH
    groups_per_span = ATT_SPAN_SUBS // ATT_SUB
    first = (span * ATT_SPAN_SUBS + g * ATT_SUB - dil) < 0
    thr_first = jnp.where(first, 2 * ATT_BLOCK, 0).astype(jnp.int32)

    row = lax.broadcasted_iota(jnp.int32, (ATT_BLOCK, 2 * ATT_BLOCK), 0)
    col = lax.broadcasted_iota(jnp.int32, (ATT_BLOCK, 2 * ATT_BLOCK), 1)
    in_prev = col < ATT_BLOCK
    rel = col - row

    def mask_bias(thr):
        slack = jnp.where(in_prev, rel - thr, ATT_BLOCK - rel)
        return jnp.where(slack >= 0, 0.0, NEG).astype(F32)

    bias_inner = mask_bias(jnp.int32(0))
    bias_edge = mask_bias(thr_first)
    ones = jnp.ones((2 * ATT_BLOCK, hd), BF16)
    lane = lax.broadcasted_iota(jnp.int32, (ATT_BLOCK, LANES), 1)
    nt = (((1,), (1,)), ((), ()))

    def dst_rows(u):
        if dil <= ATT_SUB:
            start = (g * (ATT_SUB // dil) + u // dil) * (ATT_BLOCK * dil) + u % dil
        else:
            start = g * ATT_SUB + u
        if dil == 1:
            return pl.ds(pl.multiple_of(start, ATT_BLOCK), ATT_BLOCK)
        return pl.ds(start, ATT_BLOCK, stride=dil)

    for u0 in range(0, ATT_SUB, ATT_BATCH):
        units = [(u, h) for u in range(u0, u0 + ATT_BATCH) for h in range(ATT_KV_HEADS)]
        scores, vexts = {}, {}
        for u, h in units:
            ksl = slice(h * hd, (h + 1) * hd)
            vsl = slice(kw + h * hd, kw + (h + 1) * hd)
            q = q_ref[u, :, ksl]
            if u >= dil:
                k_prev, v_prev, bias = kv_ref[u - dil, :, ksl], kv_ref[u - dil, :, vsl], bias_inner
            else:
                k_prev, v_prev, bias = kvp_ref[u, :, ksl], kvp_ref[u, :, vsl], bias_edge
            k_cat = jnp.concatenate([k_prev, kv_ref[u, :, ksl]], axis=0)
            v_cat = jnp.concatenate([v_prev, kv_ref[u, :, vsl]], axis=0)
            vexts[u, h] = jnp.concatenate([v_cat, ones], axis=1)
            scores[u, h] = lax.dot_general(q, k_cat, nt, preferred_element_type=F32) + bias
        maxes = {un: jnp.max(scores[un], axis=-1, keepdims=True) for un in units}
        probs = {un: jnp.exp(scores[un] - maxes[un]).astype(BF16) for un in units}
        outs = {un: jnp.dot(probs[un], vexts[un], preferred_element_type=F32) for un in units}
        for u in range(u0, u0 + ATT_BATCH):
            dst = dst_rows(u)
            stats = jnp.zeros((ATT_BLOCK, LANES), F32)
            for h in range(ATT_KV_HEADS):
                den = outs[u, h][:, hd:]
                onat_scr[h, dst, :] = outs[u, h][:, :hd] / den
                stats = jnp.where(lane == h, maxes[u, h] + jnp.log(den[:, 0:1]), stats)
            st_ref[dst, :] = stats

    @pl.when(g == groups_per_span - 1)
    def _():
        for h in range(ATT_KV_HEADS):
            o_ref[:, h * hd:(h + 1) * hd] = onat_scr[h].astype(o_ref.dtype)


def _attention_group(q_sub, kv_sub, kv_prev, bsz, seq, dil):
    kw = ATT_KV_WIDTH
    span_rows = ATT_SPAN_SUBS * ATT_BLOCK
    gps = ATT_SPAN_SUBS // ATT_SUB
    n_prev = min(dil, ATT_SUB)
    assert ATT_SPAN_SUBS % ATT_SUB == 0 and ATT_SUB % ATT_BATCH == 0
    assert (ATT_SUB % dil == 0) if dil <= ATT_SUB else (dil == ATT_SPAN_SUBS and dil % ATT_SUB == 0)
    own_map = lambda b, s, g: (b, s * gps + g, 0, 0)
    prev_map = lambda b, s, g: (b, jnp.maximum(s * (ATT_SPAN_SUBS // n_prev) + g * (ATT_SUB // n_prev)
                                               - dil // n_prev, 0), 0, 0)
    prev_spec = pl.BlockSpec((None, n_prev, ATT_BLOCK, 2 * kw), prev_map)
    return pl.pallas_call(
        functools.partial(_attn_kernel, dil=dil),
        grid=(bsz, seq // span_rows, gps),
        in_specs=[
            pl.BlockSpec((None, ATT_SUB, ATT_BLOCK, kw), own_map),
            pl.BlockSpec((None, ATT_SUB, ATT_BLOCK, 2 * kw), own_map),
            prev_spec,
        ],
        out_specs=(
            pl.BlockSpec((None, span_rows, kw), lambda b, s, g: (b, s, 0)),
            pl.BlockSpec((None, span_rows, LANES), lambda b, s, g: (b, s, 0)),
        ),
        out_shape=(jax.ShapeDtypeStruct((bsz, seq, kw), BF16),
                   jax.ShapeDtypeStruct((bsz, seq, LANES), F32)),
        scratch_shapes=[pltpu.VMEM((ATT_KV_HEADS, span_rows, ATT_HEAD_DIM), F32)],
        compiler_params=_params("arbitrary", "arbitrary", "arbitrary"),
        name=f"attn_d{dil}",
    )(q_sub, kv_sub, kv_prev)


def _mpre_kernel(xm_ref, halo_ref, cw_ref, cb_ref, wq_ref, wk_ref, wkt_ref, wv_ref, wif_ref, bif_ref,
                 qm_ref, kt_ref, vm_ref, xc_ref, g_ref, ext_scr):
    i = pl.program_id(1)
    tm = xm_ref.shape[0]
    xm = xm_ref[...]
    halo = halo_ref[...].astype(F32)
    ext_scr[0:CONV_HALO, :] = jnp.where(i > 0, halo, 0.0)
    ext_scr[CONV_HALO:CONV_HALO + tm, :] = xm.astype(F32)
    acc = jnp.zeros((tm, M_WIDTH), F32) + cb_ref[...]
    for k in range(M_CONV):
        start = CONV_HALO - (M_CONV - 1) + k
        acc = acc + cw_ref[k:k + 1, :] * ext_scr[pl.ds(start, tm), :]
    xc = _silu(acc)
    xc_b = xc.astype(BF16)
    xc_ref[...] = xc_b
    gates = jnp.zeros((tm, LANES), F32) + bif_ref[...]
    for h in range(M_HEADS):
        sl = slice(h * M_HEAD_DIM, (h + 1) * M_HEAD_DIM)
        qh = jnp.dot(xc_b[:, sl], wq_ref[h], preferred_element_type=F32).astype(BF16)
        kh = jnp.dot(xc_b[:, sl], wk_ref[h], preferred_element_type=F32).astype(BF16)
        vh = jnp.dot(xm[:, sl], wv_ref[h], preferred_element_type=F32).astype(BF16)
        qm_ref[:, sl] = qh
        vm_ref[:, sl] = vh
        kt_ref[h] = lax.dot_general(wkt_ref[h], xc_b[:, sl], (((1,), (1,)), ((), ())),
                                    preferred_element_type=F32).astype(BF16)
        gates = gates + jnp.dot(qh, wif_ref[0, sl, :], preferred_element_type=F32)
        gates = gates + jnp.dot(kh, wif_ref[1, sl, :], preferred_element_type=F32)
        gates = gates + jnp.dot(vh, wif_ref[2, sl, :], preferred_element_type=F32)
    lane = lax.broadcasted_iota(jnp.int32, gates.shape, 1)
    log_f = jnp.minimum(gates, 0.0) - jnp.log1p(jnp.exp(-jnp.abs(gates)))
    g_ref[...] = jnp.where(lane >= M_HEADS, log_f, gates)


def _mpre(rest3, conv_w, conv_b, wq, wk, wkt, wv, wif3, bif):
    bsz, seq, _ = rest3.shape
    tm = MPRE_TM
    halo_blocks = tm // CONV_HALO
    full = lambda shape: pl.BlockSpec(shape, lambda b, i: (0,) * len(shape))
    big = jax.ShapeDtypeStruct((bsz, seq, M_WIDTH), BF16)
    return pl.pallas_call(
        _mpre_kernel,
        grid=(bsz, seq // tm),
        in_specs=[
            pl.BlockSpec((None, tm, M_WIDTH), lambda b, i: (b, i, 0)),
            pl.BlockSpec((None, CONV_HALO, M_WIDTH),
                         lambda b, i: (b, jnp.maximum(i * halo_blocks - 1, 0), 0)),
            full((M_CONV, M_WIDTH)), full((1, M_WIDTH)),
            full((M_HEADS, M_HEAD_DIM, M_HEAD_DIM)), full((M_HEADS, M_HEAD_DIM, M_HEAD_DIM)),
            full((M_HEADS, M_HEAD_DIM, M_HEAD_DIM)), full((M_HEADS, M_HEAD_DIM, M_HEAD_DIM)),
            full((3, M_WIDTH, LANES)), full((1, LANES)),
        ],
        out_specs=(
            pl.BlockSpec((None, tm, M_WIDTH), lambda b, i: (b, i, 0)),
            pl.BlockSpec((None, M_HEADS, M_HEAD_DIM, tm), lambda b, i: (b, 0, 0, i)),
            pl.BlockSpec((None, tm, M_WIDTH), lambda b, i: (b, i, 0)),
            pl.BlockSpec((None, tm, M_WIDTH), lambda b, i: (b, i, 0)),
            pl.BlockSpec((None, tm, LANES), lambda b, i: (b, i, 0)),
        ),
        out_shape=(big, jax.ShapeDtypeStruct((bsz, M_HEADS, M_HEAD_DIM, seq), BF16), big, big,
                   jax.ShapeDtypeStruct((bsz, seq, LANES), F32)),
        scratch_shapes=[pltpu.VMEM((tm + CONV_HALO, M_WIDTH), F32)],
        compiler_params=_params("arbitrary", "arbitrary"),
        name="mpre",
    )(rest3, rest3, conv_w, conv_b, wq, wk, wkt, wv, wif3, bif)


def _mlstm_kernel(q_ref, kt_ref, v_ref, g_ref, om_ref, zm_ref, xc_ref, nw_ref, sk_ref,
                  y_ref, c_scr, m_scr):
    @pl.when(pl.program_id(1) == 0)
    def _():
        c_scr[...] = jnp.zeros_like(c_scr)
        m_scr[...] = jnp.zeros_like(m_scr)

    L, E = M_CHUNK, M_HEAD_DIM
    row = lax.broadcasted_iota(jnp.int32, (L, L), 0)
    col = lax.broadcasted_iota(jnp.int32, (L, L), 1)
    causal = col <= row
    tri = causal.astype(F32)
    ones_blk = jnp.ones((L, LANES), BF16)

    def chunk(c):
        rows = pl.ds(c * L, L)
        heads = range(M_HEADS)
        hsl = [slice(h * E, (h + 1) * E) for h in heads]
        g = g_ref[rows, :]
        bm = jnp.dot(tri, g, precision=lax.Precision.HIGHEST,
                     preferred_element_type=F32)
        g_t = g.T
        bm_t = bm.T

        b_col = [bm[:, M_HEADS + h:M_HEADS + h + 1] for h in heads]
        li_row = [g_t[h:h + 1, :] for h in heads]
        b_row = [bm_t[M_HEADS + h:M_HEADS + h + 1, :] for h in heads]
        b_last = [b_col[h][L - 1:L, :] for h in heads]
        dmat = [jnp.where(causal, b_col[h] - b_row[h] + li_row[h], NEG) for h in heads]
        m_intra = [jnp.max(dmat[h], axis=-1, keepdims=True) for h in heads]
        e_intra = [jnp.exp(dmat[h] - m_intra[h]) for h in heads]
        w_src = [b_last[h] - b_row[h] + li_row[h] for h in heads]
        a = [jnp.max(w_src[h], axis=1, keepdims=True) for h in heads]
        m_prev = [m_scr[h:h + 1, 0:1] for h in heads]
        m_new = [jnp.maximum(b_last[h] + m_prev[h], a[h]) for h in heads]
        decay = [jnp.exp(b_last[h] + m_prev[h] - m_new[h]) for h in heads]
        e_key = [jnp.exp(w_src[h] - m_new[h]).astype(BF16) for h in heads]
        g_col = [b_col[h] + m_prev[h] for h in heads]
        m_t = [jnp.maximum(g_col[h], m_intra[h]) for h in heads]
        w_intra = [jnp.exp(m_intra[h] - m_t[h]) for h in heads]
        w_state = [jnp.exp(g_col[h] - m_t[h]) for h in heads]
        floor = [jnp.exp(-m_t[h]) for h in heads]

        q = [q_ref[rows, hsl[h]] for h in heads]
        kt = [kt_ref[h, :, c * L:(c + 1) * L] for h in heads]
        v_ext = [jnp.concatenate([v_ref[rows, hsl[h]], ones_blk], axis=1) for h in heads]
        qk = [jnp.dot(q[h], kt[h], preferred_element_type=F32) for h in heads]
        c_prev = [c_scr[h] for h in heads]
        y_state = [jnp.dot(q[h], c_prev[h].astype(BF16), preferred_element_type=F32) for h in heads]
        pmat = [(qk[h] * e_intra[h]).astype(BF16) for h in heads]
        y_intra = [jnp.dot(pmat[h], v_ext[h], preferred_element_type=F32) for h in heads]
        c_loc = [jnp.dot(kt[h] * e_key[h], v_ext[h], preferred_element_type=F32) for h in heads]

        y_all = [w_intra[h] * y_intra[h] + w_state[h] * y_state[h] for h in heads]
        for h in heads:
            c_scr[h] = decay[h] * c_prev[h] + c_loc[h]
            m_scr[h:h + 1, :] = jnp.broadcast_to(m_new[h], (1, LANES))
        inv_den = [1.0 / jnp.maximum(jnp.abs(y_all[h][:, E:]), floor[h]) for h in heads]
        hh = [jnp.concatenate([y_all[h][:, j * LANES:(j + 1) * LANES] * inv_den[h]
                               for j in range(E // LANES)], axis=1) for h in heads]

        hm = [_sigmoid(om_ref[rows, hsl[h]]).astype(F32) * hh[h] for h in heads]
        mu = [jnp.mean(hm[h], axis=-1, keepdims=True) for h in heads]
        hc = [hm[h] - mu[h] for h in heads]
        var = [jnp.mean(hc[h] * hc[h], axis=-1, keepdims=True) for h in heads]
        for h in heads:
            hn = hc[h] * lax.rsqrt(var[h] + LN_EPS) * nw_ref[:, hsl[h]]
            y_ref[rows, hsl[h]] = ((hn + sk_ref[:, hsl[h]] * xc_ref[rows, hsl[h]].astype(F32)).astype(BF16)
                                   * _silu(zm_ref[rows, hsl[h]]))

    for c in range(q_ref.shape[0] // L):
        chunk(c)


def _mlstm(qm, kmt, vm, gates, rest3, xc, norm_w, skip):
    bsz, seq, _ = qm.shape
    tm = MLSTM_TM
    row_spec = pl.BlockSpec((None, tm, M_WIDTH), lambda b, i: (b, i, 0))
    return pl.pallas_call(
        _mlstm_kernel,
        grid=(bsz, seq // tm),
        in_specs=[
            row_spec,
            pl.BlockSpec((None, M_HEADS, M_HEAD_DIM, tm), lambda b, i: (b, 0, 0, i)),
            row_spec,
            pl.BlockSpec((None, tm, LANES), lambda b, i: (b, i, 0)),
            pl.BlockSpec((None, tm, M_WIDTH), lambda b, i: (b, i, 2)),
            pl.BlockSpec((None, tm, M_WIDTH), lambda b, i: (b, i, 1)),
            row_spec,
            pl.BlockSpec((1, M_WIDTH), lambda b, i: (0, 0)),
            pl.BlockSpec((1, M_WIDTH), lambda b, i: (0, 0)),
        ],
        out_specs=row_spec,
        out_shape=jax.ShapeDtypeStruct((bsz, seq, M_WIDTH), BF16),
        scratch_shapes=[pltpu.VMEM((M_HEADS, M_HEAD_DIM, M_HEAD_DIM + LANES), F32),
                        pltpu.VMEM((8, LANES), F32)],
        compiler_params=_params("arbitrary", "arbitrary"),
        name="mlstm",
    )(qm, kmt, vm, gates, rest3, rest3, xc, norm_w, skip)


def _post_kernel(o0_ref, o1_ref, o2_ref, s0_ref, s1_ref, s2_ref, za_ref, ym_ref, ga_ref, gm_ref,
                 x_ref, ada_ref, wpa_ref, wpm_ref, wout_ref, lng_ref, lnb_ref, out_ref,
                 *, alpha, d_model):
    hd = ATT_HEAD_DIM
    tm = x_ref.shape[0]
    gate = ada_ref[:, 2 * d_model:3 * d_model]
    for part in range(POST_PARTS):
        rs = slice(part * (tm // POST_PARTS), (part + 1) * (tm // POST_PARTS))
        parts = []
        for h in range(ATT_KV_HEADS):
            sl = slice(h * hd, (h + 1) * hd)
            l0 = s0_ref[rs, h:h + 1]
            l1 = s1_ref[rs, h:h + 1]
            l2 = s2_ref[rs, h:h + 1]
            mx = jnp.maximum(jnp.maximum(l0, l1), l2)
            e0 = jnp.exp(l0 - mx)
            e1 = jnp.exp(l1 - mx)
            e2 = jnp.exp(l2 - mx)
            tot = e0 + e1 + e2
            o = ((e0 / tot) * o0_ref[rs, sl].astype(F32) + (e1 / tot) * o1_ref[rs, sl].astype(F32)
                 + (e2 / tot) * o2_ref[rs, sl].astype(F32))
            parts.append(o.astype(BF16) * _silu(za_ref[rs, sl]))
        att_in = jnp.concatenate(parts, axis=1)
        y_att = jnp.dot(att_in, wpa_ref[...], preferred_element_type=F32)
        y_m = jnp.dot(ym_ref[rs, :], wpm_ref[...], preferred_element_type=F32)
        merged = (_sigmoid(ga_ref[rs, :]) * y_att.astype(BF16)
                  + _sigmoid(gm_ref[rs, :]) * y_m.astype(BF16))
        out = jnp.dot(merged, wout_ref[...], preferred_element_type=F32)
        res = alpha * x_ref[rs, :] + gate * out
        mu = jnp.mean(res, axis=-1, keepdims=True)
        rc = res - mu
        var = jnp.mean(rc * rc, axis=-1, keepdims=True)
        out_ref[rs, :] = (rc * lax.rsqrt(var + LN_EPS) * lng_ref[...] + lnb_ref[...]).astype(out_ref.dtype)


def _post(o_list, st_list, za2, rest2, ymin2, x2d, ada3, wpa, wpm, wout, ln_g, ln_b, seq, alpha):
    rows, d = x2d.shape
    tm = POST_TM
    tiles_per_seq = seq // tm
    kw = ATT_KV_WIDTH
    o_spec = pl.BlockSpec((tm, kw), lambda i: (i, 0))
    s_spec = pl.BlockSpec((tm, LANES), lambda i: (i, 0))
    const = lambda shape: pl.BlockSpec(shape, lambda i: (0,) * len(shape))
    return pl.pallas_call(
        functools.partial(_post_kernel, alpha=alpha, d_model=d),
        grid=(rows // tm,),
        in_specs=[
            o_spec, o_spec, o_spec, s_spec, s_spec, s_spec,
            pl.BlockSpec((tm, kw), lambda i: (i, 0)),
            pl.BlockSpec((tm, M_WIDTH), lambda i: (i, 0)),
            pl.BlockSpec((tm, d), lambda i: (i, 3)),
            pl.BlockSpec((tm, d), lambda i: (i, 4)),
            pl.BlockSpec((tm, d), lambda i: (i, 0)),
            pl.BlockSpec((None, 1, 3 * d), lambda i: (i // tiles_per_seq, 0, 0)),
            const((kw, d)), const((M_WIDTH, d)), const((d, d)),
            const((1, d)), const((1, d)),
        ],
        out_specs=pl.BlockSpec((tm, d), lambda i: (i, 0)),
        out_shape=jax.ShapeDtypeStruct((rows, d), x2d.dtype),
        compiler_params=_params("arbitrary"),
        name="post",
    )(*o_list, *st_list, za2, ymin2, rest2, rest2, x2d, ada3, wpa, wpm, wout, ln_g, ln_b)


def _layer(x, c, positions, w_ada, b_ada, w_in, conv_w, conv_b, w_qm, w_km, w_vm,
           w_if, b_if, mh_norm_w, skip_m, w_pa, w_pm, w_out, ln_g, ln_b, alpha):
    bsz, seq, d = x.shape
    rows = bsz * seq
    x2d = x.reshape(rows, d)

    ada3 = _ada(c, w_ada, b_ada).reshape(bsz, 1, 3 * d)

    half = ATT_HEAD_DIM // 2
    inv = jnp.power(ROPE_THETA, -jnp.arange(half, dtype=F32) / half)
    sign = jnp.concatenate([-jnp.ones((half,), F32), jnp.ones((half,), F32)])
    rope_tab = jnp.zeros((8, LANES), F32).at[0].set(jnp.concatenate([inv, inv])).at[1].set(sign)

    w_in_b = w_in.astype(BF16)
    q0, q1r, q2r, kv, kv4r, kv16r, za, h2d = _inproj(x2d, ada3, positions.reshape(rows, 1), rope_tab,
                                                     w_in_b, bsz, seq)
    rest = _restproj(h2d, w_in_b, 6 * ATT_KV_WIDTH)

    n_sub = seq // ATT_BLOCK
    o_list, st_list = [], []
    for q_g, kv_g, (_, dil) in zip((q0, q1r, q2r), (kv, kv4r, kv16r), ATT_GROUPS):
        kv_sub = kv_g.reshape(bsz, n_sub, ATT_BLOCK, 2 * ATT_KV_WIDTH)
        o_g, st_g = _attention_group(q_g.reshape(bsz, n_sub, ATT_BLOCK, ATT_KV_WIDTH), kv_sub, kv_sub,
                                     bsz, seq, dil)
        o_list.append(o_g.reshape(rows, ATT_KV_WIDTH))
        st_list.append(st_g.reshape(rows, LANES))

    rest3 = rest.reshape(bsz, seq, rest.shape[1])
    n_gate = 2 * M_HEADS
    wif3 = jnp.zeros((3, M_WIDTH, LANES), BF16).at[:, :, :n_gate].set(
        w_if.astype(BF16).reshape(3, M_WIDTH, n_gate))
    bif = jnp.zeros((1, LANES), F32).at[0, :n_gate].set(b_if.astype(F32))
    wk = (w_km * (M_HEAD_DIM ** -0.5)).astype(BF16)
    qm, kmt, vm, xc, gates = _mpre(rest3, conv_w.astype(F32), conv_b.astype(F32).reshape(1, M_WIDTH),
                                   w_qm.astype(BF16), wk, wk.transpose(0, 2, 1), w_vm.astype(BF16), wif3, bif)
    ymin = _mlstm(qm, kmt, vm, gates, rest3, xc,
                  mh_norm_w.astype(F32).reshape(1, M_WIDTH), skip_m.astype(F32).reshape(1, M_WIDTH))

    out = _post(o_list, st_list, za, rest, ymin.reshape(rows, M_WIDTH), x2d, ada3,
                w_pa.astype(BF16), w_pm.astype(BF16), w_out.astype(BF16),
                ln_g.astype(F32).reshape(1, d), ln_b.astype(F32).reshape(1, d), seq, alpha)
    return out.reshape(bsz, seq, d)


def kernel(x, c, positions, w_ada, b_ada, w_in, conv_w, conv_b, w_qm, w_km, w_vm, w_if, b_if,
           mh_norm_w, skip_m, w_pa, w_pm, w_out, ln_g, ln_b):
    depth = w_ada.shape[0]
    alpha = (2.0 * depth) ** 0.25
    for l in range(depth):
        x = _layer(x, c, positions, w_ada[l], b_ada[l], w_in[l], conv_w[l], conv_b[l],
                   w_qm[l], w_km[l], w_vm[l], w_if[l], b_if[l], mh_norm_w[l], skip_m[l],
                   w_pa[l], w_pm[l], w_out[l], ln_g[l], ln_b[l], alpha)
    return x
```

```python
import functools

import jax
import jax.numpy as jnp
from jax import lax
from jax.experimental import pallas as pl
from jax.experimental.pallas import tpu as pltpu

F32 = jnp.float32
BF16 = jnp.bfloat16

ATT_HEAD_DIM = 128
ATT_GROUPS = ((128, 1), (512, 4), (2048, 16))
ATT_KV_HEADS = 4
ATT_KV_WIDTH = ATT_KV_HEADS * ATT_HEAD_DIM
ATT_BLOCK = 128
ROPE_THETA = 10000.0
M_HEADS = 4
M_HEAD_DIM = 256
M_WIDTH = M_HEADS * M_HEAD_DIM
M_CONV = 4
M_CHUNK = 128
LN_EPS = 1e-5
NEG = -1e30

V7X_VMEM_LIMIT_BYTES = 56 * 1024 * 1024
LANES = 128

INPROJ_TM = 1024
INPROJ_TN = 1024
REST_TM = 2048
REST_TN = 1024
ATT_SUB = 16
ATT_BATCH = 4
ATT_SPAN_SUBS = 16
MPRE_TM = 1024
MLSTM_TM = 512
POST_TM = 1024
POST_PARTS = 4
CONV_HALO = 16


def _sigmoid(v):
    return 0.5 * jnp.tanh(0.5 * v) + 0.5


def _silu(v):
    return v * _sigmoid(v)


def _params(*sem):
    return pltpu.CompilerParams(dimension_semantics=sem, vmem_limit_bytes=V7X_VMEM_LIMIT_BYTES)


def _ada_kernel(c_ref, w_ref, b_ref, o_ref):
    sc = _silu(c_ref[...])
    o_ref[...] = jnp.dot(sc, w_ref[...], precision=lax.Precision.HIGHEST,
                         preferred_element_type=F32) + b_ref[...]


def _ada(c, w_ada, b_ada):
    bsz, d = c.shape
    n = w_ada.shape[1]
    return pl.pallas_call(
        _ada_kernel,
        grid=(n // d,),
        in_specs=[pl.BlockSpec((bsz, d), lambda j: (0, 0)),
                  pl.BlockSpec((d, d), lambda j: (0, j)),
                  pl.BlockSpec((1, d), lambda j: (0, j))],
        out_specs=pl.BlockSpec((bsz, d), lambda j: (0, j)),
        out_shape=jax.ShapeDtypeStruct((bsz, n), F32),
        compiler_params=_params("arbitrary"),
        name="ada",
    )(c, w_ada, b_ada.reshape(1, n))


def _inproj_kernel(x_ref, ada_ref, pos_ref, tab_ref, w01_ref, w2z_ref, wkv_ref,
                   q0_ref, q1_ref, q2_ref, kv_ref, kv4_ref, kv16_ref, za_ref, h_ref,
                   cos_scr, sin_scr, stage_scr, stage4_scr, *, d_model):
    tm = x_ref.shape[0]
    hd, kw = ATT_HEAD_DIM, ATT_KV_WIDTH
    dil4, dil16 = ATT_GROUPS[1][1], ATT_GROUPS[2][1]
    outer = dil16 // dil4

    def prepare():
        x = x_ref[...]
        mu = jnp.mean(x, axis=-1, keepdims=True)
        xc = x - mu
        var = jnp.mean(xc * xc, axis=-1, keepdims=True)
        xn = xc * lax.rsqrt(var + LN_EPS)
        shift = ada_ref[:, 0:d_model]
        scale = ada_ref[:, d_model:2 * d_model]
        h_ref[...] = (xn * (1.0 + scale) + shift).astype(BF16)
        half = tm // 2
        lo_lane = lax.broadcasted_iota(jnp.int32, (half, LANES), 1) < hd // 2
        pos2 = jnp.where(lo_lane, pos_ref[0:half, :], pos_ref[half:tm, :])
        ang = pos2.astype(F32) * tab_ref[0:1, :]
        for fn, dst in ((jnp.cos, cos_scr), (jnp.sin, sin_scr)):
            val = fn(ang)
            swapped = pltpu.roll(val, hd // 2, axis=1)
            dst[0:half, :] = jnp.where(lo_lane, val, swapped)
            dst[half:tm, :] = jnp.where(lo_lane, swapped, val)
        sin_scr[...] = sin_scr[...] * tab_ref[1:2, :]

    def project(w_ref):
        return jnp.dot(h_ref[...], w_ref[...], preferred_element_type=F32)

    def heads(acc, rope, scale):
        for hh in range(ATT_KV_HEADS):
            xh = acc[:, hh * hd:(hh + 1) * hd]
            if rope:
                rot = pltpu.roll(xh, hd // 2, axis=1)
                xh = xh * cos_scr[...] + rot * sin_scr[...]
            if scale is not None:
                xh = xh * scale
            yield hh, xh

    def stage_heads(s, acc, rope, scale, nat_ref=None, col0=0):
        for hh, val in heads(acc, rope, scale):
            stage_scr[s, hh] = val
            if nat_ref is not None:
                nat_ref[:, col0 + hh * hd:col0 + (hh + 1) * hd] = val.astype(nat_ref.dtype)
        for hh in range(ATT_KV_HEADS):
            for r in range(dil4):
                stage4_scr[s, hh * dil4 + r] = stage_scr[s, hh, pl.ds(r, tm // dil4, stride=dil4), :]

    def emit_d4(s, dst_ref, col0=0):
        for hh in range(ATT_KV_HEADS):
            for r in range(dil4):
                for nn in range(tm // (ATT_BLOCK * dil4)):
                    rows = stage4_scr[s, hh * dil4 + r, nn * ATT_BLOCK:(nn + 1) * ATT_BLOCK, :]
                    dst_ref[nn, r, :, col0 + hh * hd:col0 + (hh + 1) * hd] = rows.astype(dst_ref.dtype)

    def emit_d16(s, dst_ref, col0=0):
        for hh in range(ATT_KV_HEADS):
            for r in range(dil4):
                for a in range(outer):
                    rows = stage4_scr[s, hh * dil4 + r, pl.ds(a, tm // dil16, stride=outer), :]
                    dst_ref[a * dil4 + r, :, col0 + hh * hd:col0 + (hh + 1) * hd] = rows.astype(dst_ref.dtype)

    q_scale = hd ** -0.5

    def tile_q01():
        acc = project(w01_ref)
        for hh, val in heads(acc[:, :kw], True, q_scale):
            q0_ref[:, hh * hd:(hh + 1) * hd] = val.astype(q0_ref.dtype)
        stage_heads(0, acc[:, kw:], True, q_scale)
        emit_d4(0, q1_ref)

    def tile_q2z():
        acc = project(w2z_ref)
        stage_heads(0, acc[:, :kw], True, q_scale)
        emit_d16(0, q2_ref)
        za_ref[...] = acc[:, kw:].astype(za_ref.dtype)

    def tile_kv():
        acc = project(wkv_ref)
        stage_heads(0, acc[:, :kw], True, None, kv_ref, 0)
        emit_d4(0, kv4_ref, 0)
        emit_d16(0, kv16_ref, 0)
        stage_heads(1, acc[:, kw:], False, None, kv_ref, kw)
        emit_d4(1, kv4_ref, kw)
        emit_d16(1, kv16_ref, kw)

    prepare()
    tile_q01()
    tile_q2z()
    tile_kv()


def _inproj(x2d, ada3, pos2d, rope_tab, w_in_bf16, bsz, seq):
    rows, d = x2d.shape
    tm, tn = INPROJ_TM, INPROJ_TN
    kw = ATT_KV_WIDTH
    w01 = w_in_bf16[:, 0:2 * kw]
    w2z = jnp.concatenate([w_in_bf16[:, 2 * kw:3 * kw], w_in_bf16[:, 5 * kw:6 * kw]], axis=1)
    wkv = w_in_bf16[:, 3 * kw:5 * kw]
    tps = seq // tm
    dil4, dil16 = ATT_GROUPS[1][1], ATT_GROUPS[2][1]
    blk4, blk16 = ATT_BLOCK * dil4, ATT_BLOCK * dil16
    assert tm % blk4 == 0 and blk16 % tm == 0 and tn == 2 * kw and dil16 % dil4 == 0
    t16 = blk16 // tm

    out_shapes = (
        jax.ShapeDtypeStruct((rows, kw), BF16),
        jax.ShapeDtypeStruct((bsz, seq // blk4, dil4, ATT_BLOCK, kw), BF16),
        jax.ShapeDtypeStruct((bsz, seq // blk16, dil16, ATT_BLOCK, kw), BF16),
        jax.ShapeDtypeStruct((rows, 2 * kw), BF16),
        jax.ShapeDtypeStruct((bsz, seq // blk4, dil4, ATT_BLOCK, 2 * kw), BF16),
        jax.ShapeDtypeStruct((bsz, seq // blk16, dil16, ATT_BLOCK, 2 * kw), BF16),
        jax.ShapeDtypeStruct((rows, kw), BF16),
        jax.ShapeDtypeStruct((rows, d), BF16),
    )
    r4_map = lambda i: (i // tps, i % tps, 0, 0, 0)
    r16_map = lambda i: (i // tps, (i % tps) // t16, 0, (i % tps) % t16, 0)
    r4_spec = lambda w: pl.BlockSpec((None, tm // blk4, dil4, ATT_BLOCK, w), r4_map)
    r16_spec = lambda w: pl.BlockSpec((None, None, dil16, tm // dil16, w), r16_map)
    w_spec = pl.BlockSpec((d, tn), lambda i: (0, 0))
    return pl.pallas_call(
        functools.partial(_inproj_kernel, d_model=d),
        grid=(rows // tm,),
        in_specs=[
            pl.BlockSpec((tm, d), lambda i: (i, 0)),
            pl.BlockSpec((None, 1, 3 * d), lambda i: (i // tps, 0, 0)),
            pl.BlockSpec((tm, 1), lambda i: (i, 0)),
            pl.BlockSpec((8, LANES), lambda i: (0, 0)),
            w_spec, w_spec, w_spec,
        ],
        out_specs=(
            pl.BlockSpec((tm, kw), lambda i: (i, 0)),
            r4_spec(kw),
            r16_spec(kw),
            pl.BlockSpec((tm, 2 * kw), lambda i: (i, 0)),
            r4_spec(2 * kw),
            r16_spec(2 * kw),
            pl.BlockSpec((tm, kw), lambda i: (i, 0)),
            pl.BlockSpec((tm, d), lambda i: (i, 0)),
        ),
        out_shape=out_shapes,
        scratch_shapes=[pltpu.VMEM((tm, LANES), F32),
                        pltpu.VMEM((tm, LANES), F32),
                        pltpu.VMEM((2, ATT_KV_HEADS, tm, ATT_HEAD_DIM), F32),
                        pltpu.VMEM((2, ATT_KV_HEADS * dil4, tm // dil4, ATT_HEAD_DIM), F32)],
        compiler_params=_params("arbitrary"),
        name="inproj",
    )(x2d, ada3, pos2d, rope_tab, w01, w2z, wkv)


def _restproj_kernel(h_ref, w_ref, o_ref):
    o_ref[...] = jnp.dot(h_ref[...], w_ref[...], preferred_element_type=F32).astype(o_ref.dtype)


def _restproj(h2d, w_in_bf16, col0):
    rows, d = h2d.shape
    tm, tn = REST_TM, REST_TN
    n_cols = w_in_bf16.shape[1] - col0
    assert col0 % tn == 0 and n_cols % tn == 0
    return pl.pallas_call(
        _restproj_kernel,
        grid=(rows // tm, n_cols // tn),
        in_specs=[pl.BlockSpec((tm, d), lambda i, j: (i, 0)),
                  pl.BlockSpec((d, tn), lambda i, j: (0, col0 // tn + j))],
        out_specs=pl.BlockSpec((tm, tn), lambda i, j: (i, j)),
        out_shape=jax.ShapeDtypeStruct((rows, n_cols), BF16),
        compiler_params=_params("arbitrary", "arbitrary"),
        name="restproj",
    )(h2d, w_in_bf16)


def _attn_kernel(q_ref, kv_ref, kvp_ref, o_ref, st_ref, onat_scr, *, dil):
    span = pl.program_id(1)
    g = pl.program_id(2)
    hd, kw = ATT_HEAD_DIM, ATT_KV_WIDTH
    groups_per_span = ATT_SPAN_SUBS // ATT_SUB
    first = (span * ATT_SPAN_SUBS + g * ATT_SUB - dil) < 0
    thr_first = jnp.where(first, 2 * ATT_BLOCK, 0).astype(jnp.int32)

    row = lax.broadcasted_iota(jnp.int32, (ATT_BLOCK, 2 * ATT_BLOCK), 0)
    col = lax.broadcasted_iota(jnp.int32, (ATT_BLOCK, 2 * ATT_BLOCK), 1)
    in_prev = col < ATT_BLOCK
    rel = col - row

    def mask_bias(thr):
        slack = jnp.where(in_prev, rel - thr, ATT_BLOCK - rel)
        return jnp.where(slack >= 0, 0.0, NEG).astype(F32)

    bias_inner = mask_bias(jnp.int32(0))
    bias_edge = mask_bias(thr_first)
    ones = jnp.ones((2 * ATT_BLOCK, hd), BF16)
    lane = lax.broadcasted_iota(jnp.int32, (ATT_BLOCK, LANES), 1)
    nt = (((1,), (1,)), ((), ()))

    def dst_rows(u):
        if dil <= ATT_SUB:
            start = (g * (ATT_SUB // dil) + u // dil) * (ATT_BLOCK * dil) + u % dil
        else:
            start = g * ATT_SUB + u
        if dil == 1:
            return pl.ds(pl.multiple_of(start, ATT_BLOCK), ATT_BLOCK)
        return pl.ds(start, ATT_BLOCK, stride=dil)

    for u0 in range(0, ATT_SUB, ATT_BATCH):
        units = [(u, h) for u in range(u0, u0 + ATT_BATCH) for h in range(ATT_KV_HEADS)]
        scores, vexts = {}, {}
        for u, h in units:
            ksl = slice(h * hd, (h + 1) * hd)
            vsl = slice(kw + h * hd, kw + (h + 1) * hd)
            q = q_ref[u, :, ksl]
            if u >= dil:
                k_prev, v_prev, bias = kv_ref[u - dil, :, ksl], kv_ref[u - dil, :, vsl], bias_inner
            else:
                k_prev, v_prev, bias = kvp_ref[u, :, ksl], kvp_ref[u, :, vsl], bias_edge
            k_cat = jnp.concatenate([k_prev, kv_ref[u, :, ksl]], axis=0)
            v_cat = jnp.concatenate([v_prev, kv_ref[u, :, vsl]], axis=0)
            vexts[u, h] = jnp.concatenate([v_cat, ones], axis=1)
            scores[u, h] = lax.dot_general(q, k_cat, nt, preferred_element_type=F32) + bias
        maxes = {un: jnp.max(scores[un], axis=-1, keepdims=True) for un in units}
        probs = {un: jnp.exp(scores[un] - maxes[un]).astype(BF16) for un in units}
        outs = {un: jnp.dot(probs[un], vexts[un], preferred_element_type=F32) for un in units}
        for u in range(u0, u0 + ATT_BATCH):
            dst = dst_rows(u)
            stats = jnp.zeros((ATT_BLOCK, LANES), F32)
            for h in range(ATT_KV_HEADS):
                den = outs[u, h][:, hd:]
                onat_scr[h, dst, :] = outs[u, h][:, :hd] / den
                stats = jnp.where(lane == h, maxes[u, h] + jnp.log(den[:, 0:1]), stats)
            st_ref[dst, :] = stats

    @pl.when(g == groups_per_span - 1)
    def _():
        for h in range(ATT_KV_HEADS):
            o_ref[:, h * hd:(h + 1) * hd] = onat_scr[h].astype(o_ref.dtype)


def _attention_group(q_sub, kv_sub, kv_prev, bsz, seq, dil):
    kw = ATT_KV_WIDTH
    span_rows = ATT_SPAN_SUBS * ATT_BLOCK
    gps = ATT_SPAN_SUBS // ATT_SUB
    n_prev = min(dil, ATT_SUB)
    assert ATT_SPAN_SUBS % ATT_SUB == 0 and ATT_SUB % ATT_BATCH == 0
    assert (ATT_SUB % dil == 0) if dil <= ATT_SUB else (dil == ATT_SPAN_SUBS and dil % ATT_SUB == 0)
    own_map = lambda b, s, g: (b, s * gps + g, 0, 0)
    prev_map = lambda b, s, g: (b, jnp.maximum(s * (ATT_SPAN_SUBS // n_prev) + g * (ATT_SUB // n_prev)
                                               - dil // n_prev, 0), 0, 0)
    prev_spec = pl.BlockSpec((None, n_prev, ATT_BLOCK, 2 * kw), prev_map)
    return pl.pallas_call(
        functools.partial(_attn_kernel, dil=dil),
        grid=(bsz, seq // span_rows, gps),
        in_specs=[
            pl.BlockSpec((None, ATT_SUB, ATT_BLOCK, kw), own_map),
            pl.BlockSpec((None, ATT_SUB, ATT_BLOCK, 2 * kw), own_map),
            prev_spec,
        ],
        out_specs=(
            pl.BlockSpec((None, span_rows, kw), lambda b, s, g: (b, s, 0)),
            pl.BlockSpec((None, span_rows, LANES), lambda b, s, g: (b, s, 0)),
        ),
        out_shape=(jax.ShapeDtypeStruct((bsz, seq, kw), BF16),
                   jax.ShapeDtypeStruct((bsz, seq, LANES), F32)),
        scratch_shapes=[pltpu.VMEM((ATT_KV_HEADS, span_rows, ATT_HEAD_DIM), F32)],
        compiler_params=_params("arbitrary", "arbitrary", "arbitrary"),
        name=f"attn_d{dil}",
    )(q_sub, kv_sub, kv_prev)


def _mpre_kernel(xm_ref, halo_ref, cw_ref, cb_ref, wq_ref, wk_ref, wkt_ref, wv_ref, wif_ref, bif_ref,
                 qm_ref, kt_ref, vm_ref, xc_ref, g_ref, ext_scr):
    i = pl.program_id(1)
    tm = xm_ref.shape[0]
    xm = xm_ref[...]
    halo = halo_ref[...].astype(F32)
    ext_scr[0:CONV_HALO, :] = jnp.where(i > 0, halo, 0.0)
    ext_scr[CONV_HALO:CONV_HALO + tm, :] = xm.astype(F32)
    acc = jnp.zeros((tm, M_WIDTH), F32) + cb_ref[...]
    for k in range(M_CONV):
        start = CONV_HALO - (M_CONV - 1) + k
        acc = acc + cw_ref[k:k + 1, :] * ext_scr[pl.ds(start, tm), :]
    xc = _silu(acc)
    xc_b = xc.astype(BF16)
    xc_ref[...] = xc_b
    gates = jnp.zeros((tm, LANES), F32) + bif_ref[...]
    for h in range(M_HEADS):
        sl = slice(h * M_HEAD_DIM, (h + 1) * M_HEAD_DIM)
        qh = jnp.dot(xc_b[:, sl], wq_ref[h], preferred_element_type=F32).astype(BF16)
        kh = jnp.dot(xc_b[:, sl], wk_ref[h], preferred_element_type=F32).astype(BF16)
        vh = jnp.dot(xm[:, sl], wv_ref[h], preferred_element_type=F32).astype(BF16)
        qm_ref[:, sl] = qh
        vm_ref[:, sl] = vh
        kt_ref[h] = lax.dot_general(wkt_ref[h], xc_b[:, sl], (((1,), (1,)), ((), ())),
                                    preferred_element_type=F32).astype(BF16)
        gates = gates + jnp.dot(qh, wif_ref[0, sl, :], preferred_element_type=F32)
        gates = gates + jnp.dot(kh, wif_ref[1, sl, :], preferred_element_type=F32)
        gates = gates + jnp.dot(vh, wif_ref[2, sl, :], preferred_element_type=F32)
    lane = lax.broadcasted_iota(jnp.int32, gates.shape, 1)
    log_f = jnp.minimum(gates, 0.0) - jnp.log1p(jnp.exp(-jnp.abs(gates)))
    g_ref[...] = jnp.where(lane >= M_HEADS, log_f, gates)


def _mpre(rest3, conv_w, conv_b, wq, wk, wkt, wv, wif3, bif):
    bsz, seq, _ = rest3.shape
    tm = MPRE_TM
    halo_blocks = tm // CONV_HALO
    full = lambda shape: pl.BlockSpec(shape, lambda b, i: (0,) * len(shape))
    big = jax.ShapeDtypeStruct((bsz, seq, M_WIDTH), BF16)
    return pl.pallas_call(
        _mpre_kernel,
        grid=(bsz, seq // tm),
        in_specs=[
            pl.BlockSpec((None, tm, M_WIDTH), lambda b, i: (b, i, 0)),
            pl.BlockSpec((None, CONV_HALO, M_WIDTH),
                         lambda b, i: (b, jnp.maximum(i * halo_blocks - 1, 0), 0)),
            full((M_CONV, M_WIDTH)), full((1, M_WIDTH)),
            full((M_HEADS, M_HEAD_DIM, M_HEAD_DIM)), full((M_HEADS, M_HEAD_DIM, M_HEAD_DIM)),
            full((M_HEADS, M_HEAD_DIM, M_HEAD_DIM)), full((M_HEADS, M_HEAD_DIM, M_HEAD_DIM)),
            full((3, M_WIDTH, LANES)), full((1, LANES)),
        ],
        out_specs=(
            pl.BlockSpec((None, tm, M_WIDTH), lambda b, i: (b, i, 0)),
            pl.BlockSpec((None, M_HEADS, M_HEAD_DIM, tm), lambda b, i: (b, 0, 0, i)),
            pl.BlockSpec((None, tm, M_WIDTH), lambda b, i: (b, i, 0)),
            pl.BlockSpec((None, tm, M_WIDTH), lambda b, i: (b, i, 0)),
            pl.BlockSpec((None, tm, LANES), lambda b, i: (b, i, 0)),
        ),
        out_shape=(big, jax.ShapeDtypeStruct((bsz, M_HEADS, M_HEAD_DIM, seq), BF16), big, big,
                   jax.ShapeDtypeStruct((bsz, seq, LANES), F32)),
        scratch_shapes=[pltpu.VMEM((tm + CONV_HALO, M_WIDTH), F32)],
        compiler_params=_params("arbitrary", "arbitrary"),
        name="mpre",
    )(rest3, rest3, conv_w, conv_b, wq, wk, wkt, wv, wif3, bif)


def _mlstm_kernel(q_ref, kt_ref, v_ref, g_ref, om_ref, zm_ref, xc_ref, nw_ref, sk_ref,
                  y_ref, c_scr, m_scr):
    @pl.when(pl.program_id(1) == 0)
    def _():
        c_scr[...] = jnp.zeros_like(c_scr)
        m_scr[...] = jnp.zeros_like(m_scr)

    L, E = M_CHUNK, M_HEAD_DIM
    row = lax.broadcasted_iota(jnp.int32, (L, L), 0)
    col = lax.broadcasted_iota(jnp.int32, (L, L), 1)
    causal = col <= row
    tri = causal.astype(F32)
    ones_blk = jnp.ones((L, LANES), BF16)

    def chunk(c):
        rows = pl.ds(c * L, L)
        heads = range(M_HEADS)
        hsl = [slice(h * E, (h + 1) * E) for h in heads]
        g = g_ref[rows, :]
        bm = jnp.dot(tri, g, precision=lax.Precision.HIGHEST,
                     preferred_element_type=F32)
        g_t = g.T
        bm_t = bm.T

        b_col = [bm[:, M_HEADS + h:M_HEADS + h + 1] for h in heads]
        li_row = [g_t[h:h + 1, :] for h in heads]
        b_row = [bm_t[M_HEADS + h:M_HEADS + h + 1, :] for h in heads]
        b_last = [b_col[h][L - 1:L, :] for h in heads]
        dmat = [jnp.where(causal, b_col[h] - b_row[h] + li_row[h], NEG) for h in heads]
        m_intra = [jnp.max(dmat[h], axis=-1, keepdims=True) for h in heads]
        e_intra = [jnp.exp(dmat[h] - m_intra[h]) for h in heads]
        w_src = [b_last[h] - b_row[h] + li_row[h] for h in heads]
        a = [jnp.max(w_src[h], axis=1, keepdims=True) for h in heads]
        m_prev = [m_scr[h:h + 1, 0:1] for h in heads]
        m_new = [jnp.maximum(b_last[h] + m_prev[h], a[h]) for h in heads]
        decay = [jnp.exp(b_last[h] + m_prev[h] - m_new[h]) for h in heads]
        e_key = [jnp.exp(w_src[h] - m_new[h]).astype(BF16) for h in heads]
        g_col = [b_col[h] + m_prev[h] for h in heads]
        m_t = [jnp.maximum(g_col[h], m_intra[h]) for h in heads]
        w_intra = [jnp.exp(m_intra[h] - m_t[h]) for h in heads]
        w_state = [jnp.exp(g_col[h] - m_t[h]) for h in heads]
        floor = [jnp.exp(-m_t[h]) for h in heads]

        q = [q_ref[rows, hsl[h]] for h in heads]
        kt = [kt_ref[h, :, c * L:(c + 1) * L] for h in heads]
        v_ext = [jnp.concatenate([v_ref[rows, hsl[h]], ones_blk], axis=1) for h in heads]
        qk = [jnp.dot(q[h], kt[h], preferred_element_type=F32) for h in heads]
        c_prev = [c_scr[h] for h in heads]
        y_state = [jnp.dot(q[h], c_prev[h].astype(BF16), preferred_element_type=F32) for h in heads]
        pmat = [(qk[h] * e_intra[h]).astype(BF16) for h in heads]
        y_intra = [jnp.dot(pmat[h], v_ext[h], preferred_element_type=F32) for h in heads]
        c_loc = [jnp.dot(kt[h] * e_key[h], v_ext[h], preferred_element_type=F32) for h in heads]

        y_all = [w_intra[h] * y_intra[h] + w_state[h] * y_state[h] for h in heads]
        for h in heads:
            c_scr[h] = decay[h] * c_prev[h] + c_loc[h]
            m_scr[h:h + 1, :] = jnp.broadcast_to(m_new[h], (1, LANES))
        inv_den = [1.0 / jnp.maximum(jnp.abs(y_all[h][:, E:]), floor[h]) for h in heads]
        hh = [jnp.concatenate([y_all[h][:, j * LANES:(j + 1) * LANES] * inv_den[h]
                               for j in range(E // LANES)], axis=1) for h in heads]

        hm = [_sigmoid(om_ref[rows, hsl[h]]).astype(F32) * hh[h] for h in heads]
        mu = [jnp.mean(hm[h], axis=-1, keepdims=True) for h in heads]
        hc = [hm[h] - mu[h] for h in heads]
        var = [jnp.mean(hc[h] * hc[h], axis=-1, keepdims=True) for h in heads]
        for h in heads:
            hn = hc[h] * lax.rsqrt(var[h] + LN_EPS) * nw_ref[:, hsl[h]]
            y_ref[rows, hsl[h]] = ((hn + sk_ref[:, hsl[h]] * xc_ref[rows, hsl[h]].astype(F32)).astype(BF16)
                                   * _silu(zm_ref[rows, hsl[h]]))

    for c in range(q_ref.shape[0] // L):
        chunk(c)


def _mlstm(qm, kmt, vm, gates, rest3, xc, norm_w, skip):
    bsz, seq, _ = qm.shape
    tm = MLSTM_TM
    row_spec = pl.BlockSpec((None, tm, M_WIDTH), lambda b, i: (b, i, 0))
    return pl.pallas_call(
        _mlstm_kernel,
        grid=(bsz, seq // tm),
        in_specs=[
            row_spec,
            pl.BlockSpec((None, M_HEADS, M_HEAD_DIM, tm), lambda b, i: (b, 0, 0, i)),
            row_spec,
            pl.BlockSpec((None, tm, LANES), lambda b, i: (b, i, 0)),
            pl.BlockSpec((None, tm, M_WIDTH), lambda b, i: (b, i, 2)),
            pl.BlockSpec((None, tm, M_WIDTH), lambda b, i: (b, i, 1)),
            row_spec,
            pl.BlockSpec((1, M_WIDTH), lambda b, i: (0, 0)),
            pl.BlockSpec((1, M_WIDTH), lambda b, i: (0, 0)),
        ],
        out_specs=row_spec,
        out_shape=jax.ShapeDtypeStruct((bsz, seq, M_WIDTH), BF16),
        scratch_shapes=[pltpu.VMEM((M_HEADS, M_HEAD_DIM, M_HEAD_DIM + LANES), F32),
                        pltpu.VMEM((8, LANES), F32)],
        compiler_params=_params("arbitrary", "arbitrary"),
        name="mlstm",
    )(qm, kmt, vm, gates, rest3, rest3, xc, norm_w, skip)


def _post_kernel(o0_ref, o1_ref, o2_ref, s0_ref, s1_ref, s2_ref, za_ref, ym_ref, ga_ref, gm_ref,
                 x_ref, ada_ref, wpa_ref, wpm_ref, wout_ref, lng_ref, lnb_ref, out_ref,
                 *, alpha, d_model):
    hd = ATT_HEAD_DIM
    tm = x_ref.shape[0]
    gate = ada_ref[:, 2 * d_model:3 * d_model]
    for part in range(POST_PARTS):
        rs = slice(part * (tm // POST_PARTS), (part + 1) * (tm // POST_PARTS))
        parts = []
        for h in range(ATT_KV_HEADS):
            sl = slice(h * hd, (h + 1) * hd)
            l0 = s0_ref[rs, h:h + 1]
            l1 = s1_ref[rs, h:h + 1]
            l2 = s2_ref[rs, h:h + 1]
            mx = jnp.maximum(jnp.maximum(l0, l1), l2)
            e0 = jnp.exp(l0 - mx)
            e1 = jnp.exp(l1 - mx)
            e2 = jnp.exp(l2 - mx)
            tot = e0 + e1 + e2
            o = ((e0 / tot) * o0_ref[rs, sl].astype(F32) + (e1 / tot) * o1_ref[rs, sl].astype(F32)
                 + (e2 / tot) * o2_ref[rs, sl].astype(F32))
            parts.append(o.astype(BF16) * _silu(za_ref[rs, sl]))
        att_in = jnp.concatenate(parts, axis=1)
        y_att = jnp.dot(att_in, wpa_ref[...], preferred_element_type=F32)
        y_m = jnp.dot(ym_ref[rs, :], wpm_ref[...], preferred_element_type=F32)
        merged = (_sigmoid(ga_ref[rs, :]) * y_att.astype(BF16)
                  + _sigmoid(gm_ref[rs, :]) * y_m.astype(BF16))
        out = jnp.dot(merged, wout_ref[...], preferred_element_type=F32)
        res = alpha * x_ref[rs, :] + gate * out
        mu = jnp.mean(res, axis=-1, keepdims=True)
        rc = res - mu
        var = jnp.mean(rc * rc, axis=-1, keepdims=True)
        out_ref[rs, :] = (rc * lax.rsqrt(var + LN_EPS) * lng_ref[...] + lnb_ref[...]).astype(out_ref.dtype)


def _post(o_list, st_list, za2, rest2, ymin2, x2d, ada3, wpa, wpm, wout, ln_g, ln_b, seq, alpha):
    rows, d = x2d.shape
    tm = POST_TM
    tiles_per_seq = seq // tm
    kw = ATT_KV_WIDTH
    o_spec = pl.BlockSpec((tm, kw), lambda i: (i, 0))
    s_spec = pl.BlockSpec((tm, LANES), lambda i: (i, 0))
    const = lambda shape: pl.BlockSpec(shape, lambda i: (0,) * len(shape))
    return pl.pallas_call(
        functools.partial(_post_kernel, alpha=alpha, d_model=d),
        grid=(rows // tm,),
        in_specs=[
            o_spec, o_spec, o_spec, s_spec, s_spec, s_spec,
            pl.BlockSpec((tm, kw), lambda i: (i, 0)),
            pl.BlockSpec((tm, M_WIDTH), lambda i: (i, 0)),
            pl.BlockSpec((tm, d), lambda i: (i, 3)),
            pl.BlockSpec((tm, d), lambda i: (i, 4)),
            pl.BlockSpec((tm, d), lambda i: (i, 0)),
            pl.BlockSpec((None, 1, 3 * d), lambda i: (i // tiles_per_seq, 0, 0)),
            const((kw, d)), const((M_WIDTH, d)), const((d, d)),
            const((1, d)), const((1, d)),
        ],
        out_specs=pl.BlockSpec((tm, d), lambda i: (i, 0)),
        out_shape=jax.ShapeDtypeStruct((rows, d), x2d.dtype),
        compiler_params=_params("arbitrary"),
        name="post",
    )(*o_list, *st_list, za2, ymin2, rest2, rest2, x2d, ada3, wpa, wpm, wout, ln_g, ln_b)


def _layer(x, c, positions, w_ada, b_ada, w_in, conv_w, conv_b, w_qm, w_km, w_vm,
           w_if, b_if, mh_norm_w, skip_m, w_pa, w_pm, w_out, ln_g, ln_b, alpha):
    bsz, seq, d = x.shape
    rows = bsz * seq
    x2d = x.reshape(rows, d)

    ada3 = _ada(c, w_ada, b_ada).reshape(bsz, 1, 3 * d)

    half = ATT_HEAD_DIM // 2
    inv = jnp.power(ROPE_THETA, -jnp.arange(half, dtype=F32) / half)
    sign = jnp.concatenate([-jnp.ones((half,), F32), jnp.ones((half,), F32)])
    rope_tab = jnp.zeros((8, LANES), F32).at[0].set(jnp.concatenate([inv, inv])).at[1].set(sign)

    w_in_b = w_in.astype(BF16)
    q0, q1r, q2r, kv, kv4r, kv16r, za, h2d = _inproj(x2d, ada3, positions.reshape(rows, 1), rope_tab,
                                                     w_in_b, bsz, seq)
    rest = _restproj(h2d, w_in_b, 6 * ATT_KV_WIDTH)

    n_sub = seq // ATT_BLOCK
    o_list, st_list = [], []
    for q_g, kv_g, (_, dil) in zip((q0, q1r, q2r), (kv, kv4r, kv16r), ATT_GROUPS):
        kv_sub = kv_g.reshape(bsz, n_sub, ATT_BLOCK, 2 * ATT_KV_WIDTH)
        o_g, st_g = _attention_group(q_g.reshape(bsz, n_sub, ATT_BLOCK, ATT_KV_WIDTH), kv_sub, kv_sub,
                                     bsz, seq, dil)
        o_list.append(o_g.reshape(rows, ATT_KV_WIDTH))
        st_list.append(st_g.reshape(rows, LANES))

    rest3 = rest.reshape(bsz, seq, rest.shape[1])
    n_gate = 2 * M_HEADS
    wif3 = jnp.zeros((3, M_WIDTH, LANES), BF16).at[:, :, :n_gate].set(
        w_if.astype(BF16).reshape(3, M_WIDTH, n_gate))
    bif = jnp.zeros((1, LANES), F32).at[0, :n_gate].set(b_if.astype(F32))
    wk = (w_km * (M_HEAD_DIM ** -0.5)).astype(BF16)
    qm, kmt, vm, xc, gates = _mpre(rest3, conv_w.astype(F32), conv_b.astype(F32).reshape(1, M_WIDTH),
                                   w_qm.astype(BF16), wk, wk.transpose(0, 2, 1), w_vm.astype(BF16), wif3, bif)
    ymin = _mlstm(qm, kmt, vm, gates, rest3, xc,
                  mh_norm_w.astype(F32).reshape(1, M_WIDTH), skip_m.astype(F32).reshape(1, M_WIDTH))

    out = _post(o_list, st_list, za, rest, ymin.reshape(rows, M_WIDTH), x2d, ada3,
                w_pa.astype(BF16), w_pm.astype(BF16), w_out.astype(BF16),
                ln_g.astype(F32).reshape(1, d), ln_b.astype(F32).reshape(1, d), seq, alpha)
    return out.reshape(bsz, seq, d)


def kernel(x, c, positions, w_ada, b_ada, w_in, conv_w, conv_b, w_qm, w_km, w_vm, w_if, b_if,
           mh_norm_w, skip_m, w_pa, w_pm, w_out, ln_g, ln_b):
    depth = w_ada.shape[0]
    alpha = (2.0 * depth) ** 0.25
    for l in range(depth):
        x = _layer(x, c, positions, w_ada[l], b_ada[l], w_in[l], conv_w[l], conv_b[l],
                   w_qm[l], w_km[l], w_vm[l], w_if[l], b_if[l], mh_norm_w[l], skip_m[l],
                   w_pa[l], w_pm[l], w_out[l], ln_g[l], ln_b[l], alpha)
    return x
```

```python
import functools

import jax
import jax.numpy as jnp
from jax import lax
from jax.experimental import pallas as pl
from jax.experimental.pallas import tpu as pltpu

F32 = jnp.float32
BF16 = jnp.bfloat16

ATT_HEAD_DIM = 128
ATT_GROUPS = ((128, 1), (512, 4), (2048, 16))
ATT_KV_HEADS = 4
ATT_KV_WIDTH = ATT_KV_HEADS * ATT_HEAD_DIM
ATT_BLOCK = 128
ROPE_THETA = 10000.0
M_HEADS = 4
M_HEAD_DIM = 256
M_WIDTH = M_HEADS * M_HEAD_DIM
M_CONV = 4
M_CHUNK = 128
LN_EPS = 1e-5
NEG = -1e30

V7X_VMEM_LIMIT_BYTES = 56 * 1024 * 1024
LANES = 128

INPROJ_TM = 1024
INPROJ_TN = 1024
REST_TM = 2048
REST_TN = 1024
ATT_SUB = 32
ATT_BATCH = 4
ATT_SPAN_SUBS = 32
MPRE_TM = 1024
MLSTM_TM = 512
POST_TM = 1024
POST_PARTS = 4
CONV_HALO = 16


def _sigmoid(v):
    return 0.5 * jnp.tanh(0.5 * v) + 0.5


def _silu(v):
    return v * _sigmoid(v)


def _params(*sem):
    return pltpu.CompilerParams(dimension_semantics=sem, vmem_limit_bytes=V7X_VMEM_LIMIT_BYTES)


def _ada_kernel(c_ref, w_ref, b_ref, o_ref):
    sc = _silu(c_ref[...])
    o_ref[...] = jnp.dot(sc, w_ref[...], precision=lax.Precision.HIGHEST,
                         preferred_element_type=F32) + b_ref[...]


def _ada(c, w_ada, b_ada):
    bsz, d = c.shape
    n = w_ada.shape[1]
    return pl.pallas_call(
        _ada_kernel,
        grid=(n // d,),
        in_specs=[pl.BlockSpec((bsz, d), lambda j: (0, 0)),
                  pl.BlockSpec((d, d), lambda j: (0, j)),
                  pl.BlockSpec((1, d), lambda j: (0, j))],
        out_specs=pl.BlockSpec((bsz, d), lambda j: (0, j)),
        out_shape=jax.ShapeDtypeStruct((bsz, n), F32),
        compiler_params=_params("arbitrary"),
        name="ada",
    )(c, w_ada, b_ada.reshape(1, n))


def _inproj_kernel(x_ref, ada_ref, pos_ref, tab_ref, w01_ref, w2z_ref, wkv_ref,
                   q0_ref, q1_ref, q2_ref, kv_ref, kv4_ref, kv16_ref, za_ref, h_ref,
                   cos_scr, sin_scr, stage_scr, stage4_scr, *, d_model):
    tm = x_ref.shape[0]
    hd, kw = ATT_HEAD_DIM, ATT_KV_WIDTH
    dil4, dil16 = ATT_GROUPS[1][1], ATT_GROUPS[2][1]
    outer = dil16 // dil4

    def prepare():
        x = x_ref[...]
        mu = jnp.mean(x, axis=-1, keepdims=True)
        xc = x - mu
        var = jnp.mean(xc * xc, axis=-1, keepdims=True)
        xn = xc * lax.rsqrt(var + LN_EPS)
        shift = ada_ref[:, 0:d_model]
        scale = ada_ref[:, d_model:2 * d_model]
        h_ref[...] = (xn * (1.0 + scale) + shift).astype(BF16)
        half = tm // 2
        lo_lane = lax.broadcasted_iota(jnp.int32, (half, LANES), 1) < hd // 2
        pos2 = jnp.where(lo_lane, pos_ref[0:half, :], pos_ref[half:tm, :])
        ang = pos2.astype(F32) * tab_ref[0:1, :]
        for fn, dst in ((jnp.cos, cos_scr), (jnp.sin, sin_scr)):
            val = fn(ang)
            swapped = pltpu.roll(val, hd // 2, axis=1)
            dst[0:half, :] = jnp.where(lo_lane, val, swapped)
            dst[half:tm, :] = jnp.where(lo_lane, swapped, val)
        sin_scr[...] = sin_scr[...] * tab_ref[1:2, :]

    def project(w_ref):
        return jnp.dot(h_ref[...], w_ref[...], preferred_element_type=F32)

    def heads(acc, rope, scale):
        for hh in range(ATT_KV_HEADS):
            xh = acc[:, hh * hd:(hh + 1) * hd]
            if rope:
                rot = pltpu.roll(xh, hd // 2, axis=1)
                xh = xh * cos_scr[...] + rot * sin_scr[...]
            if scale is not None:
                xh = xh * scale
            yield hh, xh

    def stage_heads(s, acc, rope, scale, nat_ref=None, col0=0):
        for hh, val in heads(acc, rope, scale):
            stage_scr[s, hh] = val
            if nat_ref is not None:
                nat_ref[:, col0 + hh * hd:col0 + (hh + 1) * hd] = val.astype(nat_ref.dtype)
        for hh in range(ATT_KV_HEADS):
            for r in range(dil4):
                stage4_scr[s, hh * dil4 + r] = stage_scr[s, hh, pl.ds(r, tm // dil4, stride=dil4), :]

    def emit_d4(s, dst_ref, col0=0):
        for hh in range(ATT_KV_HEADS):
            for r in range(dil4):
                for nn in range(tm // (ATT_BLOCK * dil4)):
                    rows = stage4_scr[s, hh * dil4 + r, nn * ATT_BLOCK:(nn + 1) * ATT_BLOCK, :]
                    dst_ref[nn, r, :, col0 + hh * hd:col0 + (hh + 1) * hd] = rows.astype(dst_ref.dtype)

    def emit_d16(s, dst_ref, col0=0):
        for hh in range(ATT_KV_HEADS):
            for r in range(dil4):
                for a in range(outer):
                    rows = stage4_scr[s, hh * dil4 + r, pl.ds(a, tm // dil16, stride=outer), :]
                    dst_ref[a * dil4 + r, :, col0 + hh * hd:col0 + (hh + 1) * hd] = rows.astype(dst_ref.dtype)

    q_scale = hd ** -0.5

    def tile_q01():
        acc = project(w01_ref)
        for hh, val in heads(acc[:, :kw], True, q_scale):
            q0_ref[:, hh * hd:(hh + 1) * hd] = val.astype(q0_ref.dtype)
        stage_heads(0, acc[:, kw:], True, q_scale)
        emit_d4(0, q1_ref)

    def tile_q2z():
        acc = project(w2z_ref)
        stage_heads(0, acc[:, :kw], True, q_scale)
        emit_d16(0, q2_ref)
        za_ref[...] = acc[:, kw:].astype(za_ref.dtype)

    def tile_kv():
        acc = project(wkv_ref)
        stage_heads(0, acc[:, :kw], True, None, kv_ref, 0)
        emit_d4(0, kv4_ref, 0)
        emit_d16(0, kv16_ref, 0)
        stage_heads(1, acc[:, kw:], False, None, kv_ref, kw)
        emit_d4(1, kv4_ref, kw)
        emit_d16(1, kv16_ref, kw)

    prepare()
    tile_q01()
    tile_q2z()
    tile_kv()


def _inproj(x2d, ada3, pos2d, rope_tab, w_in_bf16, bsz, seq):
    rows, d = x2d.shape
    tm, tn = INPROJ_TM, INPROJ_TN
    kw = ATT_KV_WIDTH
    w01 = w_in_bf16[:, 0:2 * kw]
    w2z = jnp.concatenate([w_in_bf16[:, 2 * kw:3 * kw], w_in_bf16[:, 5 * kw:6 * kw]], axis=1)
    wkv = w_in_bf16[:, 3 * kw:5 * kw]
    tps = seq // tm
    dil4, dil16 = ATT_GROUPS[1][1], ATT_GROUPS[2][1]
    blk4, blk16 = ATT_BLOCK * dil4, ATT_BLOCK * dil16
    assert tm % blk4 == 0 and blk16 % tm == 0 and tn == 2 * kw and dil16 % dil4 == 0
    t16 = blk16 // tm

    out_shapes = (
        jax.ShapeDtypeStruct((rows, kw), BF16),
        jax.ShapeDtypeStruct((bsz, seq // blk4, dil4, ATT_BLOCK, kw), BF16),
        jax.ShapeDtypeStruct((bsz, seq // blk16, dil16, ATT_BLOCK, kw), BF16),
        jax.ShapeDtypeStruct((rows, 2 * kw), BF16),
        jax.ShapeDtypeStruct((bsz, seq // blk4, dil4, ATT_BLOCK, 2 * kw), BF16),
        jax.ShapeDtypeStruct((bsz, seq // blk16, dil16, ATT_BLOCK, 2 * kw), BF16),
        jax.ShapeDtypeStruct((rows, kw), BF16),
        jax.ShapeDtypeStruct((rows, d), BF16),
    )
    r4_map = lambda i: (i // tps, i % tps, 0, 0, 0)
    r16_map = lambda i: (i // tps, (i % tps) // t16, 0, (i % tps) % t16, 0)
    r4_spec = lambda w: pl.BlockSpec((None, tm // blk4, dil4, ATT_BLOCK, w), r4_map)
    r16_spec = lambda w: pl.BlockSpec((None, None, dil16, tm // dil16, w), r16_map)
    w_spec = pl.BlockSpec((d, tn), lambda i: (0, 0))
    return pl.pallas_call(
        functools.partial(_inproj_kernel, d_model=d),
        grid=(rows // tm,),
        in_specs=[
            pl.BlockSpec((tm, d), lambda i: (i, 0)),
            pl.BlockSpec((None, 1, 3 * d), lambda i: (i // tps, 0, 0)),
            pl.BlockSpec((tm, 1), lambda i: (i, 0)),
            pl.BlockSpec((8, LANES), lambda i: (0, 0)),
            w_spec, w_spec, w_spec,
        ],
        out_specs=(
            pl.BlockSpec((tm, kw), lambda i: (i, 0)),
            r4_spec(kw),
            r16_spec(kw),
            pl.BlockSpec((tm, 2 * kw), lambda i: (i, 0)),
            r4_spec(2 * kw),
            r16_spec(2 * kw),
            pl.BlockSpec((tm, kw), lambda i: (i, 0)),
            pl.BlockSpec((tm, d), lambda i: (i, 0)),
        ),
        out_shape=out_shapes,
        scratch_shapes=[pltpu.VMEM((tm, LANES), F32),
                        pltpu.VMEM((tm, LANES), F32),
                        pltpu.VMEM((2, ATT_KV_HEADS, tm, ATT_HEAD_DIM), F32),
                        pltpu.VMEM((2, ATT_KV_HEADS * dil4, tm // dil4, ATT_HEAD_DIM), F32)],
        compiler_params=_params("arbitrary"),
        name="inproj",
    )(x2d, ada3, pos2d, rope_tab, w01, w2z, wkv)


def _restproj_kernel(h_ref, w_ref, o_ref):
    o_ref[...] = jnp.dot(h_ref[...], w_ref[...], preferred_element_type=F32).astype(o_ref.dtype)


def _restproj(h2d, w_in_bf16, col0):
    rows, d = h2d.shape
    tm, tn = REST_TM, REST_TN
    n_cols = w_in_bf16.shape[1] - col0
    assert col0 % tn == 0 and n_cols % tn == 0
    return pl.pallas_call(
        _restproj_kernel,
        grid=(rows // tm, n_cols // tn),
        in_specs=[pl.BlockSpec((tm, d), lambda i, j: (i, 0)),
                  pl.BlockSpec((d, tn), lambda i, j: (0, col0 // tn + j))],
        out_specs=pl.BlockSpec((tm, tn), lambda i, j: (i, j)),
        out_shape=jax.ShapeDtypeStruct((rows, n_cols), BF16),
        compiler_params=_params("arbitrary", "arbitrary"),
        name="restproj",
    )(h2d, w_in_bf16)


def _attn_kernel(q_ref, kv_ref, kvp_ref, o_ref, st_ref, onat_scr, *, dil):
    span = pl.program_id(1)
    g = pl.program_id(2)
    hd, kw = ATT_HEAD_DIM, ATT_KV_WIDTH
    groups_per_span = ATT_SPAN_SUBS // ATT_SUB
    first = (span * ATT_SPAN_SUBS + g * ATT_SUB - dil) < 0
    thr_first = jnp.where(first, 2 * ATT_BLOCK, 0).astype(jnp.int32)

    row = lax.broadcasted_iota(jnp.int32, (ATT_BLOCK, 2 * ATT_BLOCK), 0)
    col = lax.broadcasted_iota(jnp.int32, (ATT_BLOCK, 2 * ATT_BLOCK), 1)
    in_prev = col < ATT_BLOCK
    rel = col - row

    def mask_bias(thr):
        slack = jnp.where(in_prev, rel - thr, ATT_BLOCK - rel)
        return jnp.where(slack >= 0, 0.0, NEG).astype(F32)

    bias_inner = mask_bias(jnp.int32(0))
    bias_edge = mask_bias(thr_first)
    ones = jnp.ones((2 * ATT_BLOCK, hd), BF16)
    lane = lax.broadcasted_iota(jnp.int32, (ATT_BLOCK, LANES), 1)
    nt = (((1,), (1,)), ((), ()))

    def dst_rows(u):
        if dil <= ATT_SUB:
            start = (g * (ATT_SUB // dil) + u // dil) * (ATT_BLOCK * dil) + u % dil
        else:
            start = g * ATT_SUB + u
        if dil == 1:
            return pl.ds(pl.multiple_of(start, ATT_BLOCK), ATT_BLOCK)
        return pl.ds(start, ATT_BLOCK, stride=dil)

    for u0 in range(0, ATT_SUB, ATT_BATCH):
        units = [(u, h) for u in range(u0, u0 + ATT_BATCH) for h in range(ATT_KV_HEADS)]
        scores, vexts = {}, {}
        for u, h in units:
            ksl = slice(h * hd, (h + 1) * hd)
            vsl = slice(kw + h * hd, kw + (h + 1) * hd)
            q = q_ref[u, :, ksl]
            if u >= dil:
                k_prev, v_prev, bias = kv_ref[u - dil, :, ksl], kv_ref[u - dil, :, vsl], bias_inner
            else:
                k_prev, v_prev, bias = kvp_ref[u, :, ksl], kvp_ref[u, :, vsl], bias_edge
            k_cat = jnp.concatenate([k_prev, kv_ref[u, :, ksl]], axis=0)
            v_cat = jnp.concatenate([v_prev, kv_ref[u, :, vsl]], axis=0)
            vexts[u, h] = jnp.concatenate([v_cat, ones], axis=1)
            scores[u, h] = lax.dot_general(q, k_cat, nt, preferred_element_type=F32) + bias
        maxes = {un: jnp.max(scores[un], axis=-1, keepdims=True) for un in units}
        probs = {un: jnp.exp(scores[un] - maxes[un]).astype(BF16) for un in units}
        outs = {un: jnp.dot(probs[un], vexts[un], preferred_element_type=F32) for un in units}
        for u in range(u0, u0 + ATT_BATCH):
            dst = dst_rows(u)
            stats = jnp.zeros((ATT_BLOCK, LANES), F32)
            for h in range(ATT_KV_HEADS):
                den = outs[u, h][:, hd:]
                onat_scr[h, dst, :] = outs[u, h][:, :hd] / den
                stats = jnp.where(lane == h, maxes[u, h] + jnp.log(den[:, 0:1]), stats)
            st_ref[dst, :] = stats

    @pl.when(g == groups_per_span - 1)
    def _():
        for h in range(ATT_KV_HEADS):
            o_ref[:, h * hd:(h + 1) * hd] = onat_scr[h].astype(o_ref.dtype)


def _attention_group(q_sub, kv_sub, kv_prev, bsz, seq, dil):
    kw = ATT_KV_WIDTH
    span_rows = ATT_SPAN_SUBS * ATT_BLOCK
    gps = ATT_SPAN_SUBS // ATT_SUB
    n_prev = min(dil, ATT_SUB)
    assert ATT_SPAN_SUBS % ATT_SUB == 0 and ATT_SUB % ATT_BATCH == 0
    assert (ATT_SUB % dil == 0) if dil <= ATT_SUB else (dil == ATT_SPAN_SUBS and dil % ATT_SUB == 0)
    own_map = lambda b, s, g: (b, s * gps + g, 0, 0)
    prev_map = lambda b, s, g: (b, jnp.maximum(s * (ATT_SPAN_SUBS // n_prev) + g * (ATT_SUB // n_prev)
                                               - dil // n_prev, 0), 0, 0)
    prev_spec = pl.BlockSpec((None, n_prev, ATT_BLOCK, 2 * kw), prev_map)
    return pl.pallas_call(
        functools.partial(_attn_kernel, dil=dil),
        grid=(bsz, seq // span_rows, gps),
        in_specs=[
            pl.BlockSpec((None, ATT_SUB, ATT_BLOCK, kw), own_map),
            pl.BlockSpec((None, ATT_SUB, ATT_BLOCK, 2 * kw), own_map),
            prev_spec,
        ],
        out_specs=(
            pl.BlockSpec((None, span_rows, kw), lambda b, s, g: (b, s, 0)),
            pl.BlockSpec((None, span_rows, LANES), lambda b, s, g: (b, s, 0)),
        ),
        out_shape=(jax.ShapeDtypeStruct((bsz, seq, kw), BF16),
                   jax.ShapeDtypeStruct((bsz, seq, LANES), F32)),
        scratch_shapes=[pltpu.VMEM((ATT_KV_HEADS, span_rows, ATT_HEAD_DIM), F32)],
        compiler_params=_params("arbitrary", "arbitrary", "arbitrary"),
        name=f"attn_d{dil}",
    )(q_sub, kv_sub, kv_prev)


def _mpre_kernel(xm_ref, halo_ref, cw_ref, cb_ref, wq_ref, wk_ref, wkt_ref, wv_ref, wif_ref, bif_ref,
                 qm_ref, kt_ref, vm_ref, xc_ref, g_ref, ext_scr):
    i = pl.program_id(1)
    tm = xm_ref.shape[0]
    xm = xm_ref[...]
    halo = halo_ref[...].astype(F32)
    ext_scr[0:CONV_HALO, :] = jnp.where(i > 0, halo, 0.0)
    ext_scr[CONV_HALO:CONV_HALO + tm, :] = xm.astype(F32)
    acc = jnp.zeros((tm, M_WIDTH), F32) + cb_ref[...]
    for k in range(M_CONV):
        start = CONV_HALO - (M_CONV - 1) + k
        acc = acc + cw_ref[k:k + 1, :] * ext_scr[pl.ds(start, tm), :]
    xc = _silu(acc)
    xc_b = xc.astype(BF16)
    xc_ref[...] = xc_b
    gates = jnp.zeros((tm, LANES), F32) + bif_ref[...]
    for h in range(M_HEADS):
        sl = slice(h * M_HEAD_DIM, (h + 1) * M_HEAD_DIM)
        qh = jnp.dot(xc_b[:, sl], wq_ref[h], preferred_element_type=F32).astype(BF16)
        kh = jnp.dot(xc_b[:, sl], wk_ref[h], preferred_element_type=F32).astype(BF16)
        vh = jnp.dot(xm[:, sl], wv_ref[h], preferred_element_type=F32).astype(BF16)
        qm_ref[:, sl] = qh
        vm_ref[:, sl] = vh
        kt_ref[h] = lax.dot_general(wkt_ref[h], xc_b[:, sl], (((1,), (1,)), ((), ())),
                                    preferred_element_type=F32).astype(BF16)
        gates = gates + jnp.dot(qh, wif_ref[0, sl, :], preferred_element_type=F32)
        gates = gates + jnp.dot(kh, wif_ref[1, sl, :], preferred_element_type=F32)
        gates = gates + jnp.dot(vh, wif_ref[2, sl, :], preferred_element_type=F32)
    lane = lax.broadcasted_iota(jnp.int32, gates.shape, 1)
    log_f = jnp.minimum(gates, 0.0) - jnp.log1p(jnp.exp(-jnp.abs(gates)))
    g_ref[...] = jnp.where(lane >= M_HEADS, log_f, gates)


def _mpre(rest3, conv_w, conv_b, wq, wk, wkt, wv, wif3, bif):
    bsz, seq, _ = rest3.shape
    tm = MPRE_TM
    halo_blocks = tm // CONV_HALO
    full = lambda shape: pl.BlockSpec(shape, lambda b, i: (0,) * len(shape))
    big = jax.ShapeDtypeStruct((bsz, seq, M_WIDTH), BF16)
    return pl.pallas_call(
        _mpre_kernel,
        grid=(bsz, seq // tm),
        in_specs=[
            pl.BlockSpec((None, tm, M_WIDTH), lambda b, i: (b, i, 0)),
            pl.BlockSpec((None, CONV_HALO, M_WIDTH),
                         lambda b, i: (b, jnp.maximum(i * halo_blocks - 1, 0), 0)),
            full((M_CONV, M_WIDTH)), full((1, M_WIDTH)),
            full((M_HEADS, M_HEAD_DIM, M_HEAD_DIM)), full((M_HEADS, M_HEAD_DIM, M_HEAD_DIM)),
            full((M_HEADS, M_HEAD_DIM, M_HEAD_DIM)), full((M_HEADS, M_HEAD_DIM, M_HEAD_DIM)),
            full((3, M_WIDTH, LANES)), full((1, LANES)),
        ],
        out_specs=(
            pl.BlockSpec((None, tm, M_WIDTH), lambda b, i: (b, i, 0)),
            pl.BlockSpec((None, M_HEADS, M_HEAD_DIM, tm), lambda b, i: (b, 0, 0, i)),
            pl.BlockSpec((None, tm, M_WIDTH), lambda b, i: (b, i, 0)),
            pl.BlockSpec((None, tm, M_WIDTH), lambda b, i: (b, i, 0)),
            pl.BlockSpec((None, tm, LANES), lambda b, i: (b, i, 0)),
        ),
        out_shape=(big, jax.ShapeDtypeStruct((bsz, M_HEADS, M_HEAD_DIM, seq), BF16), big, big,
                   jax.ShapeDtypeStruct((bsz, seq, LANES), F32)),
        scratch_shapes=[pltpu.VMEM((tm + CONV_HALO, M_WIDTH), F32)],
        compiler_params=_params("arbitrary", "arbitrary"),
        name="mpre",
    )(rest3, rest3, conv_w, conv_b, wq, wk, wkt, wv, wif3, bif)


def _mlstm_kernel(q_ref, kt_ref, v_ref, g_ref, om_ref, zm_ref, xc_ref, nw_ref, sk_ref,
                  y_ref, c_scr, m_scr):
    @pl.when(pl.program_id(1) == 0)
    def _():
        c_scr[...] = jnp.zeros_like(c_scr)
        m_scr[...] = jnp.zeros_like(m_scr)

    L, E = M_CHUNK, M_HEAD_DIM
    row = lax.broadcasted_iota(jnp.int32, (L, L), 0)
    col = lax.broadcasted_iota(jnp.int32, (L, L), 1)
    causal = col <= row
    tri = causal.astype(F32)
    ones_blk = jnp.ones((L, LANES), BF16)

    def chunk(c):
        rows = pl.ds(c * L, L)
        heads = range(M_HEADS)
        hsl = [slice(h * E, (h + 1) * E) for h in heads]
        g = g_ref[rows, :]
        bm = jnp.dot(tri, g, precision=lax.Precision.HIGHEST,
                     preferred_element_type=F32)
        g_t = g.T
        bm_t = bm.T

        b_col = [bm[:, M_HEADS + h:M_HEADS + h + 1] for h in heads]
        li_row = [g_t[h:h + 1, :] for h in heads]
        b_row = [bm_t[M_HEADS + h:M_HEADS + h + 1, :] for h in heads]
        b_last = [b_col[h][L - 1:L, :] for h in heads]
        dmat = [jnp.where(causal, b_col[h] - b_row[h] + li_row[h], NEG) for h in heads]
        m_intra = [jnp.max(dmat[h], axis=-1, keepdims=True) for h in heads]
        e_intra = [jnp.exp(dmat[h] - m_intra[h]) for h in heads]
        w_src = [b_last[h] - b_row[h] + li_row[h] for h in heads]
        a = [jnp.max(w_src[h], axis=1, keepdims=True) for h in heads]
        m_prev = [m_scr[h:h + 1, 0:1] for h in heads]
        m_new = [jnp.maximum(b_last[h] + m_prev[h], a[h]) for h in heads]
        decay = [jnp.exp(b_last[h] + m_prev[h] - m_new[h]) for h in heads]
        e_key = [jnp.exp(w_src[h] - m_new[h]).astype(BF16) for h in heads]
        g_col = [b_col[h] + m_prev[h] for h in heads]
        m_t = [jnp.maximum(g_col[h], m_intra[h]) for h in heads]
        w_intra = [jnp.exp(m_intra[h] - m_t[h]) for h in heads]
        w_state = [jnp.exp(g_col[h] - m_t[h]) for h in heads]
        floor = [jnp.exp(-m_t[h]) for h in heads]

        q = [q_ref[rows, hsl[h]] for h in heads]
        kt = [kt_ref[h, :, c * L:(c + 1) * L] for h in heads]
        v_ext = [jnp.concatenate([v_ref[rows, hsl[h]], ones_blk], axis=1) for h in heads]
        qk = [jnp.dot(q[h], kt[h], preferred_element_type=F32) for h in heads]
        c_prev = [c_scr[h] for h in heads]
        y_state = [jnp.dot(q[h], c_prev[h].astype(BF16), preferred_element_type=F32) for h in heads]
        pmat = [(qk[h] * e_intra[h]).astype(BF16) for h in heads]
        y_intra = [jnp.dot(pmat[h], v_ext[h], preferred_element_type=F32) for h in heads]
        c_loc = [jnp.dot(kt[h] * e_key[h], v_ext[h], preferred_element_type=F32) for h in heads]

        y_all = [w_intra[h] * y_intra[h] + w_state[h] * y_state[h] for h in heads]
        for h in heads:
            c_scr[h] = decay[h] * c_prev[h] + c_loc[h]
            m_scr[h:h + 1, :] = jnp.broadcast_to(m_new[h], (1, LANES))
        inv_den = [1.0 / jnp.maximum(jnp.abs(y_all[h][:, E:]), floor[h]) for h in heads]
        hh = [jnp.concatenate([y_all[h][:, j * LANES:(j + 1) * LANES] * inv_den[h]
                               for j in range(E // LANES)], axis=1) for h in heads]

        hm = [_sigmoid(om_ref[rows, hsl[h]]).astype(F32) * hh[h] for h in heads]
        mu = [jnp.mean(hm[h], axis=-1, keepdims=True) for h in heads]
        hc = [hm[h] - mu[h] for h in heads]
        var = [jnp.mean(hc[h] * hc[h], axis=-1, keepdims=True) for h in heads]
        for h in heads:
            hn = hc[h] * lax.rsqrt(var[h] + LN_EPS) * nw_ref[:, hsl[h]]
            y_ref[rows, hsl[h]] = ((hn + sk_ref[:, hsl[h]] * xc_ref[rows, hsl[h]].astype(F32)).astype(BF16)
                                   * _silu(zm_ref[rows, hsl[h]]))

    for c in range(q_ref.shape[0] // L):
        chunk(c)


def _mlstm(qm, kmt, vm, gates, rest3, xc, norm_w, skip):
    bsz, seq, _ = qm.shape
    tm = MLSTM_TM
    row_spec = pl.BlockSpec((None, tm, M_WIDTH), lambda b, i: (b, i, 0))
    return pl.pallas_call(
        _mlstm_kernel,
        grid=(bsz, seq // tm),
        in_specs=[
            row_spec,
            pl.BlockSpec((None, M_HEADS, M_HEAD_DIM, tm), lambda b, i: (b, 0, 0, i)),
            row_spec,
            pl.BlockSpec((None, tm, LANES), lambda b, i: (b, i, 0)),
            pl.BlockSpec((None, tm, M_WIDTH), lambda b, i: (b, i, 2)),
            pl.BlockSpec((None, tm, M_WIDTH), lambda b, i: (b, i, 1)),
            row_spec,
            pl.BlockSpec((1, M_WIDTH), lambda b, i: (0, 0)),
            pl.BlockSpec((1, M_WIDTH), lambda b, i: (0, 0)),
        ],
        out_specs=row_spec,
        out_shape=jax.ShapeDtypeStruct((bsz, seq, M_WIDTH), BF16),
        scratch_shapes=[pltpu.VMEM((M_HEADS, M_HEAD_DIM, M_HEAD_DIM + LANES), F32),
                        pltpu.VMEM((8, LANES), F32)],
        compiler_params=_params("arbitrary", "arbitrary"),
        name="mlstm",
    )(qm, kmt, vm, gates, rest3, rest3, xc, norm_w, skip)


def _post_kernel(o0_ref, o1_ref, o2_ref, s0_ref, s1_ref, s2_ref, za_ref, ym_ref, ga_ref, gm_ref,
                 x_ref, ada_ref, wpa_ref, wpm_ref, wout_ref, lng_ref, lnb_ref, out_ref,
                 *, alpha, d_model):
    hd = ATT_HEAD_DIM
    tm = x_ref.shape[0]
    gate = ada_ref[:, 2 * d_model:3 * d_model]
    for part in range(POST_PARTS):
        rs = slice(part * (tm // POST_PARTS), (part + 1) * (tm // POST_PARTS))
        parts = []
        for h in range(ATT_KV_HEADS):
            sl = slice(h * hd, (h + 1) * hd)
            l0 = s0_ref[rs, h:h + 1]
            l1 = s1_ref[rs, h:h + 1]
            l2 = s2_ref[rs, h:h + 1]
            mx = jnp.maximum(jnp.maximum(l0, l1), l2)
            e0 = jnp.exp(l0 - mx)
            e1 = jnp.exp(l1 - mx)
            e2 = jnp.exp(l2 - mx)
            tot = e0 + e1 + e2
            o = ((e0 / tot) * o0_ref[rs, sl].astype(F32) + (e1 / tot) * o1_ref[rs, sl].astype(F32)
                 + (e2 / tot) * o2_ref[rs, sl].astype(F32))
            parts.append(o.astype(BF16) * _silu(za_ref[rs, sl]))
        att_in = jnp.concatenate(parts, axis=1)
        y_att = jnp.dot(att_in, wpa_ref[...], preferred_element_type=F32)
        y_m = jnp.dot(ym_ref[rs, :], wpm_ref[...], preferred_element_type=F32)
        merged = (_sigmoid(ga_ref[rs, :]) * y_att.astype(BF16)
                  + _sigmoid(gm_ref[rs, :]) * y_m.astype(BF16))
        out = jnp.dot(merged, wout_ref[...], preferred_element_type=F32)
        res = alpha * x_ref[rs, :] + gate * out
        mu = jnp.mean(res, axis=-1, keepdims=True)
        rc = res - mu
        var = jnp.mean(rc * rc, axis=-1, keepdims=True)
        out_ref[rs, :] = (rc * lax.rsqrt(var + LN_EPS) * lng_ref[...] + lnb_ref[...]).astype(out_ref.dtype)


def _post(o_list, st_list, za2, rest2, ymin2, x2d, ada3, wpa, wpm, wout, ln_g, ln_b, seq, alpha):
    rows, d = x2d.shape
    tm = POST_TM
    tiles_per_seq = seq // tm
    kw = ATT_KV_WIDTH
    o_spec = pl.BlockSpec((tm, kw), lambda i: (i, 0))
    s_spec = pl.BlockSpec((tm, LANES), lambda i: (i, 0))
    const = lambda shape: pl.BlockSpec(shape, lambda i: (0,) * len(shape))
    return pl.pallas_call(
        functools.partial(_post_kernel, alpha=alpha, d_model=d),
        grid=(rows // tm,),
        in_specs=[
            o_spec, o_spec, o_spec, s_spec, s_spec, s_spec,
            pl.BlockSpec((tm, kw), lambda i: (i, 0)),
            pl.BlockSpec((tm, M_WIDTH), lambda i: (i, 0)),
            pl.BlockSpec((tm, d), lambda i: (i, 3)),
            pl.BlockSpec((tm, d), lambda i: (i, 4)),
            pl.BlockSpec((tm, d), lambda i: (i, 0)),
            pl.BlockSpec((None, 1, 3 * d), lambda i: (i // tiles_per_seq, 0, 0)),
            const((kw, d)), const((M_WIDTH, d)), const((d, d)),
            const((1, d)), const((1, d)),
        ],
        out_specs=pl.BlockSpec((tm, d), lambda i: (i, 0)),
        out_shape=jax.ShapeDtypeStruct((rows, d), x2d.dtype),
        compiler_params=_params("arbitrary"),
        name="post",
    )(*o_list, *st_list, za2, ymin2, rest2, rest2, x2d, ada3, wpa, wpm, wout, ln_g, ln_b)


def _layer(x, c, positions, w_ada, b_ada, w_in, conv_w, conv_b, w_qm, w_km, w_vm,
           w_if, b_if, mh_norm_w, skip_m, w_pa, w_pm, w_out, ln_g, ln_b, alpha):
    bsz, seq, d = x.shape
    rows = bsz * seq
    x2d = x.reshape(rows, d)

    ada3 = _ada(c, w_ada, b_ada).reshape(bsz, 1, 3 * d)

    half = ATT_HEAD_DIM // 2
    inv = jnp.power(ROPE_THETA, -jnp.arange(half, dtype=F32) / half)
    sign = jnp.concatenate([-jnp.ones((half,), F32), jnp.ones((half,), F32)])
    rope_tab = jnp.zeros((8, LANES), F32).at[0].set(jnp.concatenate([inv, inv])).at[1].set(sign)

    w_in_b = w_in.astype(BF16)
    q0, q1r, q2r, kv, kv4r, kv16r, za, h2d = _inproj(x2d, ada3, positions.reshape(rows, 1), rope_tab,
                                                     w_in_b, bsz, seq)
    rest = _restproj(h2d, w_in_b, 6 * ATT_KV_WIDTH)

    n_sub = seq // ATT_BLOCK
    o_list, st_list = [], []
    for q_g, kv_g, (_, dil) in zip((q0, q1r, q2r), (kv, kv4r, kv16r), ATT_GROUPS):
        kv_sub = kv_g.reshape(bsz, n_sub, ATT_BLOCK, 2 * ATT_KV_WIDTH)
        o_g, st_g = _attention_group(q_g.reshape(bsz, n_sub, ATT_BLOCK, ATT_KV_WIDTH), kv_sub, kv_sub,
                                     bsz, seq, dil)
        o_list.append(o_g.reshape(rows, ATT_KV_WIDTH))
        st_list.append(st_g.reshape(rows, LANES))

    rest3 = rest.reshape(bsz, seq, rest.shape[1])
    n_gate = 2 * M_HEADS
    wif3 = jnp.zeros((3, M_WIDTH, LANES), BF16).at[:, :, :n_gate].set(
        w_if.astype(BF16).reshape(3, M_WIDTH, n_gate))
    bif = jnp.zeros((1, LANES), F32).at[0, :n_gate].set(b_if.astype(F32))
    wk = (w_km * (M_HEAD_DIM ** -0.5)).astype(BF16)
    qm, kmt, vm, xc, gates = _mpre(rest3, conv_w.astype(F32), conv_b.astype(F32).reshape(1, M_WIDTH),
                                   w_qm.astype(BF16), wk, wk.transpose(0, 2, 1), w_vm.astype(BF16), wif3, bif)
    ymin = _mlstm(qm, kmt, vm, gates, rest3, xc,
                  mh_norm_w.astype(F32).reshape(1, M_WIDTH), skip_m.astype(F32).reshape(1, M_WIDTH))

    out = _post(o_list, st_list, za, rest, ymin.reshape(rows, M_WIDTH), x2d, ada3,
                w_pa.astype(BF16), w_pm.astype(BF16), w_out.astype(BF16),
                ln_g.astype(F32).reshape(1, d), ln_b.astype(F32).reshape(1, d), seq, alpha)
    return out.reshape(bsz, seq, d)


def kernel(x, c, positions, w_ada, b_ada, w_in, conv_w, conv_b, w_qm, w_km, w_vm, w_if, b_if,
           mh_norm_w, skip_m, w_pa, w_pm, w_out, ln_g, ln_b):
    depth = w_ada.shape[0]
    alpha = (2.0 * depth) ** 0.25
    for l in range(depth):
        x = _layer(x, c, positions, w_ada[l], b_ada[l], w_in[l], conv_w[l], conv_b[l],
                   w_qm[l], w_km[l], w_vm[l], w_if[l], b_if[l], mh_norm_w[l], skip_m[l],
                   w_pa[l], w_pm[l], w_out[l], ln_g[l], ln_b[l], alpha)
    return x
```
